```python
import functools
import jax, jax.numpy as jnp
from jax import lax
import numpy as np

D_MODEL = 1024
BATCH = 8
SEQ = 4096
DEPTH = 1
DEC_BATCH = 16
DEC_SEQ = 16
PAST_LEN = 1024

CHUNK = 64
WINDOW = 128
WINDOW_CHUNKS = WINDOW // CHUNK
HEAD_DIM = 64
N_HEADS = 16
N_KV_HEADS = 4
Q_PER_KV = N_HEADS // N_KV_HEADS
D_ATTN = N_HEADS * HEAD_DIM
D_KV = N_KV_HEADS * HEAD_DIM
SCALE = HEAD_DIM ** -0.5
SSM_HEADS = 16
SSM_HEAD_DIM = 64
SSM_GROUPS = 4
HEADS_PER_GROUP = SSM_HEADS // SSM_GROUPS
SSM_STATE = 128
D_SSM = SSM_HEADS * SSM_HEAD_DIM
CONV_WIDTH = 4
CONV_DIM = D_SSM + 2 * SSM_GROUPS * SSM_STATE
D_MIX = D_ATTN + D_SSM
IN_SPLITS = (D_ATTN, D_ATTN + D_KV, D_ATTN + 2 * D_KV, D_ATTN + 2 * D_KV + D_SSM,
             D_ATTN + 2 * D_KV + D_SSM + CONV_DIM)
D_IN = D_ATTN + 2 * D_KV + D_SSM + CONV_DIM + SSM_HEADS
D_FF = -(-8 * D_MODEL // (3 * 256)) * 256
EPS = 1e-6

kernel_name = 'hymba_swa_sink_ssd_stream_step'


def rmsnorm(x, g):
    xf = x.astype(jnp.float32)
    xf = xf * lax.rsqrt(jnp.mean(xf * xf, axis=-1, keepdims=True) + EPS)
    return xf.astype(x.dtype) * g


def gated_group_rmsnorm(y, z, w):
    g = y.astype(jnp.float32) * jax.nn.silu(z.astype(jnp.float32))
    gg = g.reshape(*g.shape[:-1], SSM_GROUPS, D_SSM // SSM_GROUPS)
    gg = gg * lax.rsqrt(jnp.mean(gg * gg, axis=-1, keepdims=True) + EPS)
    return gg.reshape(g.shape) * w.astype(jnp.float32)


def sink_softmax(scores, sinks):
    sink = sinks.astype(jnp.float32).reshape(N_KV_HEADS, Q_PER_KV, 1, 1)
    m = jnp.maximum(scores.max(axis=-1, keepdims=True), sink)
    p = jnp.exp(scores - m)
    return p / (p.sum(axis=-1, keepdims=True) + jnp.exp(sink - m))


def window_attention_prompt(q, k, v, sinks):
    b, s = q.shape[:2]
    nc = s // CHUNK
    qc = q.reshape(b, nc, CHUNK, N_KV_HEADS, Q_PER_KV, HEAD_DIM)
    pad = ((0, 0), (WINDOW, 0), (0, 0), (0, 0))
    kp = jnp.pad(k, pad).reshape(b, nc + WINDOW_CHUNKS, CHUNK, N_KV_HEADS, HEAD_DIM)
    vp = jnp.pad(v, pad).reshape(b, nc + WINDOW_CHUNKS, CHUNK, N_KV_HEADS, HEAD_DIM)
    kb = jnp.concatenate([kp[:, j:j + nc] for j in range(WINDOW_CHUNKS + 1)], axis=2)
    vb = jnp.concatenate([vp[:, j:j + nc] for j in range(WINDOW_CHUNKS + 1)], axis=2)
    key_chunk = jnp.arange(nc)[:, None] + jnp.arange(-WINDOW_CHUNKS, 1)[None, :]
    valid = jnp.repeat(key_chunk >= 0, CHUNK, axis=1)
    scores = jnp.einsum('bcqkgd,bcskd->bckgqs', qc, kb, preferred_element_type=jnp.float32) * SCALE
    scores = jnp.where(valid[None, :, None, None, None, :], scores, -jnp.inf)
    p = sink_softmax(scores, sinks).astype(vb.dtype)
    o = jnp.einsum('bckgqs,bcskd->bcqkgd', p, vb)
    return o.reshape(b, s, D_ATTN)


def window_attention_sample(q, k, v, past_k, past_v, sinks):
    b, t = q.shape[:2]
    kk = jnp.concatenate([past_k.astype(k.dtype), k], axis=1)
    vv = jnp.concatenate([past_v.astype(v.dtype), v], axis=1)
    qg = q.reshape(b, t, N_KV_HEADS, Q_PER_KV, HEAD_DIM)
    scores = jnp.einsum('btkgd,bskd->bkgts', qg, kk, preferred_element_type=jnp.float32) * SCALE
    p = sink_softmax(scores, sinks).astype(vv.dtype)
    o = jnp.einsum('bkgts,bskd->btkgd', p, vv)
    return o.reshape(b, t, D_ATTN)


def causal_conv(xbc, buf, conv_w, conv_b):
    t = xbc.shape[1]
    xx = jnp.concatenate([buf.astype(xbc.dtype), xbc], axis=1)
    y = sum((xx[:, i:i + t] * conv_w[i] for i in range(CONV_WIDTH)), conv_b)
    return jax.nn.silu(y), xx[:, t:]


def ssd_scan(x, dt, a_log, bm, cm, h0, chunk):
    b, L = x.shape[:2]
    nc = L // chunk
    R = HEADS_PER_GROUP
    a = -jnp.exp(a_log.astype(jnp.float32))
    da = (dt * a).reshape(b, nc, chunk, SSM_GROUPS, R)
    xd = (x.astype(jnp.float32) * dt[..., None]).reshape(b, nc, chunk, SSM_GROUPS, R, SSM_HEAD_DIM)
    bc = bm.astype(jnp.float32).reshape(b, nc, chunk, SSM_GROUPS, SSM_STATE)
    cc = cm.astype(jnp.float32).reshape(b, nc, chunk, SSM_GROUPS, SSM_STATE)
    cs = jnp.cumsum(da, axis=2)
    causal = jnp.tril(jnp.ones((chunk, chunk), dtype=bool))[:, :, None, None]
    lmat = jnp.exp(jnp.where(causal, cs[:, :, :, None] - cs[:, :, None, :], -jnp.inf))
    cb = jnp.einsum('bclgn,bcsgn->bclsg', cc, bc)
    y_diag = jnp.einsum('bclsgr,bcsgrp->bclgrp', cb[..., None] * lmat, xd)
    decay = jnp.exp(cs[:, :, -1:] - cs)
    states = jnp.einsum('bclgn,bclgrp->bcgrpn', bc, xd * decay[..., None])
    chunk_decay = jnp.exp(cs[:, :, -1])

    def step(h, inp):
        s_c, d_c = inp
        return d_c[..., None, None] * h + s_c, h

    h_init = h0.astype(jnp.float32).reshape(b, SSM_GROUPS, R, SSM_HEAD_DIM, SSM_STATE)
    h_last, h_prev = lax.scan(step, h_init, (jnp.moveaxis(states, 1, 0), jnp.moveaxis(chunk_decay, 1, 0)))
    h_prev = jnp.moveaxis(h_prev, 0, 1)
    y_off = jnp.einsum('bclgn,bcgrpn->bclgrp', cc, h_prev) * jnp.exp(cs)[..., None]
    y = (y_diag + y_off).reshape(b, L, SSM_HEADS, SSM_HEAD_DIM)
    return y, h_last.reshape(b, SSM_HEADS, SSM_HEAD_DIM, SSM_STATE)


def hybrid_mixer(u, past_k, past_v, conv_buf, h0, w_in, conv_w, conv_b, dt_bias, a_log, d_skip,
                 ssm_norm_w, sinks, w_out):
    b, t, _ = u.shape
    q, k, v, z, xbc, dt_raw = jnp.split(u @ w_in, IN_SPLITS, axis=-1)
    q = q.reshape(b, t, N_HEADS, HEAD_DIM)
    k = k.reshape(b, t, N_KV_HEADS, HEAD_DIM)
    v = v.reshape(b, t, N_KV_HEADS, HEAD_DIM)
    if past_k is None:
        attn = window_attention_prompt(q, k, v, sinks)
        new_k, new_v = k[:, -WINDOW:], v[:, -WINDOW:]
        conv_buf = jnp.zeros((b, CONV_WIDTH - 1, CONV_DIM), u.dtype)
        h0 = jnp.zeros((b, SSM_HEADS, SSM_HEAD_DIM, SSM_STATE), jnp.float32)
        ssd_chunk = CHUNK
    else:
        attn = window_attention_sample(q, k, v, past_k, past_v, sinks)
        new_k, new_v = k, v
        ssd_chunk = t
    xbc, new_conv = causal_conv(xbc, conv_buf, conv_w, conv_b)
    xs, bm, cm = jnp.split(xbc, [D_SSM, D_SSM + SSM_GROUPS * SSM_STATE], axis=-1)
    xs = xs.reshape(b, t, SSM_HEADS, SSM_HEAD_DIM)
    bm = bm.reshape(b, t, SSM_GROUPS, SSM_STATE)
    cm = cm.reshape(b, t, SSM_GROUPS, SSM_STATE)
    dt = jax.nn.softplus(dt_raw.astype(jnp.float32) + dt_bias.astype(jnp.float32))
    y, h_last = ssd_scan(xs, dt, a_log, bm, cm, h0, ssd_chunk)
    y = y + xs.astype(jnp.float32) * d_skip.astype(jnp.float32)[:, None]
    y = gated_group_rmsnorm(y.reshape(b, t, D_SSM), z, ssm_norm_w).astype(u.dtype)
    out = jnp.concatenate([attn.astype(u.dtype), y], axis=-1) @ w_out
    return out, new_k, new_v, new_conv, h_last.astype(u.dtype)


def swiglu(u, w_gate, w_up, w_down):
    return (jax.nn.silu(u @ w_gate) * (u @ w_up)) @ w_down


def setup_inputs(seed: int = 0) -> dict:
    key = jax.random.key(seed)
    ks = jax.random.split(key, 24)
    f32 = jnp.float32
    nrm = lambda k, shape, s: jax.random.normal(k, shape, f32) * s
    dt0 = jnp.exp(jax.random.uniform(ks[10], (DEPTH, SSM_HEADS), f32, np.log(1e-3), np.log(1e-1)))
    return {
        'x_prompt': nrm(ks[0], (BATCH, SEQ, D_MODEL), 1.0),
        'x_sample': nrm(ks[1], (DEC_BATCH, DEC_SEQ, D_MODEL), 1.0),
        'cache_k': nrm(ks[2], (DEPTH, DEC_BATCH, WINDOW, N_KV_HEADS, HEAD_DIM), 1.0),
        'cache_v': nrm(ks[3], (DEPTH, DEC_BATCH, WINDOW, N_KV_HEADS, HEAD_DIM), 1.0),
        'state_conv': nrm(ks[4], (DEPTH, DEC_BATCH, CONV_WIDTH - 1, CONV_DIM), 1.0),
        'state_ssm': nrm(ks[5], (DEPTH, DEC_BATCH, SSM_HEADS, SSM_HEAD_DIM, SSM_STATE), 0.5),
        'ln1_g': 1.0 + nrm(ks[6], (DEPTH, D_MODEL), 0.01),
        'w_in': nrm(ks[7], (DEPTH, D_MODEL, D_IN), D_MODEL ** -0.5),
        'conv_w': nrm(ks[8], (DEPTH, CONV_WIDTH, CONV_DIM), CONV_WIDTH ** -0.5),
        'conv_b': nrm(ks[9], (DEPTH, CONV_DIM), 0.01),
        'dt_bias': dt0 + jnp.log(-jnp.expm1(-dt0)),
        'a_log': jnp.log(jax.random.uniform(ks[11], (DEPTH, SSM_HEADS), f32, 1.0, 16.0)),
        'd_skip': 1.0 + nrm(ks[12], (DEPTH, SSM_HEADS), 0.01),
        'ssm_norm_w': 1.0 + nrm(ks[13], (DEPTH, D_SSM), 0.01),
        'sinks': nrm(ks[14], (DEPTH, N_HEADS), 0.5),
        'w_out': nrm(ks[15], (DEPTH, D_MIX, D_MODEL), D_MIX ** -0.5),
        'ln2_g': 1.0 + nrm(ks[16], (DEPTH, D_MODEL), 0.01),
        'w_gate': nrm(ks[17], (DEPTH, D_MODEL, D_FF), D_MODEL ** -0.5),
        'w_up': nrm(ks[18], (DEPTH, D_MODEL, D_FF), D_MODEL ** -0.5),
        'w_down': nrm(ks[19], (DEPTH, D_FF, D_MODEL), D_FF ** -0.5),
        'final_g': 1.0 + nrm(ks[20], (D_MODEL,), 0.01),
    }


def reference(x_prompt, x_sample, cache_k, cache_v, state_conv, state_ssm, ln1_g, w_in, conv_w, conv_b,
              dt_bias, a_log, d_skip, ssm_norm_w, sinks, w_out, ln2_g, w_gate, w_up, w_down, final_g):
    hp, hs = x_prompt, x_sample
    kp_l, vp_l, cp_l, sp_l, ks_l, vs_l, cs_l, ss_l = [], [], [], [], [], [], [], []
    for l in range(DEPTH):
        mixer = functools.partial(hybrid_mixer, w_in=w_in[l], conv_w=conv_w[l], conv_b=conv_b[l],
                                  dt_bias=dt_bias[l], a_log=a_log[l], d_skip=d_skip[l],
                                  ssm_norm_w=ssm_norm_w[l], sinks=sinks[l], w_out=w_out[l])
        mp, kp, vp, cp, sp = mixer(rmsnorm(hp, ln1_g[l]), None, None, None, None)
        ms, kn, vn, cn, sn = mixer(rmsnorm(hs, ln1_g[l]), cache_k[l], cache_v[l], state_conv[l], state_ssm[l])
        hp = hp + mp
        hs = hs + ms
        hp = hp + swiglu(rmsnorm(hp, ln2_g[l]), w_gate[l], w_up[l], w_down[l])
        hs = hs + swiglu(rmsnorm(hs, ln2_g[l]), w_gate[l], w_up[l], w_down[l])
        kp_l.append(kp); vp_l.append(vp); cp_l.append(cp); sp_l.append(sp)
        ks_l.append(kn); vs_l.append(vn); cs_l.append(cn); ss_l.append(sn)
    y_prompt = rmsnorm(hp, final_g)
    y_sample = rmsnorm(hs, final_g)
    return (y_prompt, y_sample, jnp.stack(kp_l), jnp.stack(vp_l), jnp.stack(cp_l), jnp.stack(sp_l),
            jnp.stack(ks_l), jnp.stack(vs_l), jnp.stack(cs_l), jnp.stack(ss_l))
```

```python
import functools

import jax
import jax.numpy as jnp
from jax import lax
from jax.experimental import pallas as pl
from jax.experimental.pallas import tpu as pltpu

F32 = jnp.float32
BF16 = jnp.bfloat16

D_MODEL = 1024
CHUNK = 64
WINDOW = 128
HEAD_DIM = 64
N_HEADS = 16
N_KV_HEADS = 4
D_ATTN = N_HEADS * HEAD_DIM
D_KV = N_KV_HEADS * HEAD_DIM
SCALE = HEAD_DIM ** -0.5
SSM_HEADS = 16
SSM_HEAD_DIM = 64
SSM_GROUPS = 4
SSM_STATE = 128
D_SSM = SSM_HEADS * SSM_HEAD_DIM
D_BC = SSM_GROUPS * SSM_STATE
CONV_WIDTH = 4
CONV_DIM = D_SSM + 2 * D_BC
D_MIX = D_ATTN + D_SSM
EPS = 1e-6

LANES = 128
BLK = 2 * CHUNK
DT_PAD = LANES
O_Q, O_K, O_V, O_Z, O_X, O_DT = 0, 1024, 1280, 1536, 2560, 4608
D_IN_PAD = O_DT + DT_PAD
HIST = 8
VMEM_LIMIT = 56 * 1024 * 1024


def _rms(x, g):
    ms = jnp.mean(x * x, axis=-1, keepdims=True)
    return (x * lax.rsqrt(ms + EPS)) * g


def _silu(x):
    return x / (1.0 + jnp.exp(-x))


def _softplus(x):
    return jnp.maximum(x, 0.0) + jnp.log1p(jnp.exp(-jnp.abs(x)))


def _dot(a, b):
    return jnp.dot(a, b, preferred_element_type=F32)


def _dot_nt(a, b):
    return lax.dot_general(a, b, (((1,), (1,)), ((), ())), preferred_element_type=F32)


def _left_half(shape):
    return lax.broadcasted_iota(jnp.int32, shape, len(shape) - 1) < HEAD_DIM


def _dup_head(x2, half):
    rolled = pltpu.roll(x2, HEAD_DIM, axis=1)
    left = _left_half(x2.shape)
    return jnp.where(left, x2, rolled) if half == 0 else jnp.where(left, rolled, x2)


def _attn_block(q_blk, kwin, vwin, mask, sinks_ref):
    left = _left_half((BLK, LANES))
    mask4 = jnp.concatenate([mask] * 4, axis=0)
    zero = jnp.zeros((BLK, LANES), BF16)
    outs = []
    for kv in range(N_KV_HEADS):
        sl = slice((kv // 2) * LANES, (kv // 2 + 1) * LANES)
        kdup = _dup_head(kwin[:, sl], kv % 2).astype(BF16)
        vdup = _dup_head(vwin[:, sl], kv % 2).astype(BF16)
        qa = q_blk[:, (2 * kv) * LANES:(2 * kv + 1) * LANES]
        qb = q_blk[:, (2 * kv + 1) * LANES:(2 * kv + 2) * LANES]
        lhs = jnp.concatenate([jnp.where(left, qa, zero), jnp.where(left, zero, qa),
                               jnp.where(left, qb, zero), jnp.where(left, zero, qb)], axis=0)
        s = _dot_nt(lhs, kdup)
        s = jnp.where(mask4, s, -jnp.inf)
        sink = jnp.concatenate([jnp.full((BLK, 1), sinks_ref[kv * 4 + g], F32) for g in range(4)], axis=0)
        m = jnp.maximum(jnp.max(s, axis=-1, keepdims=True), sink)
        e = jnp.exp(s - m)
        denom = jnp.sum(e, axis=-1, keepdims=True) + jnp.exp(sink - m)
        p = (e / denom).astype(BF16)
        o = [_dot(p[g * BLK:(g + 1) * BLK], vdup) for g in range(4)]
        outs.append(jnp.concatenate([jnp.where(left, o[0], o[1]), jnp.where(left, o[2], o[3])], axis=1))
    return outs


def _split3(x):
    hi = x.astype(BF16)
    r1 = x - hi.astype(F32)
    mid = r1.astype(BF16)
    lo = (r1 - mid.astype(F32)).astype(BF16)
    return hi, mid, lo


def _ssd_block(xc, dt, a_row, dskip_ref, ht_ref, y_ref, r0):
    left = _left_half((BLK, LANES))
    row_i = lax.broadcasted_iota(jnp.int32, (BLK, BLK), 0)
    col_i = lax.broadcasted_iota(jnp.int32, (BLK, BLK), 1)
    tri = col_i <= row_i
    tri_b = tri.astype(BF16)
    da = dt * a_row
    hi, mid, lo = _split3(da)
    cs = (_dot(tri_b, hi) + _dot(tri_b, mid)) + _dot(tri_b, lo)
    cs_t = cs.T
    ecs = jnp.exp(cs)
    decay = jnp.exp(cs[BLK - 1:BLK, :] - cs)

    def colb(a, h):
        return jnp.broadcast_to(a[:, h:h + 1], (BLK, LANES))

    for g in range(SSM_GROUPS):
        bg = xc[:, D_SSM + g * SSM_STATE:D_SSM + (g + 1) * SSM_STATE]
        cg = xc[:, D_SSM + D_BC + g * SSM_STATE:D_SSM + D_BC + (g + 1) * SSM_STATE].astype(BF16)
        cb = _dot_nt(cg, bg.astype(BF16))
        gs = slice(g * 2 * LANES, (g + 1) * 2 * LANES)
        ht_g = ht_ref[:, gs]
        yoff = _dot(cg, ht_g.astype(BF16))
        xdec, cdec = [], []
        for hp in range(2):
            h0 = g * 4 + 2 * hp
            h1 = h0 + 1
            cols = slice((g * 2 + hp) * LANES, (g * 2 + hp + 1) * LANES)
            xs2 = xc[:, cols]
            dt2 = jnp.where(left, colb(dt, h0), colb(dt, h1))
            ecs2 = jnp.where(left, colb(ecs, h0), colb(ecs, h1))
            dec2 = jnp.where(left, colb(decay, h0), colb(decay, h1))
            xd2 = xs2 * dt2
            xd2b = xd2.astype(BF16)
            yd = []
            for h in (h0, h1):
                seg = colb(cs, h) - jnp.broadcast_to(cs_t[h:h + 1, :], (BLK, BLK))
                lm = jnp.exp(jnp.where(tri, seg, -jnp.inf))
                yd.append(_dot((cb * lm).astype(BF16), xd2b))
            dsk2 = jnp.where(left, dskip_ref[h0], dskip_ref[h1])
            y2 = (jnp.where(left, yd[0], yd[1]) + yoff[:, hp * LANES:(hp + 1) * LANES] * ecs2) + xs2 * dsk2
            y_ref[pl.ds(r0, BLK), cols] = y2
            xdec.append((xd2 * dec2).astype(BF16))
            cdec.append(ecs2[BLK - 1:BLK, :])
        xdec = jnp.concatenate(xdec, axis=1)
        cdec = jnp.concatenate(cdec, axis=1)
        ht_ref[:, gs] = cdec * ht_g + _dot(bg.T.astype(BF16), xdec)


def _gated_norm(y, z, w):
    g = y * _silu(z)
    parts = []
    width = D_SSM // SSM_GROUPS
    for i in range(SSM_GROUPS):
        gg = g[:, i * width:(i + 1) * width]
        parts.append(gg * lax.rsqrt(jnp.mean(gg * gg, axis=-1, keepdims=True) + EPS))
    return jnp.concatenate(parts, axis=1) * w


def _conv(xbc_ref, rows, conv_w_ref, conv_b_ref):
    acc = conv_b_ref[...]
    for i in range(CONV_WIDTH):
        off = HIST - (CONV_WIDTH - 1) + i
        acc = acc + xbc_ref[off:off + rows, :] * conv_w_ref[i:i + 1, :]
    return _silu(acc)


def _in_proj(u, w_in_ref, lo, hi):
    return _dot(u, w_in_ref[:, lo:hi])


def _mix_prompt_kernel(sinks_ref, dskip_ref, x_ref, g1_ref, w_in_ref, conv_w_ref, conv_b_ref, dtb_ref, alog_ref,
                       normw_ref, w_out_ref,
                       h1_ref, kout_ref, vout_ref, convout_ref, ssmout_ref,
                       q_s, kh_s, vh_s, z_s, xbc_s, xc_s, y_s, mix_s, ht_s, *, ts):
    j = pl.program_id(1)
    nj = pl.num_programs(1)

    @pl.when(j == 0)
    def _():
        kh_s[0:WINDOW, :] = jnp.zeros((WINDOW, D_KV), F32)
        vh_s[0:WINDOW, :] = jnp.zeros((WINDOW, D_KV), F32)
        xbc_s[0:HIST, :] = jnp.zeros((HIST, CONV_DIM), F32)
        ht_s[...] = jnp.zeros(ht_s.shape, F32)

    x = x_ref[0]
    u = _rms(x, g1_ref[...]).astype(BF16)
    q_s[...] = (_in_proj(u, w_in_ref, O_Q, O_K) * SCALE).astype(BF16)
    kh_s[WINDOW:WINDOW + ts, :] = _in_proj(u, w_in_ref, O_K, O_V)
    vh_s[WINDOW:WINDOW + ts, :] = _in_proj(u, w_in_ref, O_V, O_Z)
    z_s[...] = _in_proj(u, w_in_ref, O_Z, O_X)
    xbc_s[HIST:HIST + ts, :] = _in_proj(u, w_in_ref, O_X, O_DT)
    dt = _softplus(_in_proj(u, w_in_ref, O_DT, D_IN_PAD) + dtb_ref[...])
    a_row = -jnp.exp(alog_ref[...])

    xc_s[...] = _conv(xbc_s, ts, conv_w_ref, conv_b_ref)

    row = lax.broadcasted_iota(jnp.int32, (BLK, 2 * BLK), 0)
    key = lax.broadcasted_iota(jnp.int32, (BLK, 2 * BLK), 1)
    qc = row // CHUNK
    band = (key >= qc * CHUNK) & (key < (qc + 3) * CHUNK)
    for p in range(ts // BLK):
        r0 = p * BLK
        first = jnp.where(j == 0, WINDOW, 0) if p == 0 else 0
        mask = band & (key >= first)
        outs = _attn_block(q_s[r0:r0 + BLK, :], kh_s[r0:r0 + 2 * BLK, :], vh_s[r0:r0 + 2 * BLK, :], mask, sinks_ref)
        for kv in range(N_KV_HEADS):
            mix_s[r0:r0 + BLK, kv * 256:(kv + 1) * 256] = outs[kv].astype(BF16)
        _ssd_block(xc_s[r0:r0 + BLK, :], dt[r0:r0 + BLK, :], a_row, dskip_ref, ht_s, y_s, r0)

    mix_s[:, D_ATTN:] = _gated_norm(y_s[...], z_s[...], normw_ref[...]).astype(BF16)
    h1_ref[0] = x + _dot(mix_s[...], w_out_ref[...])

    @pl.when(j == nj - 1)
    def _():
        kout_ref[0] = kh_s[ts:ts + WINDOW, :]
        vout_ref[0] = vh_s[ts:ts + WINDOW, :]
        convout_ref[0] = xbc_s[HIST + ts - (CONV_WIDTH - 1):HIST + ts, :]
        ssmout_ref[0] = ht_s[...].T

    @pl.when(j < nj - 1)
    def _():
        kh_s[0:WINDOW, :] = kh_s[ts:ts + WINDOW, :]
        vh_s[0:WINDOW, :] = vh_s[ts:ts + WINDOW, :]
        xbc_s[0:HIST, :] = xbc_s[ts:ts + HIST, :]


def _mix_prompt(x, p, ts):
    b, s, _ = x.shape
    nj = s // ts
    const2 = lambda i, j: (0, 0)
    smem = pl.BlockSpec(memory_space=pltpu.SMEM)
    in_specs = [
        smem, smem,
        pl.BlockSpec((1, ts, D_MODEL), lambda i, j: (i, j, 0)),
        pl.BlockSpec((1, D_MODEL), const2),
        pl.BlockSpec((D_MODEL, D_IN_PAD), const2),
        pl.BlockSpec((CONV_WIDTH, CONV_DIM), const2),
        pl.BlockSpec((1, CONV_DIM), const2),
        pl.BlockSpec((1, DT_PAD), const2),
        pl.BlockSpec((1, DT_PAD), const2),
        pl.BlockSpec((1, D_SSM), const2),
        pl.BlockSpec((D_MIX, D_MODEL), const2),
    ]
    out_shape = (
        jax.ShapeDtypeStruct((b, s, D_MODEL), F32),
        jax.ShapeDtypeStruct((b, WINDOW, D_KV), F32),
        jax.ShapeDtypeStruct((b, WINDOW, D_KV), F32),
        jax.ShapeDtypeStruct((b, CONV_WIDTH - 1, CONV_DIM), F32),
        jax.ShapeDtypeStruct((b, D_SSM, SSM_STATE), F32),
    )
    out_specs = (
        pl.BlockSpec((1, ts, D_MODEL), lambda i, j: (i, j, 0)),
        pl.BlockSpec((1, WINDOW, D_KV), lambda i, j: (i, 0, 0)),
        pl.BlockSpec((1, WINDOW, D_KV), lambda i, j: (i, 0, 0)),
        pl.BlockSpec((1, CONV_WIDTH - 1, CONV_DIM), lambda i, j: (i, 0, 0)),
        pl.BlockSpec((1, D_SSM, SSM_STATE), lambda i, j: (i, 0, 0)),
    )
    scratch = [
        pltpu.VMEM((ts, D_ATTN), BF16),
        pltpu.VMEM((WINDOW + ts, D_KV), F32),
        pltpu.VMEM((WINDOW + ts, D_KV), F32),
        pltpu.VMEM((ts, D_SSM), F32),
        pltpu.VMEM((HIST + ts, CONV_DIM), F32),
        pltpu.VMEM((ts, CONV_DIM), F32),
        pltpu.VMEM((ts, D_SSM), F32),
        pltpu.VMEM((ts, D_MIX), BF16),
        pltpu.VMEM((SSM_STATE, D_SSM), F32),
    ]
    return pl.pallas_call(
        functools.partial(_mix_prompt_kernel, ts=ts),
        grid=(b, nj),
        in_specs=in_specs,
        out_specs=out_specs,
        out_shape=out_shape,
        scratch_shapes=scratch,
        compiler_params=pltpu.CompilerParams(dimension_semantics=("arbitrary", "arbitrary"),
                                             vmem_limit_bytes=VMEM_LIMIT),
        name="mix_prompt",
    )(p["sinks"], p["d_skip"], x, p["ln1_g"], p["w_in"], p["conv_w"], p["conv_b"], p["dt_bias"], p["a_log"],
      p["ssm_norm_w"], p["w_out"])


def _mix_sample_kernel(sinks_ref, dskip_ref, x_ref, ck_ref, cv_ref, sconv_ref, sssm_ref, g1_ref, w_in_ref,
                       conv_w_ref, conv_b_ref, dtb_ref, alog_ref, normw_ref, w_out_ref,
                       h1_ref, kout_ref, vout_ref, convout_ref, ssmout_ref,
                       q_s, z_s, xbcall_s, dt_s, kwin_s, vwin_s, xbc_s, y_s, mix_s, ht_s, *, nb, t):
    i = pl.program_id(0)

    @pl.when(i == 0)
    def _():
        u = _rms(x_ref[...], g1_ref[...]).astype(BF16)
        q_s[...] = (_in_proj(u, w_in_ref, O_Q, O_K) * SCALE).astype(BF16)
        kout_ref[...] = _in_proj(u, w_in_ref, O_K, O_V)
        vout_ref[...] = _in_proj(u, w_in_ref, O_V, O_Z)
        z_s[...] = _in_proj(u, w_in_ref, O_Z, O_X)
        xbcall_s[...] = _in_proj(u, w_in_ref, O_X, O_DT)
        dt_s[...] = _softplus(_in_proj(u, w_in_ref, O_DT, D_IN_PAD) + dtb_ref[...])
        kwin_s[...] = jnp.zeros(kwin_s.shape, F32)
        vwin_s[...] = jnp.zeros(vwin_s.shape, F32)
        xbc_s[...] = jnp.zeros(xbc_s.shape, F32)

    @pl.when((i >= 1) & (i <= nb))
    def _():
        r = pl.multiple_of((i - 1) * t, t)
        rows = pl.ds(r, t)
        kwin_s[0:WINDOW, :] = ck_ref[0]
        vwin_s[0:WINDOW, :] = cv_ref[0]
        kwin_s[WINDOW:WINDOW + t, :] = kout_ref[rows, :]
        vwin_s[WINDOW:WINDOW + t, :] = vout_ref[rows, :]
        q_blk = jnp.concatenate([q_s[rows, :], jnp.zeros((BLK - t, D_ATTN), BF16)], axis=0)
        key = lax.broadcasted_iota(jnp.int32, (BLK, 2 * BLK), 1)
        outs = _attn_block(q_blk, kwin_s[...], vwin_s[...], key < WINDOW + t, sinks_ref)
        for kv in range(N_KV_HEADS):
            mix_s[rows, kv * 256:(kv + 1) * 256] = outs[kv][0:t].astype(BF16)

        xbc_s[HIST - (CONV_WIDTH - 1):HIST, :] = sconv_ref[0]
        xbc_s[HIST:HIST + t, :] = xbcall_s[rows, :]
        convout_ref[0] = xbc_s[HIST + t - (CONV_WIDTH - 1):HIST + t, :]
        live = lax.broadcasted_iota(jnp.int32, (BLK, 1), 0) < t
        xc = jnp.where(live, _conv(xbc_s, BLK, conv_w_ref, conv_b_ref), 0.0)
        dt = jnp.concatenate([dt_s[rows, :], jnp.zeros((BLK - t, DT_PAD), F32)], axis=0)
        ht_s[...] = sssm_ref[0].T
        _ssd_block(xc, dt, -jnp.exp(alog_ref[...]), dskip_ref, ht_s, y_s, 0)
        ssmout_ref[0] = ht_s[...].T
        yn = _gated_norm(y_s[0:t, :], z_s[rows, :], normw_ref[...])
        mix_s[rows, D_ATTN:] = yn.astype(BF16)

    @pl.when(i == nb + 1)
    def _():
        h1_ref[...] = x_ref[...] + _dot(mix_s[...], w_out_ref[...])


def _mix_sample(x, cache_k, cache_v, state_conv, state_ssm, p):
    nb, t, _ = x.shape
    n = nb * t
    const2 = lambda i: (0, 0)
    per = lambda i: (jnp.clip(i - 1, 0, nb - 1), 0, 0)
    smem = pl.BlockSpec(memory_space=pltpu.SMEM)
    in_specs = [
        smem, smem,
        pl.BlockSpec((n, D_MODEL), const2),
        pl.BlockSpec((1, WINDOW, D_KV), per),
        pl.BlockSpec((1, WINDOW, D_KV), per),
        pl.BlockSpec((1, CONV_WIDTH - 1, CONV_DIM), per),
        pl.BlockSpec((1, D_SSM, SSM_STATE), per),
        pl.BlockSpec((1, D_MODEL), const2),
        pl.BlockSpec((D_MODEL, D_IN_PAD), const2),
        pl.BlockSpec((CONV_WIDTH, CONV_DIM), const2),
        pl.BlockSpec((1, CONV_DIM), const2),
        pl.BlockSpec((1, DT_PAD), const2),
        pl.BlockSpec((1, DT_PAD), const2),
        pl.BlockSpec((1, D_SSM), const2),
        pl.BlockSpec((D_MIX, D_MODEL), const2),
    ]
    out_shape = (
        jax.ShapeDtypeStruct((n, D_MODEL), F32),
        jax.ShapeDtypeStruct((n, D_KV), F32),
        jax.ShapeDtypeStruct((n, D_KV), F32),
        jax.ShapeDtypeStruct((nb, CONV_WIDTH - 1, CONV_DIM), F32),
        jax.ShapeDtypeStruct((nb, D_SSM, SSM_STATE), F32),
    )
    out_specs = (
        pl.BlockSpec((n, D_MODEL), const2),
        pl.BlockSpec((n, D_KV), const2),
        pl.BlockSpec((n, D_KV), const2),
        pl.BlockSpec((1, CONV_WIDTH - 1, CONV_DIM), per),
        pl.BlockSpec((1, D_SSM, SSM_STATE), per),
    )
    scratch = [
        pltpu.VMEM((n, D_ATTN), BF16),
        pltpu.VMEM((n, D_SSM), F32),
        pltpu.VMEM((n, CONV_DIM), F32),
        pltpu.VMEM((n, DT_PAD), F32),
        pltpu.VMEM((2 * BLK, D_KV), F32),
        pltpu.VMEM((2 * BLK, D_KV), F32),
        pltpu.VMEM((HIST + BLK, CONV_DIM), F32),
        pltpu.VMEM((BLK, D_SSM), F32),
        pltpu.VMEM((n, D_MIX), BF16),
        pltpu.VMEM((SSM_STATE, D_SSM), F32),
    ]
    return pl.pallas_call(
        functools.partial(_mix_sample_kernel, nb=nb, t=t),
        grid=(nb + 2,),
        in_specs=in_specs,
        out_specs=out_specs,
        out_shape=out_shape,
        scratch_shapes=scratch,
        compiler_params=pltpu.CompilerParams(dimension_semantics=("arbitrary",), vmem_limit_bytes=VMEM_LIMIT),
        name="mix_sample",
    )(p["sinks"], p["d_skip"], x.reshape(n, D_MODEL), cache_k, cache_v, state_conv, state_ssm, p["ln1_g"],
      p["w_in"], p["conv_w"], p["conv_b"], p["dt_bias"], p["a_log"], p["ssm_norm_w"], p["w_out"])


def _ffn_kernel(h_ref, g2_ref, wg_ref, wu_ref, wd_ref, gf_ref, o_ref):
    h = h_ref[...]
    u = _rms(h, g2_ref[...]).astype(BF16)
    mid = (_silu(_dot(u, wg_ref[...])) * _dot(u, wu_ref[...])).astype(BF16)
    h2 = h + _dot(mid, wd_ref[...])
    o_ref[...] = _rms(h2, gf_ref[...])


def _ffn(h, p, tm):
    n = h.shape[0]
    d_ff = p["w_gate"].shape[1]
    const2 = lambda i: (0, 0)
    return pl.pallas_call(
        _ffn_kernel,
        grid=(n // tm,),
        in_specs=[
            pl.BlockSpec((tm, D_MODEL), lambda i: (i, 0)),
            pl.BlockSpec((1, D_MODEL), const2),
            pl.BlockSpec((D_MODEL, d_ff), const2),
            pl.BlockSpec((D_MODEL, d_ff), const2),
            pl.BlockSpec((d_ff, D_MODEL), const2),
            pl.BlockSpec((1, D_MODEL), const2),
        ],
        out_specs=pl.BlockSpec((tm, D_MODEL), lambda i: (i, 0)),
        out_shape=jax.ShapeDtypeStruct((n, D_MODEL), F32),
        compiler_params=pltpu.CompilerParams(dimension_semantics=("arbitrary",), vmem_limit_bytes=VMEM_LIMIT),
        name="ffn",
    )(h, p["ln2_g"], p["w_gate"], p["w_up"], p["w_down"], p["final_g"])


def _prep_params(ln1_g, w_in, conv_w, conv_b, dt_bias, a_log, d_skip, ssm_norm_w, sinks, w_out, ln2_g, w_gate,
                 w_up, w_down, final_g):
    pad_h = lambda v: jnp.pad(v.reshape(1, SSM_HEADS), ((0, 0), (0, DT_PAD - SSM_HEADS)))
    return {
        "ln1_g": ln1_g.reshape(1, D_MODEL),
        "w_in": jnp.pad(w_in.astype(BF16), ((0, 0), (0, D_IN_PAD - w_in.shape[1]))),
        "conv_w": conv_w,
        "conv_b": conv_b.reshape(1, CONV_DIM),
        "dt_bias": pad_h(dt_bias),
        "a_log": pad_h(a_log),
        "d_skip": d_skip,
        "ssm_norm_w": ssm_norm_w.reshape(1, D_SSM),
        "sinks": sinks,
        "w_out": w_out.astype(BF16),
        "ln2_g": ln2_g.reshape(1, D_MODEL),
        "w_gate": w_gate.astype(BF16),
        "w_up": w_up.astype(BF16),
        "w_down": w_down.astype(BF16),
        "final_g": final_g.reshape(1, D_MODEL),
    }


def _layer(x_prompt, x_sample, cache_k, cache_v, state_conv, state_ssm, p, ts, tm):
    b, s, _ = x_prompt.shape
    nb, t, _ = x_sample.shape
    h1p, kp, vp, cp, sp = _mix_prompt(x_prompt, p, ts)
    h1s, kn, vn, cn, sn = _mix_sample(x_sample, cache_k.reshape(nb, WINDOW, D_KV), cache_v.reshape(nb, WINDOW, D_KV),
                                      state_conv, state_ssm.reshape(nb, D_SSM, SSM_STATE), p)
    yp = _ffn(h1p.reshape(b * s, D_MODEL), p, tm).reshape(b, s, D_MODEL)
    ys = _ffn(h1s, p, min(tm, nb * t)).reshape(nb, t, D_MODEL)
    return (yp, ys,
            kp.reshape(b, WINDOW, N_KV_HEADS, HEAD_DIM), vp.reshape(b, WINDOW, N_KV_HEADS, HEAD_DIM), cp,
            sp.reshape(b, SSM_HEADS, SSM_HEAD_DIM, SSM_STATE),
            kn.reshape(nb, t, N_KV_HEADS, HEAD_DIM), vn.reshape(nb, t, N_KV_HEADS, HEAD_DIM), cn,
            sn.reshape(nb, SSM_HEADS, SSM_HEAD_DIM, SSM_STATE))


def kernel(x_prompt, x_sample, cache_k, cache_v, state_conv, state_ssm, ln1_g, w_in, conv_w, conv_b, dt_bias, a_log,
           d_skip, ssm_norm_w, sinks, w_out, ln2_g, w_gate, w_up, w_down, final_g):
    assert w_in.shape[0] == 1, "one layer"
    p = _prep_params(ln1_g[0], w_in[0], conv_w[0], conv_b[0], dt_bias[0], a_log[0], d_skip[0], ssm_norm_w[0],
                     sinks[0], w_out[0], ln2_g[0], w_gate[0], w_up[0], w_down[0], final_g)
    outs = _layer(x_prompt, x_sample, cache_k[0], cache_v[0], state_conv[0], state_ssm[0], p, ts=256, tm=512)
    return tuple(o[None] if i >= 2 else o for i, o in enumerate(outs))
```

```python
import functools

import jax
import jax.numpy as jnp
from jax import lax
from jax.experimental import pallas as pl
from jax.experimental.pallas import tpu as pltpu

F32 = jnp.float32
BF16 = jnp.bfloat16

D_MODEL = 1024
CHUNK = 64
WINDOW = 128
HEAD_DIM = 64
N_HEADS = 16
N_KV_HEADS = 4
Q_PER_KV = N_HEADS // N_KV_HEADS
D_ATTN = N_HEADS * HEAD_DIM
D_KV = N_KV_HEADS * HEAD_DIM
SCALE = HEAD_DIM ** -0.5
SSM_HEADS = 16
SSM_HEAD_DIM = 64
SSM_GROUPS = 4
SSM_STATE = 128
D_SSM = SSM_HEADS * SSM_HEAD_DIM
D_BC = SSM_GROUPS * SSM_STATE
CONV_WIDTH = 4
CONV_DIM = D_SSM + 2 * D_BC
D_MIX = D_ATTN + D_SSM
EPS = 1e-6

LANES = 128
BLK = 2 * CHUNK
DT_PAD = LANES
N_K, N_Z, N_X, N_DT = 0, D_KV, D_KV + D_SSM, D_KV + D_SSM + CONV_DIM
D_NAT = N_DT + DT_PAD
HIST = 8
VMEM_LIMIT = 56 * 1024 * 1024


def _rms(x, g):
    ms = jnp.mean(x * x, axis=-1, keepdims=True)
    return (x * lax.rsqrt(ms + EPS)) * g


def _silu(x):
    return x / (1.0 + jnp.exp(-x))


def _softplus(x):
    return jnp.maximum(x, 0.0) + jnp.log1p(jnp.exp(-jnp.abs(x)))


def _dot(a, b):
    return jnp.dot(a, b, preferred_element_type=F32)


def _dot_nt(a, b):
    return lax.dot_general(a, b, (((1,), (1,)), ((), ())), preferred_element_type=F32)


def _dot_tn(a, b):
    return lax.dot_general(a, b, (((0,), (0,)), ((), ())), preferred_element_type=F32)


def _left_half(shape):
    return lax.broadcasted_iota(jnp.int32, shape, len(shape) - 1) < HEAD_DIM


def _attn_block(qt_blk, kwin, vtwin, mask_t, sinks_ref):
    nq = Q_PER_KV * BLK
    mask4 = jnp.concatenate([mask_t] * Q_PER_KV, axis=1)
    zeros = jnp.zeros((HEAD_DIM, nq), BF16)
    pieces = []
    for kv in range(N_KV_HEADS):
        slab, half = kv // 2, kv % 2
        rows = slice(slab * LANES, (slab + 1) * LANES)
        qk = jnp.concatenate([qt_blk[(kv * Q_PER_KV + g) * HEAD_DIM:(kv * Q_PER_KV + g + 1) * HEAD_DIM, :]
                              for g in range(Q_PER_KV)], axis=1)
        rhs = jnp.concatenate([qk, zeros] if half == 0 else [zeros, qk], axis=0)
        s = _dot(kwin[:, rows], rhs)
        s = jnp.where(mask4, s, -jnp.inf)
        sink = jnp.concatenate([jnp.full((1, BLK), sinks_ref[kv * Q_PER_KV + g], F32) for g in range(Q_PER_KV)],
                               axis=1)
        m = jnp.maximum(jnp.max(s, axis=0, keepdims=True), sink)
        e = jnp.exp(s - m)
        denom = jnp.sum(e, axis=0, keepdims=True) + jnp.exp(sink - m)
        p = (e * (1.0 / denom)).astype(BF16)
        o = _dot(vtwin[rows, :], p)[half * HEAD_DIM:(half + 1) * HEAD_DIM, :]
        pieces += [o[:, g * BLK:(g + 1) * BLK] for g in range(Q_PER_KV)]
    return jnp.concatenate(pieces, axis=0)


def _split3(x):
    hi = x.astype(BF16)
    r1 = x - hi.astype(F32)
    mid = r1.astype(BF16)
    lo = (r1 - mid.astype(F32)).astype(BF16)
    return hi, mid, lo


def _ssd_block(xc, dt, a_row, dskip_ref, ht_ref, y_ref, r0):
    left = _left_half((BLK, LANES))
    row_i = lax.broadcasted_iota(jnp.int32, (BLK, BLK), 0)
    col_i = lax.broadcasted_iota(jnp.int32, (BLK, BLK), 1)
    tri = col_i <= row_i
    tri_b = tri.astype(BF16)
    da = dt * a_row
    hi, mid, lo = _split3(da)
    cs = (_dot(tri_b, hi) + _dot(tri_b, mid)) + _dot(tri_b, lo)
    cs_t = cs.T

    def colb(a, h):
        return jnp.broadcast_to(a[:, h:h + 1], (BLK, LANES))

    for g in range(SSM_GROUPS):
        bg = xc[:, D_SSM + g * SSM_STATE:D_SSM + (g + 1) * SSM_STATE]
        cg = xc[:, D_SSM + D_BC + g * SSM_STATE:D_SSM + D_BC + (g + 1) * SSM_STATE].astype(BF16)
        cb = _dot_nt(cg, bg.astype(BF16))
        gs = slice(g * 2 * LANES, (g + 1) * 2 * LANES)
        ht_g = ht_ref[:, gs]
        yoff = _dot(cg, ht_g.astype(BF16))
        xdec, cdec = [], []
        for hp in range(2):
            h0 = g * 4 + 2 * hp
            h1 = h0 + 1
            cols = slice((g * 2 + hp) * LANES, (g * 2 + hp + 1) * LANES)
            xs2 = xc[:, cols]
            csb = [colb(cs, h0), colb(cs, h1)]
            cs2 = jnp.where(left, csb[0], csb[1])
            ecs2 = jnp.exp(cs2)
            dec2 = jnp.exp(cs2[BLK - 1:BLK, :] - cs2)
            dt2 = jnp.where(left, colb(dt, h0), colb(dt, h1))
            xd2 = xs2 * dt2
            ms = []
            for idx, h in enumerate((h0, h1)):
                seg = csb[idx] - jnp.broadcast_to(cs_t[h:h + 1, :], (BLK, BLK))
                ms.append((cb * jnp.exp(jnp.where(tri, seg, -jnp.inf))).astype(BF16))
            yd = _dot(jnp.concatenate(ms, axis=0), xd2.astype(BF16))
            dsk2 = jnp.where(left, dskip_ref[h0], dskip_ref[h1])
            y2 = (jnp.where(left, yd[0:BLK], yd[BLK:]) + yoff[:, hp * LANES:(hp + 1) * LANES] * ecs2) + xs2 * dsk2
            y_ref[pl.ds(r0, BLK), cols] = y2
            xdec.append((xd2 * dec2).astype(BF16))
            cdec.append(ecs2[BLK - 1:BLK, :])
        xdec = jnp.concatenate(xdec, axis=1)
        cdec = jnp.concatenate(cdec, axis=1)
        ht_ref[:, gs] = cdec * ht_g + _dot(bg.T.astype(BF16), xdec)


def _gated_norm(y, z, w):
    g = y * _silu(z)
    parts = []
    width = D_SSM // SSM_GROUPS
    for i in range(SSM_GROUPS):
        gg = g[:, i * width:(i + 1) * width]
        parts.append(gg * lax.rsqrt(jnp.mean(gg * gg, axis=-1, keepdims=True) + EPS))
    return jnp.concatenate(parts, axis=1) * w


def _conv(xbc_ref, rows, conv_w_ref, conv_b_ref):
    acc = conv_b_ref[...]
    for i in range(CONV_WIDTH):
        off = HIST - (CONV_WIDTH - 1) + i
        acc = acc + xbc_ref[off:off + rows, :] * conv_w_ref[i:i + 1, :]
    return _silu(acc)


def _nat_proj(u, w_nat_ref, lo, hi):
    return _dot(u, w_nat_ref[:, lo:hi])


def _out_proj(x, attn_t, yn, w_out_ref):
    return x + (_dot_tn(attn_t, w_out_ref[0:D_ATTN, :]) + _dot(yn, w_out_ref[D_ATTN:, :]))


def _mix_prompt_kernel(sinks_ref, dskip_ref, x_ref, g1_ref, w_qvt_ref, w_nat_ref, conv_w_ref, conv_b_ref, dtb_ref,
                       alog_ref, normw_ref, w_out_ref,
                       h1_ref, kout_ref, vout_ref, convout_ref, ssmout_ref,
                       qt_s, kh_s, vth_s, z_s, xbc_s, xc_s, y_s, attnt_s, ht_s, *, ts):
    j = pl.program_id(1)
    nj = pl.num_programs(1)

    @pl.when(j == 0)
    def _():
        kh_s[0:WINDOW, :] = jnp.zeros((WINDOW, D_KV), F32)
        vth_s[:, 0:WINDOW] = jnp.zeros((D_KV, WINDOW), F32)
        xbc_s[0:HIST, :] = jnp.zeros((HIST, CONV_DIM), F32)
        ht_s[...] = jnp.zeros(ht_s.shape, F32)

    x = x_ref[0]
    u = _rms(x, g1_ref[...]).astype(BF16)
    qt_s[...] = (_dot_nt(w_qvt_ref[0:D_ATTN, :], u) * SCALE).astype(BF16)
    vth_s[:, WINDOW:WINDOW + ts] = _dot_nt(w_qvt_ref[D_ATTN:, :], u)
    kh_s[WINDOW:WINDOW + ts, :] = _nat_proj(u, w_nat_ref, N_K, N_Z)
    z_s[...] = _nat_proj(u, w_nat_ref, N_Z, N_X)
    xbc_s[HIST:HIST + ts, :] = _nat_proj(u, w_nat_ref, N_X, N_DT)
    dt = _softplus(_nat_proj(u, w_nat_ref, N_DT, D_NAT) + dtb_ref[...])
    a_row = -jnp.exp(alog_ref[...])

    xc_s[...] = _conv(xbc_s, ts, conv_w_ref, conv_b_ref)

    key = lax.broadcasted_iota(jnp.int32, (2 * BLK, BLK), 0)
    qc = lax.broadcasted_iota(jnp.int32, (2 * BLK, BLK), 1) // CHUNK
    band = (key >= qc * CHUNK) & (key < (qc + 3) * CHUNK)
    for p in range(ts // BLK):
        r0 = p * BLK
        first = jnp.where(j == 0, WINDOW, 0) if p == 0 else 0
        mask_t = band & (key >= first)
        o_t = _attn_block(qt_s[:, r0:r0 + BLK], kh_s[r0:r0 + 2 * BLK, :].astype(BF16),
                          vth_s[:, r0:r0 + 2 * BLK].astype(BF16), mask_t, sinks_ref)
        attnt_s[:, r0:r0 + BLK] = o_t.astype(BF16)
        _ssd_block(xc_s[r0:r0 + BLK, :], dt[r0:r0 + BLK, :], a_row, dskip_ref, ht_s, y_s, r0)

    yn = _gated_norm(y_s[...], z_s[...], normw_ref[...]).astype(BF16)
    h1_ref[0] = _out_proj(x, attnt_s[...], yn, w_out_ref)

    @pl.when(j == nj - 1)
    def _():
        kout_ref[0] = kh_s[ts:ts + WINDOW, :]
        vout_ref[0] = vth_s[:, ts:ts + WINDOW].T
        convout_ref[0] = xbc_s[HIST + ts - (CONV_WIDTH - 1):HIST + ts, :]
        ssmout_ref[0] = ht_s[...].T

    @pl.when(j < nj - 1)
    def _():
        kh_s[0:WINDOW, :] = kh_s[ts:ts + WINDOW, :]
        vth_s[:, 0:WINDOW] = vth_s[:, ts:ts + WINDOW]
        xbc_s[0:HIST, :] = xbc_s[ts:ts + HIST, :]


def _mix_prompt(x, p, ts):
    b, s, _ = x.shape
    nj = s // ts
    const2 = lambda i, j: (0, 0)
    smem = pl.BlockSpec(memory_space=pltpu.SMEM)
    in_specs = [
        smem, smem,
        pl.BlockSpec((1, ts, D_MODEL), lambda i, j: (i, j, 0)),
        pl.BlockSpec((1, D_MODEL), const2),
        pl.BlockSpec((D_ATTN + D_KV, D_MODEL), const2),
        pl.BlockSpec((D_MODEL, D_NAT), const2),
        pl.BlockSpec((CONV_WIDTH, CONV_DIM), const2),
        pl.BlockSpec((1, CONV_DIM), const2),
        pl.BlockSpec((1, DT_PAD), const2),
        pl.BlockSpec((1, DT_PAD), const2),
        pl.BlockSpec((1, D_SSM), const2),
        pl.BlockSpec((D_MIX, D_MODEL), const2),
    ]
    out_shape = (
        jax.ShapeDtypeStruct((b, s, D_MODEL), F32),
        jax.ShapeDtypeStruct((b, WINDOW, D_KV), F32),
        jax.ShapeDtypeStruct((b, WINDOW, D_KV), F32),
        jax.ShapeDtypeStruct((b, CONV_WIDTH - 1, CONV_DIM), F32),
        jax.ShapeDtypeStruct((b, D_SSM, SSM_STATE), F32),
    )
    out_specs = (
        pl.BlockSpec((1, ts, D_MODEL), lambda i, j: (i, j, 0)),
        pl.BlockSpec((1, WINDOW, D_KV), lambda i, j: (i, 0, 0)),
        pl.BlockSpec((1, WINDOW, D_KV), lambda i, j: (i, 0, 0)),
        pl.BlockSpec((1, CONV_WIDTH - 1, CONV_DIM), lambda i, j: (i, 0, 0)),
        pl.BlockSpec((1, D_SSM, SSM_STATE), lambda i, j: (i, 0, 0)),
    )
    scratch = [
        pltpu.VMEM((D_ATTN, ts), BF16),
        pltpu.VMEM((WINDOW + ts, D_KV), F32),
        pltpu.VMEM((D_KV, WINDOW + ts), F32),
        pltpu.VMEM((ts, D_SSM), F32),
        pltpu.VMEM((HIST + ts, CONV_DIM), F32),
        pltpu.VMEM((ts, CONV_DIM), F32),
        pltpu.VMEM((ts, D_SSM), F32),
        pltpu.VMEM((D_ATTN, ts), BF16),
        pltpu.VMEM((SSM_STATE, D_SSM), F32),
    ]
    return pl.pallas_call(
        functools.partial(_mix_prompt_kernel, ts=ts),
        grid=(b, nj),
        in_specs=in_specs,
        out_specs=out_specs,
        out_shape=out_shape,
        scratch_shapes=scratch,
        compiler_params=pltpu.CompilerParams(dimension_semantics=("arbitrary", "arbitrary"),
                                             vmem_limit_bytes=VMEM_LIMIT),
        name="mix_prompt",
    )(p["sinks"], p["d_skip"], x, p["ln1_g"], p["w_qvt"], p["w_nat"], p["conv_w"], p["conv_b"], p["dt_bias"],
      p["a_log"], p["ssm_norm_w"], p["w_out"])


def _mix_sample_kernel(sinks_ref, dskip_ref, x_ref, ck_ref, cv_ref, sconv_ref, sssm_ref, g1_ref, w_qvt_ref,
                       w_nat_ref, conv_w_ref, conv_b_ref, dtb_ref, alog_ref, normw_ref, w_out_ref,
                       h1_ref, kout_ref, vout_ref, convout_ref, ssmout_ref,
                       qt_s, vt_s, z_s, xbcall_s, dt_s, xbc_s, y_s, attnt_s, yn_s, ht_s, *, nb, t):
    i = pl.program_id(0)
    n = nb * t
    per_tile = BLK // t

    @pl.when(i == 0)
    def _():
        u = _rms(x_ref[...], g1_ref[...]).astype(BF16)
        qt = (_dot_nt(w_qvt_ref[0:D_ATTN, :], u) * SCALE).astype(BF16)
        vt = _dot_nt(w_qvt_ref[D_ATTN:, :], u)
        for c in range(n // BLK):
            qt_s[c] = qt[:, c * BLK:(c + 1) * BLK]
            vt_s[c] = vt[:, c * BLK:(c + 1) * BLK]
        vout_ref[...] = vt.T
        kout_ref[...] = _nat_proj(u, w_nat_ref, N_K, N_Z)
        z_s[...] = _nat_proj(u, w_nat_ref, N_Z, N_X)
        xbcall_s[...] = _nat_proj(u, w_nat_ref, N_X, N_DT)
        dt_s[...] = _softplus(_nat_proj(u, w_nat_ref, N_DT, D_NAT) + dtb_ref[...])
        attnt_s[...] = jnp.zeros(attnt_s.shape, BF16)
        xbc_s[...] = jnp.zeros(xbc_s.shape, F32)

    @pl.when((i >= 1) & (i <= nb))
    def _():
        b = i - 1
        rows = pl.ds(pl.multiple_of(b * t, t), t)
        c = b // per_tile
        lo = (b % per_tile) * t
        kwin = jnp.concatenate([kout_ref[pl.ds(pl.multiple_of(c * BLK, BLK), BLK), :], ck_ref[0]], axis=0)
        vtwin = jnp.concatenate([vt_s[c], cv_ref[0].T], axis=1)
        key = lax.broadcasted_iota(jnp.int32, (2 * BLK, BLK), 0)
        mask_t = ((key >= lo) & (key < lo + t)) | (key >= BLK)
        o_t = _attn_block(qt_s[c], kwin.astype(BF16), vtwin.astype(BF16), mask_t, sinks_ref)
        lane = lax.broadcasted_iota(jnp.int32, (D_ATTN, BLK), 1)
        attnt_s[c] = jnp.where((lane >= lo) & (lane < lo + t), o_t.astype(BF16), attnt_s[c])

        xbc_s[HIST - (CONV_WIDTH - 1):HIST, :] = sconv_ref[0]
        xbc_s[HIST:HIST + t, :] = xbcall_s[rows, :]
        convout_ref[0] = xbc_s[HIST + t - (CONV_WIDTH - 1):HIST + t, :]
        live = lax.broadcasted_iota(jnp.int32, (BLK, 1), 0) < t
        xc = jnp.where(live, _conv(xbc_s, BLK, conv_w_ref, conv_b_ref), 0.0)
        dt = jnp.concatenate([dt_s[rows, :], jnp.zeros((BLK - t, DT_PAD), F32)], axis=0)
        ht_s[...] = sssm_ref[0].T
        _ssd_block(xc, dt, -jnp.exp(alog_ref[...]), dskip_ref, ht_s, y_s, 0)
        ssmout_ref[0] = ht_s[...].T
        yn_s[rows, :] = _gated_norm(y_s[0:t, :], z_s[rows, :], normw_ref[...]).astype(BF16)

    @pl.when(i == nb + 1)
    def _():
        for c in range(n // BLK):
            r = slice(c * BLK, (c + 1) * BLK)
            h1_ref[r, :] = _out_proj(x_ref[r, :], attnt_s[c], yn_s[r, :], w_out_ref)


def _mix_sample(x, cache_k, cache_v, state_conv, state_ssm, p):
    nb, t, _ = x.shape
    n = nb * t
    assert BLK % t == 0 and n % BLK == 0
    const2 = lambda i: (0, 0)
    per = lambda i: (jnp.clip(i - 1, 0, nb - 1), 0, 0)
    smem = pl.BlockSpec(memory_space=pltpu.SMEM)
    in_specs = [
        smem, smem,
        pl.BlockSpec((n, D_MODEL), const2),
        pl.BlockSpec((1, WINDOW, D_KV), per),
        pl.BlockSpec((1, WINDOW, D_KV), per),
        pl.BlockSpec((1, CONV_WIDTH - 1, CONV_DIM), per),
        pl.BlockSpec((1, D_SSM, SSM_STATE), per),
        pl.BlockSpec((1, D_MODEL), const2),
        pl.BlockSpec((D_ATTN + D_KV, D_MODEL), const2),
        pl.BlockSpec((D_MODEL, D_NAT), const2),
        pl.BlockSpec((CONV_WIDTH, CONV_DIM), const2),
        pl.BlockSpec((1, CONV_DIM), const2),
        pl.BlockSpec((1, DT_PAD), const2),
        pl.BlockSpec((1, DT_PAD), const2),
        pl.BlockSpec((1, D_SSM), const2),
        pl.BlockSpec((D_MIX, D_MODEL), const2),
    ]
    out_shape = (
        jax.ShapeDtypeStruct((n, D_MODEL), F32),
        jax.ShapeDtypeStruct((n, D_KV), F32),
        jax.ShapeDtypeStruct((n, D_KV), F32),
        jax.ShapeDtypeStruct((nb, CONV_WIDTH - 1, CONV_DIM), F32),
        jax.ShapeDtypeStruct((nb, D_SSM, SSM_STATE), F32),
    )
    out_specs = (
        pl.BlockSpec((n, D_MODEL), const2),
        pl.BlockSpec((n, D_KV), const2),
        pl.BlockSpec((n, D_KV), const2),
        pl.BlockSpec((1, CONV_WIDTH - 1, CONV_DIM), per),
        pl.BlockSpec((1, D_SSM, SSM_STATE), per),
    )
    scratch = [
        pltpu.VMEM((n // BLK, D_ATTN, BLK), BF16),
        pltpu.VMEM((n // BLK, D_KV, BLK), F32),
        pltpu.VMEM((n, D_SSM), F32),
        pltpu.VMEM((n, CONV_DIM), F32),
        pltpu.VMEM((n, DT_PAD), F32),
        pltpu.VMEM((HIST + BLK, CONV_DIM), F32),
        pltpu.VMEM((BLK, D_SSM), F32),
        pltpu.VMEM((n // BLK, D_ATTN, BLK), BF16),
        pltpu.VMEM((n, D_SSM), BF16),
        pltpu.VMEM((SSM_STATE, D_SSM), F32),
    ]
    return pl.pallas_call(
        functools.partial(_mix_sample_kernel, nb=nb, t=t),
        grid=(nb + 2,),
        in_specs=in_specs,
        out_specs=out_specs,
        out_shape=out_shape,
        scratch_shapes=scratch,
        compiler_params=pltpu.CompilerParams(dimension_semantics=("arbitrary",), vmem_limit_bytes=VMEM_LIMIT),
        name="mix_sample",
    )(p["sinks"], p["d_skip"], x.reshape(n, D_MODEL), cache_k, cache_v, state_conv, state_ssm, p["ln1_g"],
      p["w_qvt"], p["w_nat"], p["conv_w"], p["conv_b"], p["dt_bias"], p["a_log"], p["ssm_norm_w"], p["w_out"])


def _ffn_kernel(h_ref, g2_ref, wg_ref, wu_ref, wd_ref, gf_ref, o_ref):
    h = h_ref[...]
    u = _rms(h, g2_ref[...]).astype(BF16)
    mid = (_silu(_dot(u, wg_ref[...])) * _dot(u, wu_ref[...])).astype(BF16)
    h2 = h + _dot(mid, wd_ref[...])
    o_ref[...] = _rms(h2, gf_ref[...])


def _ffn(h, p, tm):
    n = h.shape[0]
    d_ff = p["w_gate"].shape[1]
    const2 = lambda i: (0, 0)
    return pl.pallas_call(
        _ffn_kernel,
        grid=(n // tm,),
        in_specs=[
            pl.BlockSpec((tm, D_MODEL), lambda i: (i, 0)),
            pl.BlockSpec((1, D_MODEL), const2),
            pl.BlockSpec((D_MODEL, d_ff), const2),
            pl.BlockSpec((D_MODEL, d_ff), const2),
            pl.BlockSpec((d_ff, D_MODEL), const2),
            pl.BlockSpec((1, D_MODEL), const2),
        ],
        out_specs=pl.BlockSpec((tm, D_MODEL), lambda i: (i, 0)),
        out_shape=jax.ShapeDtypeStruct((n, D_MODEL), F32),
        compiler_params=pltpu.CompilerParams(dimension_semantics=("arbitrary",), vmem_limit_bytes=VMEM_LIMIT),
        name="ffn",
    )(h, p["ln2_g"], p["w_gate"], p["w_up"], p["w_down"], p["final_g"])


def _prep_params(ln1_g, w_in, conv_w, conv_b, dt_bias, a_log, d_skip, ssm_norm_w, sinks, w_out, ln2_g, w_gate,
                 w_up, w_down, final_g):
    pad_h = lambda v: jnp.pad(v.reshape(1, SSM_HEADS), ((0, 0), (0, DT_PAD - SSM_HEADS)))
    o_k, o_v, o_z = D_ATTN, D_ATTN + D_KV, D_ATTN + 2 * D_KV
    w_q, w_k, w_v, w_rest = w_in[:, :o_k], w_in[:, o_k:o_v], w_in[:, o_v:o_z], w_in[:, o_z:]
    w_nat = jnp.concatenate([w_k, w_rest], axis=1).astype(BF16)
    return {
        "ln1_g": ln1_g.reshape(1, D_MODEL),
        "w_qvt": jnp.concatenate([w_q, w_v], axis=1).T.astype(BF16),
        "w_nat": jnp.pad(w_nat, ((0, 0), (0, D_NAT - w_nat.shape[1]))),
        "conv_w": conv_w,
        "conv_b": conv_b.reshape(1, CONV_DIM),
        "dt_bias": pad_h(dt_bias),
        "a_log": pad_h(a_log),
        "d_skip": d_skip,
        "ssm_norm_w": ssm_norm_w.reshape(1, D_SSM),
        "sinks": sinks,
        "w_out": w_out.astype(BF16),
        "ln2_g": ln2_g.reshape(1, D_MODEL),
        "w_gate": w_gate.astype(BF16),
        "w_up": w_up.astype(BF16),
        "w_down": w_down.astype(BF16),
        "final_g": final_g.reshape(1, D_MODEL),
    }


def _layer(x_prompt, x_sample, cache_k, cache_v, state_conv, state_ssm, p, ts, tm):
    b, s, _ = x_prompt.shape
    nb, t, _ = x_sample.shape
    h1p, kp, vp, cp, sp = _mix_prompt(x_prompt, p, ts)
    h1s, kn, vn, cn, sn = _mix_sample(x_sample, cache_k.reshape(nb, WINDOW, D_KV), cache_v.reshape(nb, WINDOW, D_KV),
                                      state_conv, state_ssm.reshape(nb, D_SSM, SSM_STATE), p)
    yp = _ffn(h1p.reshape(b * s, D_MODEL), p, tm).reshape(b, s, D_MODEL)
    ys = _ffn(h1s, p, min(tm, nb * t)).reshape(nb, t, D_MODEL)
    return (yp, ys,
            kp.reshape(b, WINDOW, N_KV_HEADS, HEAD_DIM), vp.reshape(b, WINDOW, N_KV_HEADS, HEAD_DIM), cp,
            sp.reshape(b, SSM_HEADS, SSM_HEAD_DIM, SSM_STATE),
            kn.reshape(nb, t, N_KV_HEADS, HEAD_DIM), vn.reshape(nb, t, N_KV_HEADS, HEAD_DIM), cn,
            sn.reshape(nb, SSM_HEADS, SSM_HEAD_DIM, SSM_STATE))


def kernel(x_prompt, x_sample, cache_k, cache_v, state_conv, state_ssm, ln1_g, w_in, conv_w, conv_b, dt_bias, a_log,
           d_skip, ssm_norm_w, sinks, w_out, ln2_g, w_gate, w_up, w_down, final_g):
    assert w_in.shape[0] == 1, "one layer"
    p = _prep_params(ln1_g[0], w_in[0], conv_w[0], conv_b[0], dt_bias[0], a_log[0], d_skip[0], ssm_norm_w[0],
                     sinks[0], w_out[0], ln2_g[0], w_gate[0], w_up[0], w_down[0], final_g)
    outs = _layer(x_prompt, x_sample, cache_k[0], cache_v[0], state_conv[0], state_ssm[0], p, ts=256, tm=512)
    return tuple(o[None] if i >= 2 else o for i, o in enumerate(outs))
```

```python
import functools

import jax
import jax.numpy as jnp
from jax import lax
from jax.experimental import pallas as pl
from jax.experimental.pallas import tpu as pltpu

F32 = jnp.float32
BF16 = jnp.bfloat16

D_MODEL = 1024
CHUNK = 64
WINDOW = 128
HEAD_DIM = 64
N_HEADS = 16
N_KV_HEADS = 4
Q_PER_KV = N_HEADS // N_KV_HEADS
D_ATTN = N_HEADS * HEAD_DIM
D_KV = N_KV_HEADS * HEAD_DIM
SCALE = HEAD_DIM ** -0.5
SSM_HEADS = 16
SSM_HEAD_DIM = 64
SSM_GROUPS = 4
SSM_STATE = 128
D_SSM = SSM_HEADS * SSM_HEAD_DIM
D_BC = SSM_GROUPS * SSM_STATE
CONV_WIDTH = 4
CONV_DIM = D_SSM + 2 * D_BC
D_MIX = D_ATTN + D_SSM
EPS = 1e-6

LANES = 128
BLK = 2 * CHUNK
DT_PAD = LANES
N_K, N_Z, N_X, N_DT = 0, D_KV, D_KV + D_SSM, D_KV + D_SSM + CONV_DIM
D_NAT = N_DT + DT_PAD
HIST = 8
VMEM_LIMIT = 56 * 1024 * 1024


def _rms(x, g):
    ms = jnp.mean(x * x, axis=-1, keepdims=True)
    return (x * lax.rsqrt(ms + EPS)) * g


def _silu(x):
    return x * (0.5 * jnp.tanh(0.5 * x) + 0.5)


def _softplus(x):
    return jnp.maximum(x, 0.0) + jnp.log1p(jnp.exp(-jnp.abs(x)))


def _dot(a, b):
    return jnp.dot(a, b, preferred_element_type=F32)


def _dot_nt(a, b):
    return lax.dot_general(a, b, (((1,), (1,)), ((), ())), preferred_element_type=F32)


def _dot_tn(a, b):
    return lax.dot_general(a, b, (((0,), (0,)), ((), ())), preferred_element_type=F32)


def _left_half(shape):
    return lax.broadcasted_iota(jnp.int32, shape, len(shape) - 1) < HEAD_DIM


def _attn_block(qt_blk, kwin, vtwin, mask_t, sinks_ref, between=None):
    nq = Q_PER_KV * BLK
    mask4 = jnp.concatenate([mask_t] * Q_PER_KV, axis=1)
    zeros = jnp.zeros((HEAD_DIM, nq), BF16)
    pieces = []
    for kv in range(N_KV_HEADS):
        slab, half = kv // 2, kv % 2
        rows = slice(slab * LANES, (slab + 1) * LANES)
        qk = jnp.concatenate([qt_blk[(kv * Q_PER_KV + g) * HEAD_DIM:(kv * Q_PER_KV + g + 1) * HEAD_DIM, :]
                              for g in range(Q_PER_KV)], axis=1)
        rhs = jnp.concatenate([qk, zeros] if half == 0 else [zeros, qk], axis=0)
        s = _dot(kwin[:, rows], rhs)
        if between is not None:
            between()
        s = jnp.where(mask4, s, -jnp.inf)
        sink = jnp.concatenate([jnp.full((1, BLK), sinks_ref[kv * Q_PER_KV + g], F32) for g in range(Q_PER_KV)],
                               axis=1)
        m = jnp.maximum(jnp.max(s, axis=0, keepdims=True), sink)
        e = jnp.exp(s - m)
        denom = jnp.sum(e, axis=0, keepdims=True) + jnp.exp(sink - m)
        p = (e * (1.0 / denom)).astype(BF16)
        o = _dot(vtwin[rows, :], p)[half * HEAD_DIM:(half + 1) * HEAD_DIM, :]
        pieces += [o[:, g * BLK:(g + 1) * BLK] for g in range(Q_PER_KV)]
    return jnp.concatenate(pieces, axis=0)


def _split3(x):
    hi = x.astype(BF16)
    r1 = x - hi.astype(F32)
    mid = r1.astype(BF16)
    lo = (r1 - mid.astype(F32)).astype(BF16)
    return hi, mid, lo


def _ssd_block(xc, dt, a_row, dskip_ref, ht_ref, y_ref, r0, between=None):
    left = _left_half((BLK, LANES))
    row_i = lax.broadcasted_iota(jnp.int32, (BLK, BLK), 0)
    col_i = lax.broadcasted_iota(jnp.int32, (BLK, BLK), 1)
    tri = col_i <= row_i
    tri_b = tri.astype(BF16)
    da = dt * a_row
    hi, mid, lo = _split3(da)
    cs = (_dot(tri_b, hi) + _dot(tri_b, mid)) + _dot(tri_b, lo)
    cs_t = cs.T

    def colb(a, h):
        return jnp.broadcast_to(a[:, h:h + 1], (BLK, LANES))

    for g in range(SSM_GROUPS):
        bg = xc[:, D_SSM + g * SSM_STATE:D_SSM + (g + 1) * SSM_STATE]
        cg = xc[:, D_SSM + D_BC + g * SSM_STATE:D_SSM + D_BC + (g + 1) * SSM_STATE].astype(BF16)
        cb = _dot_nt(cg, bg.astype(BF16))
        gs = slice(g * 2 * LANES, (g + 1) * 2 * LANES)
        ht_g = ht_ref[:, gs]
        yoff = _dot(cg, ht_g.astype(BF16))
        if between is not None:
            between()
        xdec, cdec = [], []
        for hp in range(2):
            h0 = g * 4 + 2 * hp
            h1 = h0 + 1
            cols = slice((g * 2 + hp) * LANES, (g * 2 + hp + 1) * LANES)
            xs2 = xc[:, cols]
            csb = [colb(cs, h0), colb(cs, h1)]
            cs2 = jnp.where(left, csb[0], csb[1])
            ecs2 = jnp.exp(cs2)
            dec2 = jnp.exp(cs2[BLK - 1:BLK, :] - cs2)
            dt2 = jnp.where(left, colb(dt, h0), colb(dt, h1))
            xd2 = xs2 * dt2
            ms = []
            for idx, h in enumerate((h0, h1)):
                seg = csb[idx] - jnp.broadcast_to(cs_t[h:h + 1, :], (BLK, BLK))
                ms.append((cb * jnp.exp(jnp.where(tri, seg, -jnp.inf))).astype(BF16))
            yd = _dot(jnp.concatenate(ms, axis=0), xd2.astype(BF16))
            dsk2 = jnp.where(left, dskip_ref[h0], dskip_ref[h1])
            y2 = (jnp.where(left, yd[0:BLK], yd[BLK:]) + yoff[:, hp * LANES:(hp + 1) * LANES] * ecs2) + xs2 * dsk2
            y_ref[pl.ds(r0, BLK), cols] = y2
            xdec.append((xd2 * dec2).astype(BF16))
            cdec.append(ecs2[BLK - 1:BLK, :])
        xdec = jnp.concatenate(xdec, axis=1)
        cdec = jnp.concatenate(cdec, axis=1)
        ht_ref[:, gs] = cdec * ht_g + _dot(bg.T.astype(BF16), xdec)


def _gated_norm(y, z, w):
    g = y * _silu(z)
    parts = []
    width = D_SSM // SSM_GROUPS
    for i in range(SSM_GROUPS):
        gg = g[:, i * width:(i + 1) * width]
        parts.append(gg * lax.rsqrt(jnp.mean(gg * gg, axis=-1, keepdims=True) + EPS))
    return jnp.concatenate(parts, axis=1) * w


def _conv(xbc_ref, rows, conv_w_ref, conv_b_ref, cols=slice(None)):
    acc = conv_b_ref[:, cols]
    for i in range(CONV_WIDTH):
        off = HIST - (CONV_WIDTH - 1) + i
        acc = acc + xbc_ref[off:off + rows, cols] * conv_w_ref[i:i + 1, cols]
    return _silu(acc)


def _nat_proj(u, w_nat_ref, lo, hi):
    return _dot(u, w_nat_ref[:, lo:hi])


def _out_proj(x, attn_t, yn, w_out_ref):
    return x + (_dot_tn(attn_t, w_out_ref[0:D_ATTN, :]) + _dot(yn, w_out_ref[D_ATTN:, :]))


def _mix_prompt_kernel(sinks_ref, dskip_ref, xp_ref, xc_ref, g1_ref, w_qvt_ref, w_nat_ref, conv_w_ref, conv_b_ref,
                       dtb_ref, alog_ref, normw_ref, w_out_ref,
                       h1_ref, kout_ref, vout_ref, convout_ref, ssmout_ref,
                       qt_s, kh_s, vth_s, z_s, xbc_s, xc_s, dt_s, y_s, attnt_s, ht_s, *, ts, nj):
    t = pl.program_id(0)
    sp = t % 2
    sc = 1 - sp
    jp = t % nj
    jc = (t + nj - 1) % nj

    @pl.when(t == 0)
    def _():
        for ref in (qt_s, kh_s, vth_s, z_s, xbc_s, xc_s, dt_s):
            ref[...] = jnp.zeros(ref.shape, ref.dtype)

    @pl.when(jc == 0)
    def _():
        ht_s[...] = jnp.zeros(ht_s.shape, F32)

    def project():
        fresh = jp == 0
        kh_s[sp, 0:WINDOW, :] = jnp.where(fresh, 0.0, kh_s[sc, ts:ts + WINDOW, :])
        vth_s[sp, :, 0:WINDOW] = jnp.where(fresh, 0.0, vth_s[sc, :, ts:ts + WINDOW])
        xbc_s[sp, 0:HIST, :] = jnp.where(fresh, 0.0, xbc_s[sc, ts:ts + HIST, :])
        u = _rms(xp_ref[0], g1_ref[...]).astype(BF16)
        yield
        n_piece = 4
        w = CONV_DIM // n_piece
        for c in range(n_piece):
            xbc_s[sp, HIST:HIST + ts, c * w:(c + 1) * w] = _nat_proj(u, w_nat_ref, N_X + c * w, N_X + (c + 1) * w)
            yield
        w = D_ATTN // n_piece
        for c in range(n_piece):
            qt_s[sp, c * w:(c + 1) * w, :] = (_dot_nt(w_qvt_ref[c * w:(c + 1) * w, :], u) * SCALE).astype(BF16)
            yield
        w = D_SSM // 2
        for c in range(2):
            z_s[sp, :, c * w:(c + 1) * w] = _nat_proj(u, w_nat_ref, N_Z + c * w, N_Z + (c + 1) * w)
            yield
        vth_s[sp, :, WINDOW:WINDOW + ts] = _dot_nt(w_qvt_ref[D_ATTN:, :], u)
        yield
        kh_s[sp, WINDOW:WINDOW + ts, :] = _nat_proj(u, w_nat_ref, N_K, N_Z)
        yield
        dt_s[sp] = _softplus(_nat_proj(u, w_nat_ref, N_DT, D_NAT) + dtb_ref[...])
        yield
        w = CONV_DIM // n_piece
        for c in range(n_piece):
            cols = slice(c * w, (c + 1) * w)
            xc_s[sp, :, cols] = _conv(xbc_s.at[sp], ts, conv_w_ref, conv_b_ref, cols)
            yield

    pieces = project()
    advance = lambda: next(pieces, None)
    advance()

    a_row = -jnp.exp(alog_ref[...])
    key = lax.broadcasted_iota(jnp.int32, (2 * BLK, BLK), 0)
    qc = lax.broadcasted_iota(jnp.int32, (2 * BLK, BLK), 1) // CHUNK
    band = (key >= qc * CHUNK) & (key < (qc + 3) * CHUNK)
    for p in range(ts // BLK):
        r0 = p * BLK
        first = jnp.where(jc == 0, WINDOW, 0) if p == 0 else 0
        mask_t = band & (key >= first)
        o_t = _attn_block(qt_s[sc, :, r0:r0 + BLK], kh_s[sc, r0:r0 + 2 * BLK, :].astype(BF16),
                          vth_s[sc, :, r0:r0 + 2 * BLK].astype(BF16), mask_t, sinks_ref, advance)
        attnt_s[:, r0:r0 + BLK] = o_t.astype(BF16)
        _ssd_block(xc_s[sc, r0:r0 + BLK, :], dt_s[sc, r0:r0 + BLK, :], a_row, dskip_ref, ht_s, y_s, r0, advance)
    yn = _gated_norm(y_s[...], z_s[sc], normw_ref[...]).astype(BF16)
    n_piece = 4
    w = D_MODEL // n_piece
    for c in range(n_piece):
        cols = slice(c * w, (c + 1) * w)
        h1_ref[0, :, cols] = xc_ref[0, :, cols] + (_dot_tn(attnt_s[...], w_out_ref[0:D_ATTN, cols])
                                                   + _dot(yn, w_out_ref[D_ATTN:, cols]))
        advance()
    for _ in pieces:
        pass

    @pl.when(jc == nj - 1)
    def _():
        kout_ref[0] = kh_s[sc, ts:ts + WINDOW, :]
        vout_ref[0] = vth_s[sc, :, ts:ts + WINDOW].T
        convout_ref[0] = xbc_s[sc, HIST + ts - (CONV_WIDTH - 1):HIST + ts, :]
        ssmout_ref[0] = ht_s[...].T


def _mix_prompt(x, p, ts):
    b, s, _ = x.shape
    nj = s // ts
    nt = b * nj
    const2 = lambda t: (0, 0)
    prev = lambda t: (jnp.maximum(t - 1, 0), 0, 0)
    per_stream = lambda t: (jnp.maximum(t - 1, 0) // nj, 0, 0)
    smem = pl.BlockSpec(memory_space=pltpu.SMEM)
    in_specs = [
        smem, smem,
        pl.BlockSpec((1, ts, D_MODEL), lambda t: (jnp.minimum(t, nt - 1), 0, 0)),
        pl.BlockSpec((1, ts, D_MODEL), prev),
        pl.BlockSpec((1, D_MODEL), const2),
        pl.BlockSpec((D_ATTN + D_KV, D_MODEL), const2),
        pl.BlockSpec((D_MODEL, D_NAT), const2),
        pl.BlockSpec((CONV_WIDTH, CONV_DIM), const2),
        pl.BlockSpec((1, CONV_DIM), const2),
        pl.BlockSpec((1, DT_PAD), const2),
        pl.BlockSpec((1, DT_PAD), const2),
        pl.BlockSpec((1, D_SSM), const2),
        pl.BlockSpec((D_MIX, D_MODEL), const2),
    ]
    out_shape = (
        jax.ShapeDtypeStruct((nt, ts, D_MODEL), F32),
        jax.ShapeDtypeStruct((b, WINDOW, D_KV), F32),
        jax.ShapeDtypeStruct((b, WINDOW, D_KV), F32),
        jax.ShapeDtypeStruct((b, CONV_WIDTH - 1, CONV_DIM), F32),
        jax.ShapeDtypeStruct((b, D_SSM, SSM_STATE), F32),
    )
    out_specs = (
        pl.BlockSpec((1, ts, D_MODEL), prev),
        pl.BlockSpec((1, WINDOW, D_KV), per_stream),
        pl.BlockSpec((1, WINDOW, D_KV), per_stream),
        pl.BlockSpec((1, CONV_WIDTH - 1, CONV_DIM), per_stream),
        pl.BlockSpec((1, D_SSM, SSM_STATE), per_stream),
    )
    scratch = [
        pltpu.VMEM((2, D_ATTN, ts), BF16),
        pltpu.VMEM((2, WINDOW + ts, D_KV), F32),
        pltpu.VMEM((2, D_KV, WINDOW + ts), F32),
        pltpu.VMEM((2, ts, D_SSM), F32),
        pltpu.VMEM((2, HIST + ts, CONV_DIM), F32),
        pltpu.VMEM((2, ts, CONV_DIM), F32),
        pltpu.VMEM((2, ts, DT_PAD), F32),
        pltpu.VMEM((ts, D_SSM), F32),
        pltpu.VMEM((D_ATTN, ts), BF16),
        pltpu.VMEM((SSM_STATE, D_SSM), F32),
    ]
    x3 = x.reshape(nt, ts, D_MODEL)
    outs = pl.pallas_call(
        functools.partial(_mix_prompt_kernel, ts=ts, nj=nj),
        grid=(nt + 1,),
        in_specs=in_specs,
        out_specs=out_specs,
        out_shape=out_shape,
        scratch_shapes=scratch,
        compiler_params=pltpu.CompilerParams(dimension_semantics=("arbitrary",), vmem_limit_bytes=VMEM_LIMIT),
        name="mix_prompt",
    )(p["sinks"], p["d_skip"], x3, x3, p["ln1_g"], p["w_qvt"], p["w_nat"], p["conv_w"], p["conv_b"], p["dt_bias"],
      p["a_log"], p["ssm_norm_w"], p["w_out"])
    return (outs[0].reshape(b, s, D_MODEL),) + tuple(outs[1:])


def _mix_sample_kernel(sinks_ref, dskip_ref, x_ref, ck_ref, cv_ref, sconv_ref, sssm_ref, g1_ref, w_qvt_ref,
                       w_nat_ref, conv_w_ref, conv_b_ref, dtb_ref, alog_ref, normw_ref, w_out_ref,
                       h1_ref, kout_ref, vout_ref, convout_ref, ssmout_ref,
                       qt_s, vt_s, z_s, xbcall_s, dt_s, xbc_s, y_s, attnt_s, yn_s, ht_s, *, nb, t):
    i = pl.program_id(0)
    n = nb * t
    per_tile = BLK // t

    @pl.when(i == 0)
    def _():
        u = _rms(x_ref[...], g1_ref[...]).astype(BF16)
        qt = (_dot_nt(w_qvt_ref[0:D_ATTN, :], u) * SCALE).astype(BF16)
        vt = _dot_nt(w_qvt_ref[D_ATTN:, :], u)
        for c in range(n // BLK):
            qt_s[c] = qt[:, c * BLK:(c + 1) * BLK]
            vt_s[c] = vt[:, c * BLK:(c + 1) * BLK]
        vout_ref[...] = vt.T
        kout_ref[...] = _nat_proj(u, w_nat_ref, N_K, N_Z)
        z_s[...] = _nat_proj(u, w_nat_ref, N_Z, N_X)
        xbcall_s[...] = _nat_proj(u, w_nat_ref, N_X, N_DT)
        dt_s[...] = _softplus(_nat_proj(u, w_nat_ref, N_DT, D_NAT) + dtb_ref[...])
        attnt_s[...] = jnp.zeros(attnt_s.shape, BF16)
        xbc_s[...] = jnp.zeros(xbc_s.shape, F32)

    @pl.when((i >= 1) & (i <= nb))
    def _():
        b = i - 1
        rows = pl.ds(pl.multiple_of(b * t, t), t)
        c = b // per_tile
        lo = (b % per_tile) * t
        kwin = jnp.concatenate([kout_ref[pl.ds(pl.multiple_of(c * BLK, BLK), BLK), :], ck_ref[0]], axis=0)
        vtwin = jnp.concatenate([vt_s[c], cv_ref[0].T], axis=1)
        key = lax.broadcasted_iota(jnp.int32, (2 * BLK, BLK), 0)
        mask_t = ((key >= lo) & (key < lo + t)) | (key >= BLK)
        o_t = _attn_block(qt_s[c], kwin.astype(BF16), vtwin.astype(BF16), mask_t, sinks_ref)
        lane = lax.broadcasted_iota(jnp.int32, (D_ATTN, BLK), 1)
        attnt_s[c] = jnp.where((lane >= lo) & (lane < lo + t), o_t.astype(BF16), attnt_s[c])

        xbc_s[HIST - (CONV_WIDTH - 1):HIST, :] = sconv_ref[0]
        xbc_s[HIST:HIST + t, :] = xbcall_s[rows, :]
        convout_ref[0] = xbc_s[HIST + t - (CONV_WIDTH - 1):HIST + t, :]
        live = lax.broadcasted_iota(jnp.int32, (BLK, 1), 0) < t
        xc = jnp.where(live, _conv(xbc_s, BLK, conv_w_ref, conv_b_ref), 0.0)
        dt = jnp.concatenate([dt_s[rows, :], jnp.zeros((BLK - t, DT_PAD), F32)], axis=0)
        ht_s[...] = sssm_ref[0].T
        _ssd_block(xc, dt, -jnp.exp(alog_ref[...]), dskip_ref, ht_s, y_s, 0)
        ssmout_ref[0] = ht_s[...].T
        yn_s[rows, :] = _gated_norm(y_s[0:t, :], z_s[rows, :], normw_ref[...]).astype(BF16)

    @pl.when(i == nb + 1)
    def _():
        for c in range(n // BLK):
            r = slice(c * BLK, (c + 1) * BLK)
            h1_ref[r, :] = _out_proj(x_ref[r, :], attnt_s[c], yn_s[r, :], w_out_ref)


def _mix_sample(x, cache_k, cache_v, state_conv, state_ssm, p):
    nb, t, _ = x.shape
    n = nb * t
    assert BLK % t == 0 and n % BLK == 0
    const2 = lambda i: (0, 0)
    per = lambda i: (jnp.clip(i - 1, 0, nb - 1), 0, 0)
    smem = pl.BlockSpec(memory_space=pltpu.SMEM)
    in_specs = [
        smem, smem,
        pl.BlockSpec((n, D_MODEL), const2),
        pl.BlockSpec((1, WINDOW, D_KV), per),
        pl.BlockSpec((1, WINDOW, D_KV), per),
        pl.BlockSpec((1, CONV_WIDTH - 1, CONV_DIM), per),
        pl.BlockSpec((1, D_SSM, SSM_STATE), per),
        pl.BlockSpec((1, D_MODEL), const2),
        pl.BlockSpec((D_ATTN + D_KV, D_MODEL), const2),
        pl.BlockSpec((D_MODEL, D_NAT), const2),
        pl.BlockSpec((CONV_WIDTH, CONV_DIM), const2),
        pl.BlockSpec((1, CONV_DIM), const2),
        pl.BlockSpec((1, DT_PAD), const2),
        pl.BlockSpec((1, DT_PAD), const2),
        pl.BlockSpec((1, D_SSM), const2),
        pl.BlockSpec((D_MIX, D_MODEL), const2),
    ]
    out_shape = (
        jax.ShapeDtypeStruct((n, D_MODEL), F32),
        jax.ShapeDtypeStruct((n, D_KV), F32),
        jax.ShapeDtypeStruct((n, D_KV), F32),
        jax.ShapeDtypeStruct((nb, CONV_WIDTH - 1, CONV_DIM), F32),
        jax.ShapeDtypeStruct((nb, D_SSM, SSM_STATE), F32),
    )
    out_specs = (
        pl.BlockSpec((n, D_MODEL), const2),
        pl.BlockSpec((n, D_KV), const2),
        pl.BlockSpec((n, D_KV), const2),
        pl.BlockSpec((1, CONV_WIDTH - 1, CONV_DIM), per),
        pl.BlockSpec((1, D_SSM, SSM_STATE), per),
    )
    scratch = [
        pltpu.VMEM((n // BLK, D_ATTN, BLK), BF16),
        pltpu.VMEM((n // BLK, D_KV, BLK), F32),
        pltpu.VMEM((n, D_SSM), F32),
        pltpu.VMEM((n, CONV_DIM), F32),
        pltpu.VMEM((n, DT_PAD), F32),
        pltpu.VMEM((HIST + BLK, CONV_DIM), F32),
        pltpu.VMEM((BLK, D_SSM), F32),
        pltpu.VMEM((n // BLK, D_ATTN, BLK), BF16),
        pltpu.VMEM((n, D_SSM), BF16),
        pltpu.VMEM((SSM_STATE, D_SSM), F32),
    ]
    return pl.pallas_call(
        functools.partial(_mix_sample_kernel, nb=nb, t=t),
        grid=(nb + 2,),
        in_specs=in_specs,
        out_specs=out_specs,
        out_shape=out_shape,
        scratch_shapes=scratch,
        compiler_params=pltpu.CompilerParams(dimension_semantics=("arbitrary",), vmem_limit_bytes=VMEM_LIMIT),
        name="mix_sample",
    )(p["sinks"], p["d_skip"], x.reshape(n, D_MODEL), cache_k, cache_v, state_conv, state_ssm, p["ln1_g"],
      p["w_qvt"], p["w_nat"], p["conv_w"], p["conv_b"], p["dt_bias"], p["a_log"], p["ssm_norm_w"], p["w_out"])


def _ffn_kernel(h_ref, g2_ref, wg_ref, wu_ref, wd_ref, gf_ref, o_ref):
    h = h_ref[...]
    u = _rms(h, g2_ref[...]).astype(BF16)
    mid = (_silu(_dot(u, wg_ref[...])) * _dot(u, wu_ref[...])).astype(BF16)
    h2 = h + _dot(mid, wd_ref[...])
    o_ref[...] = _rms(h2, gf_ref[...])


def _ffn(h, p, tm):
    n = h.shape[0]
    d_ff = p["w_gate"].shape[1]
    const2 = lambda i: (0, 0)
    return pl.pallas_call(
        _ffn_kernel,
        grid=(n // tm,),
        in_specs=[
            pl.BlockSpec((tm, D_MODEL), lambda i: (i, 0)),
            pl.BlockSpec((1, D_MODEL), const2),
            pl.BlockSpec((D_MODEL, d_ff), const2),
            pl.BlockSpec((D_MODEL, d_ff), const2),
            pl.BlockSpec((d_ff, D_MODEL), const2),
            pl.BlockSpec((1, D_MODEL), const2),
        ],
        out_specs=pl.BlockSpec((tm, D_MODEL), lambda i: (i, 0)),
        out_shape=jax.ShapeDtypeStruct((n, D_MODEL), F32),
        compiler_params=pltpu.CompilerParams(dimension_semantics=("arbitrary",), vmem_limit_bytes=VMEM_LIMIT),
        name="ffn",
    )(h, p["ln2_g"], p["w_gate"], p["w_up"], p["w_down"], p["final_g"])


def _prep_params(ln1_g, w_in, conv_w, conv_b, dt_bias, a_log, d_skip, ssm_norm_w, sinks, w_out, ln2_g, w_gate,
                 w_up, w_down, final_g):
    pad_h = lambda v: jnp.pad(v.reshape(1, SSM_HEADS), ((0, 0), (0, DT_PAD - SSM_HEADS)))
    o_k, o_v, o_z = D_ATTN, D_ATTN + D_KV, D_ATTN + 2 * D_KV
    w_q, w_k, w_v, w_rest = w_in[:, :o_k], w_in[:, o_k:o_v], w_in[:, o_v:o_z], w_in[:, o_z:]
    w_nat = jnp.concatenate([w_k, w_rest], axis=1).astype(BF16)
    return {
        "ln1_g": ln1_g.reshape(1, D_MODEL),
        "w_qvt": jnp.concatenate([w_q, w_v], axis=1).T.astype(BF16),
        "w_nat": jnp.pad(w_nat, ((0, 0), (0, D_NAT - w_nat.shape[1]))),
        "conv_w": conv_w,
        "conv_b": conv_b.reshape(1, CONV_DIM),
        "dt_bias": pad_h(dt_bias),
        "a_log": pad_h(a_log),
        "d_skip": d_skip,
        "ssm_norm_w": ssm_norm_w.reshape(1, D_SSM),
        "sinks": sinks,
        "w_out": w_out.astype(BF16),
        "ln2_g": ln2_g.reshape(1, D_MODEL),
        "w_gate": w_gate.astype(BF16),
        "w_up": w_up.astype(BF16),
        "w_down": w_down.astype(BF16),
        "final_g": final_g.reshape(1, D_MODEL),
    }


def _layer(x_prompt, x_sample, cache_k, cache_v, state_conv, state_ssm, p, ts, tm):
    b, s, _ = x_prompt.shape
    nb, t, _ = x_sample.shape
    h1p, kp, vp, cp, sp = _mix_prompt(x_prompt, p, ts)
    h1s, kn, vn, cn, sn = _mix_sample(x_sample, cache_k.reshape(nb, WINDOW, D_KV), cache_v.reshape(nb, WINDOW, D_KV),
                                      state_conv, state_ssm.reshape(nb, D_SSM, SSM_STATE), p)
    yp = _ffn(h1p.reshape(b * s, D_MODEL), p, tm).reshape(b, s, D_MODEL)
    ys = _ffn(h1s, p, min(tm, nb * t)).reshape(nb, t, D_MODEL)
    return (yp, ys,
            kp.reshape(b, WINDOW, N_KV_HEADS, HEAD_DIM), vp.reshape(b, WINDOW, N_KV_HEADS, HEAD_DIM), cp,
            sp.reshape(b, SSM_HEADS, SSM_HEAD_DIM, SSM_STATE),
            kn.reshape(nb, t, N_KV_HEADS, HEAD_DIM), vn.reshape(nb, t, N_KV_HEADS, HEAD_DIM), cn,
            sn.reshape(nb, SSM_HEADS, SSM_HEAD_DIM, SSM_STATE))


def kernel(x_prompt, x_sample, cache_k, cache_v, state_conv, state_ssm, ln1_g, w_in, conv_w, conv_b, dt_bias, a_log,
           d_skip, ssm_norm_w, sinks, w_out, ln2_g, w_gate, w_up, w_down, final_g):
    assert w_in.shape[0] == 1, "one layer"
    p = _prep_params(ln1_g[0], w_in[0], conv_w[0], conv_b[0], dt_bias[0], a_log[0], d_skip[0], ssm_norm_w[0],
                     sinks[0], w_out[0], ln2_g[0], w_gate[0], w_up[0], w_down[0], final_g)
    outs = _layer(x_prompt, x_sample, cache_k[0], cache_v[0], state_conv[0], state_ssm[0], p, ts=256, tm=512)
    return tuple(o[None] if i >= 2 else o for i, o in enumerate(outs))
```

```python
import functools

import jax
import jax.numpy as jnp
from jax import lax
from jax.experimental import pallas as pl
from jax.experimental.pallas import tpu as pltpu

F32 = jnp.float32
BF16 = jnp.bfloat16

D_MODEL = 1024
CHUNK = 64
WINDOW = 128
HEAD_DIM = 64
N_HEADS = 16
N_KV_HEADS = 4
Q_PER_KV = N_HEADS // N_KV_HEADS
D_ATTN = N_HEADS * HEAD_DIM
D_KV = N_KV_HEADS * HEAD_DIM
SCALE = HEAD_DIM ** -0.5
SSM_HEADS = 16
SSM_HEAD_DIM = 64
SSM_GROUPS = 4
SSM_STATE = 128
D_SSM = SSM_HEADS * SSM_HEAD_DIM
D_BC = SSM_GROUPS * SSM_STATE
CONV_WIDTH = 4
CONV_DIM = D_SSM + 2 * D_BC
D_MIX = D_ATTN + D_SSM
EPS = 1e-6

LANES = 128
BLK = 2 * CHUNK
DT_PAD = LANES
N_K, N_Z, N_X, N_DT = 0, D_KV, D_KV + D_SSM, D_KV + D_SSM + CONV_DIM
D_NAT = N_DT + DT_PAD
HIST = 8
VMEM_LIMIT = 56 * 1024 * 1024


def _rms(x, g):
    ms = jnp.mean(x * x, axis=-1, keepdims=True)
    return (x * lax.rsqrt(ms + EPS)) * g


def _silu(x):
    h = 0.5 * x
    return h + h * jnp.tanh(h)


def _softplus(x):
    return jnp.maximum(x, 0.0) + jnp.log1p(jnp.exp(-jnp.abs(x)))


def _dot(a, b):
    return jnp.dot(a, b, preferred_element_type=F32)


def _dot_nt(a, b):
    return lax.dot_general(a, b, (((1,), (1,)), ((), ())), preferred_element_type=F32)


def _dot_tn(a, b):
    return lax.dot_general(a, b, (((0,), (0,)), ((), ())), preferred_element_type=F32)


def _left_half(shape):
    return lax.broadcasted_iota(jnp.int32, shape, len(shape) - 1) < HEAD_DIM


def _attn_block(qt_blk, kwin, vtwin, mask_t, sinks_ref, between=None):
    nq = Q_PER_KV * BLK
    mask4 = jnp.concatenate([mask_t] * Q_PER_KV, axis=1)
    zeros = jnp.zeros((HEAD_DIM, nq), BF16)
    pieces = []
    for kv in range(N_KV_HEADS):
        slab, half = kv // 2, kv % 2
        rows = slice(slab * LANES, (slab + 1) * LANES)
        qk = jnp.concatenate([qt_blk[(kv * Q_PER_KV + g) * HEAD_DIM:(kv * Q_PER_KV + g + 1) * HEAD_DIM, :]
                              for g in range(Q_PER_KV)], axis=1)
        rhs = jnp.concatenate([qk, zeros] if half == 0 else [zeros, qk], axis=0)
        s = _dot(kwin[:, rows], rhs)
        if between is not None:
            between()
        s = jnp.where(mask4, s, -jnp.inf)
        sink = jnp.concatenate([jnp.full((1, BLK), sinks_ref[kv * Q_PER_KV + g], F32) for g in range(Q_PER_KV)],
                               axis=1)
        m = jnp.maximum(jnp.max(s, axis=0, keepdims=True), sink)
        e = jnp.exp(s - m)
        denom = jnp.sum(e, axis=0, keepdims=True) + jnp.exp(sink - m)
        p = (e * (1.0 / denom)).astype(BF16)
        o = _dot(vtwin[rows, :], p)[half * HEAD_DIM:(half + 1) * HEAD_DIM, :]
        pieces += [o[:, g * BLK:(g + 1) * BLK] for g in range(Q_PER_KV)]
    return jnp.concatenate(pieces, axis=0)


def _split3(x):
    hi = x.astype(BF16)
    r1 = x - hi.astype(F32)
    mid = r1.astype(BF16)
    lo = (r1 - mid.astype(F32)).astype(BF16)
    return hi, mid, lo


def _ssd_block(xc, dt, a_row, dskip_ref, ht_ref, y_ref, r0, between=None):
    left = _left_half((BLK, LANES))
    row_i = lax.broadcasted_iota(jnp.int32, (BLK, BLK), 0)
    col_i = lax.broadcasted_iota(jnp.int32, (BLK, BLK), 1)
    tri = col_i <= row_i
    tri_b = tri.astype(BF16)
    da = dt * a_row
    hi, mid, lo = _split3(da)
    cs = (_dot(tri_b, hi) + _dot(tri_b, mid)) + _dot(tri_b, lo)
    cs_t = cs.T

    def colb(a, h):
        return jnp.broadcast_to(a[:, h:h + 1], (BLK, LANES))

    for g in range(SSM_GROUPS):
        bg = xc((D_SSM + g * SSM_STATE) // LANES)
        cg = xc((D_SSM + D_BC + g * SSM_STATE) // LANES).astype(BF16)
        cb = _dot_nt(cg, bg.astype(BF16))
        gs = slice(g * 2 * LANES, (g + 1) * 2 * LANES)
        ht_g = ht_ref[:, gs]
        yoff = _dot(cg, ht_g.astype(BF16))
        if between is not None:
            between()
        xdec, cdec = [], []
        for hp in range(2):
            h0 = g * 4 + 2 * hp
            h1 = h0 + 1
            cols = slice((g * 2 + hp) * LANES, (g * 2 + hp + 1) * LANES)
            xs2 = xc(g * 2 + hp)
            csb = [colb(cs, h0), colb(cs, h1)]
            cs2 = jnp.where(left, csb[0], csb[1])
            ecs2 = jnp.exp(cs2)
            dec2 = jnp.exp(cs2[BLK - 1:BLK, :] - cs2)
            dt2 = jnp.where(left, colb(dt, h0), colb(dt, h1))
            xd2 = xs2 * dt2
            ms = []
            for idx, h in enumerate((h0, h1)):
                seg = csb[idx] - jnp.broadcast_to(cs_t[h:h + 1, :], (BLK, BLK))
                ms.append((cb * jnp.exp(jnp.where(tri, seg, -jnp.inf))).astype(BF16))
            yd = _dot(jnp.concatenate(ms, axis=0), xd2.astype(BF16))
            dsk2 = jnp.where(left, dskip_ref[h0], dskip_ref[h1])
            y2 = (jnp.where(left, yd[0:BLK], yd[BLK:]) + yoff[:, hp * LANES:(hp + 1) * LANES] * ecs2) + xs2 * dsk2
            y_ref[pl.ds(r0, BLK), cols] = y2
            xdec.append((xd2 * dec2).astype(BF16))
            cdec.append(ecs2[BLK - 1:BLK, :])
        xdec = jnp.concatenate(xdec, axis=1)
        cdec = jnp.concatenate(cdec, axis=1)
        ht_ref[:, gs] = cdec * ht_g + _dot(bg.T.astype(BF16), xdec)


def _gated_norm(y, z, w):
    g = y * _silu(z)
    parts = []
    width = D_SSM // SSM_GROUPS
    for i in range(SSM_GROUPS):
        gg = g[:, i * width:(i + 1) * width]
        parts.append(gg * lax.rsqrt(jnp.mean(gg * gg, axis=-1, keepdims=True) + EPS))
    return jnp.concatenate(parts, axis=1) * w


def _conv(xbc_ref, rows, conv_w_ref, conv_b_ref):
    acc = conv_b_ref[...]
    for i in range(CONV_WIDTH):
        off = HIST - (CONV_WIDTH - 1) + i
        acc = acc + xbc_ref[off:off + rows, :] * conv_w_ref[i:i + 1, :]
    return _silu(acc)


CONV_PHASES = 4


def _conv_slab(xbc_ref, xc_ref, c, rows, conv_w_ref, conv_b_ref):
    n = rows // CONV_PHASES
    cols = slice(c * LANES, (c + 1) * LANES)
    taps = [conv_w_ref[i:i + 1, cols] for i in range(CONV_WIDTH)]
    bias = conv_b_ref[:, cols]
    lo = -(CONV_WIDTH - 1)
    shifted = [xbc_ref[c, pl.ds(HIST + s, n, stride=CONV_PHASES), :] for s in range(lo, CONV_PHASES)]
    for r in range(CONV_PHASES):
        acc = bias
        for i in range(CONV_WIDTH):
            acc = acc + shifted[r + i] * taps[i]
        xc_ref[c, pl.ds(r, n, stride=CONV_PHASES), :] = _silu(acc)


def _nat_proj(u, w_nat_ref, lo, hi):
    return _dot(u, w_nat_ref[:, lo:hi])


def _out_proj(x, attn_t, yn, w_out_ref):
    return x + (_dot_tn(attn_t, w_out_ref[0:D_ATTN, :]) + _dot(yn, w_out_ref[D_ATTN:, :]))


def _mix_prompt_kernel(sinks_ref, dskip_ref, xp_ref, xo_ref, g1_ref, w_qvt_ref, w_nat_ref, conv_w_ref, conv_b_ref,
                       dtb_ref, alog_ref, normw_ref, w_out_ref,
                       h1_ref, kout_ref, vout_ref, convout_ref, ssmout_ref,
                       qt_s, kh_s, vth_s, z_s, xbc_s, xc_s, dt_s, y_s, attnt_s, yn_s, ht_s, *, ts, nj):
    t = pl.program_id(0)
    sp = t % 2
    sc = 1 - sp
    jp = t % nj
    jc = (t + nj - 1) % nj

    @pl.when(t == 0)
    def _():
        for ref in (qt_s, kh_s, vth_s, z_s, xbc_s, xc_s, dt_s, attnt_s, yn_s):
            ref[...] = jnp.zeros(ref.shape, ref.dtype)

    @pl.when(jc == 0)
    def _():
        ht_s[...] = jnp.zeros(ht_s.shape, F32)

    def project():
        fresh = jp == 0
        kh_s[sp, 0:WINDOW, :] = jnp.where(fresh, 0.0, kh_s[sc, ts:ts + WINDOW, :])
        vth_s[sp, :, 0:WINDOW] = jnp.where(fresh, 0.0, vth_s[sc, :, ts:ts + WINDOW])
        xbc_s[sp, :, 0:HIST, :] = jnp.where(fresh, 0.0, xbc_s[sc, :, ts:ts + HIST, :])
        u = _rms(xp_ref[0], g1_ref[...]).astype(BF16)
        attn_t = attnt_s[sp]
        yn = yn_s[sp]
        yield
        n_piece = 4
        w = CONV_DIM // n_piece
        for c in range(n_piece):
            piece = _nat_proj(u, w_nat_ref, N_X + c * w, N_X + (c + 1) * w)
            for k in range(w // LANES):
                xbc_s[sp, c * (w // LANES) + k, HIST:HIST + ts, :] = piece[:, k * LANES:(k + 1) * LANES]
            yield
        per_piece = CONV_DIM // LANES // n_piece
        for c in range(n_piece):
            wo = D_MODEL // n_piece
            cols = slice(c * wo, (c + 1) * wo)
            h1_ref[0, :, cols] = xo_ref[0, :, cols] + (_dot_tn(attn_t, w_out_ref[0:D_ATTN, cols])
                                                       + _dot(yn, w_out_ref[D_ATTN:, cols]))
            for k in range(per_piece):
                _conv_slab(xbc_s.at[sp], xc_s.at[sp], c * per_piece + k, ts, conv_w_ref, conv_b_ref)
            yield
            wq = D_ATTN // n_piece
            qt_s[sp, c * wq:(c + 1) * wq, :] = (_dot_nt(w_qvt_ref[c * wq:(c + 1) * wq, :], u) * SCALE).astype(BF16)
            yield
        w = D_SSM // 2
        for c in range(2):
            z_s[sp, :, c * w:(c + 1) * w] = _nat_proj(u, w_nat_ref, N_Z + c * w, N_Z + (c + 1) * w)
            yield
        vth_s[sp, :, WINDOW:WINDOW + ts] = _dot_nt(w_qvt_ref[D_ATTN:, :], u)
        kh_s[sp, WINDOW:WINDOW + ts, :] = _nat_proj(u, w_nat_ref, N_K, N_Z)
        yield
        dt_s[sp] = _softplus(_nat_proj(u, w_nat_ref, N_DT, D_NAT) + dtb_ref[...])
        yield

    pieces = project()
    advance = lambda: next(pieces, None)
    advance()

    a_row = -jnp.exp(alog_ref[...])
    key = lax.broadcasted_iota(jnp.int32, (2 * BLK, BLK), 0)
    qc = lax.broadcasted_iota(jnp.int32, (2 * BLK, BLK), 1) // CHUNK
    band = (key >= qc * CHUNK) & (key < (qc + 3) * CHUNK)
    for p in range(ts // BLK):
        r0 = p * BLK
        first = jnp.where(jc == 0, WINDOW, 0) if p == 0 else 0
        mask_t = band & (key >= first)
        o_t = _attn_block(qt_s[sc, :, r0:r0 + BLK], kh_s[sc, r0:r0 + 2 * BLK, :].astype(BF16),
                          vth_s[sc, :, r0:r0 + 2 * BLK].astype(BF16), mask_t, sinks_ref, advance)
        attnt_s[sc, :, r0:r0 + BLK] = o_t.astype(BF16)
        _ssd_block(lambda i, r0=r0: xc_s[sc, i, r0:r0 + BLK, :], dt_s[sc, r0:r0 + BLK, :], a_row, dskip_ref, ht_s,
                   y_s, r0, advance)
    for _ in pieces:
        pass
    yn_s[sc] = _gated_norm(y_s[...], z_s[sc], normw_ref[...]).astype(BF16)

    @pl.when(jc == nj - 1)
    def _():
        kout_ref[0] = kh_s[sc, ts:ts + WINDOW, :]
        vout_ref[0] = vth_s[sc, :, ts:ts + WINDOW].T
        for c in range(CONV_DIM // LANES):
            convout_ref[0, :, c * LANES:(c + 1) * LANES] = xbc_s[sc, c, HIST + ts - (CONV_WIDTH - 1):HIST + ts, :]
        ssmout_ref[0] = ht_s[...].T


def _mix_prompt(x, p, ts):
    b, s, _ = x.shape
    nj = s // ts
    nt = b * nj
    const2 = lambda t: (0, 0)
    prev2 = lambda t: (jnp.maximum(t - 2, 0), 0, 0)
    per_stream = lambda t: (jnp.clip(t - 1, 0, nt - 1) // nj, 0, 0)
    smem = pl.BlockSpec(memory_space=pltpu.SMEM)
    in_specs = [
        smem, smem,
        pl.BlockSpec((1, ts, D_MODEL), lambda t: (jnp.minimum(t, nt - 1), 0, 0)),
        pl.BlockSpec((1, ts, D_MODEL), prev2),
        pl.BlockSpec((1, D_MODEL), const2),
        pl.BlockSpec((D_ATTN + D_KV, D_MODEL), const2),
        pl.BlockSpec((D_MODEL, D_NAT), const2),
        pl.BlockSpec((CONV_WIDTH, CONV_DIM), const2),
        pl.BlockSpec((1, CONV_DIM), const2),
        pl.BlockSpec((1, DT_PAD), const2),
        pl.BlockSpec((1, DT_PAD), const2),
        pl.BlockSpec((1, D_SSM), const2),
        pl.BlockSpec((D_MIX, D_MODEL), const2),
    ]
    out_shape = (
        jax.ShapeDtypeStruct((nt, ts, D_MODEL), F32),
        jax.ShapeDtypeStruct((b, WINDOW, D_KV), F32),
        jax.ShapeDtypeStruct((b, WINDOW, D_KV), F32),
        jax.ShapeDtypeStruct((b, CONV_WIDTH - 1, CONV_DIM), F32),
        jax.ShapeDtypeStruct((b, D_SSM, SSM_STATE), F32),
    )
    out_specs = (
        pl.BlockSpec((1, ts, D_MODEL), prev2),
        pl.BlockSpec((1, WINDOW, D_KV), per_stream),
        pl.BlockSpec((1, WINDOW, D_KV), per_stream),
        pl.BlockSpec((1, CONV_WIDTH - 1, CONV_DIM), per_stream),
        pl.BlockSpec((1, D_SSM, SSM_STATE), per_stream),
    )
    scratch = [
        pltpu.VMEM((2, D_ATTN, ts), BF16),
        pltpu.VMEM((2, WINDOW + ts, D_KV), F32),
        pltpu.VMEM((2, D_KV, WINDOW + ts), F32),
        pltpu.VMEM((2, ts, D_SSM), F32),
        pltpu.VMEM((2, CONV_DIM // LANES, HIST + ts, LANES), F32),
        pltpu.VMEM((2, CONV_DIM // LANES, ts, LANES), F32),
        pltpu.VMEM((2, ts, DT_PAD), F32),
        pltpu.VMEM((ts, D_SSM), F32),
        pltpu.VMEM((2, D_ATTN, ts), BF16),
        pltpu.VMEM((2, ts, D_SSM), BF16),
        pltpu.VMEM((SSM_STATE, D_SSM), F32),
    ]
    x3 = x.reshape(nt, ts, D_MODEL)
    outs = pl.pallas_call(
        functools.partial(_mix_prompt_kernel, ts=ts, nj=nj),
        grid=(nt + 2,),
        in_specs=in_specs,
        out_specs=out_specs,
        out_shape=out_shape,
        scratch_shapes=scratch,
        compiler_params=pltpu.CompilerParams(dimension_semantics=("arbitrary",), vmem_limit_bytes=VMEM_LIMIT),
        name="mix_prompt",
    )(p["sinks"], p["d_skip"], x3, x3, p["ln1_g"], p["w_qvt"], p["w_nat"], p["conv_w"], p["conv_b"], p["dt_bias"],
      p["a_log"], p["ssm_norm_w"], p["w_out"])
    return (outs[0].reshape(b, s, D_MODEL),) + tuple(outs[1:])


def _mix_sample_kernel(sinks_ref, dskip_ref, x_ref, ck_ref, cv_ref, sconv_ref, sssm_ref, g1_ref, w_qvt_ref,
                       w_nat_ref, conv_w_ref, conv_b_ref, dtb_ref, alog_ref, normw_ref, w_out_ref,
                       h1_ref, kout_ref, vout_ref, convout_ref, ssmout_ref,
                       qt_s, vt_s, z_s, xbcall_s, dt_s, xbc_s, y_s, attnt_s, yn_s, ht_s, *, nb, t):
    i = pl.program_id(0)
    n = nb * t
    per_tile = BLK // t

    @pl.when(i == 0)
    def _():
        u = _rms(x_ref[...], g1_ref[...]).astype(BF16)
        qt = (_dot_nt(w_qvt_ref[0:D_ATTN, :], u) * SCALE).astype(BF16)
        vt = _dot_nt(w_qvt_ref[D_ATTN:, :], u)
        for c in range(n // BLK):
            qt_s[c] = qt[:, c * BLK:(c + 1) * BLK]
            vt_s[c] = vt[:, c * BLK:(c + 1) * BLK]
        vout_ref[...] = vt.T
        kout_ref[...] = _nat_proj(u, w_nat_ref, N_K, N_Z)
        z_s[...] = _nat_proj(u, w_nat_ref, N_Z, N_X)
        xbcall_s[...] = _nat_proj(u, w_nat_ref, N_X, N_DT)
        dt_s[...] = _softplus(_nat_proj(u, w_nat_ref, N_DT, D_NAT) + dtb_ref[...])
        attnt_s[...] = jnp.zeros(attnt_s.shape, BF16)
        xbc_s[...] = jnp.zeros(xbc_s.shape, F32)

    @pl.when((i >= 1) & (i <= nb))
    def _():
        b = i - 1
        rows = pl.ds(pl.multiple_of(b * t, t), t)
        c = b // per_tile
        lo = (b % per_tile) * t
        kwin = jnp.concatenate([kout_ref[pl.ds(pl.multiple_of(c * BLK, BLK), BLK), :], ck_ref[0]], axis=0)
        vtwin = jnp.concatenate([vt_s[c], cv_ref[0].T], axis=1)
        key = lax.broadcasted_iota(jnp.int32, (2 * BLK, BLK), 0)
        mask_t = ((key >= lo) & (key < lo + t)) | (key >= BLK)
        o_t = _attn_block(qt_s[c], kwin.astype(BF16), vtwin.astype(BF16), mask_t, sinks_ref)
        lane = lax.broadcasted_iota(jnp.int32, (D_ATTN, BLK), 1)
        attnt_s[c] = jnp.where((lane >= lo) & (lane < lo + t), o_t.astype(BF16), attnt_s[c])

        xbc_s[HIST - (CONV_WIDTH - 1):HIST, :] = sconv_ref[0]
        xbc_s[HIST:HIST + t, :] = xbcall_s[rows, :]
        convout_ref[0] = xbc_s[HIST + t - (CONV_WIDTH - 1):HIST + t, :]
        live = lax.broadcasted_iota(jnp.int32, (BLK, 1), 0) < t
        xc = jnp.where(live, _conv(xbc_s, BLK, conv_w_ref, conv_b_ref), 0.0)
        dt = jnp.concatenate([dt_s[rows, :], jnp.zeros((BLK - t, DT_PAD), F32)], axis=0)
        ht_s[...] = sssm_ref[0].T
        _ssd_block(lambda i: xc[:, i * LANES:(i + 1) * LANES], dt, -jnp.exp(alog_ref[...]), dskip_ref, ht_s, y_s, 0)
        ssmout_ref[0] = ht_s[...].T
        yn_s[rows, :] = _gated_norm(y_s[0:t, :], z_s[rows, :], normw_ref[...]).astype(BF16)

    @pl.when(i == nb + 1)
    def _():
        for c in range(n // BLK):
            r = slice(c * BLK, (c + 1) * BLK)
            h1_ref[r, :] = _out_proj(x_ref[r, :], attnt_s[c], yn_s[r, :], w_out_ref)


def _mix_sample(x, cache_k, cache_v, state_conv, state_ssm, p):
    nb, t, _ = x.shape
    n = nb * t
    assert BLK % t == 0 and n % BLK == 0
    const2 = lambda i: (0, 0)
    per = lambda i: (jnp.clip(i - 1, 0, nb - 1), 0, 0)
    smem = pl.BlockSpec(memory_space=pltpu.SMEM)
    in_specs = [
        smem, smem,
        pl.BlockSpec((n, D_MODEL), const2),
        pl.BlockSpec((1, WINDOW, D_KV), per),
        pl.BlockSpec((1, WINDOW, D_KV), per),
        pl.BlockSpec((1, CONV_WIDTH - 1, CONV_DIM), per),
        pl.BlockSpec((1, D_SSM, SSM_STATE), per),
        pl.BlockSpec((1, D_MODEL), const2),
        pl.BlockSpec((D_ATTN + D_KV, D_MODEL), const2),
        pl.BlockSpec((D_MODEL, D_NAT), const2),
        pl.BlockSpec((CONV_WIDTH, CONV_DIM), const2),
        pl.BlockSpec((1, CONV_DIM), const2),
        pl.BlockSpec((1, DT_PAD), const2),
        pl.BlockSpec((1, DT_PAD), const2),
        pl.BlockSpec((1, D_SSM), const2),
        pl.BlockSpec((D_MIX, D_MODEL), const2),
    ]
    out_shape = (
        jax.ShapeDtypeStruct((n, D_MODEL), F32),
        jax.ShapeDtypeStruct((n, D_KV), F32),
        jax.ShapeDtypeStruct((n, D_KV), F32),
        jax.ShapeDtypeStruct((nb, CONV_WIDTH - 1, CONV_DIM), F32),
        jax.ShapeDtypeStruct((nb, D_SSM, SSM_STATE), F32),
    )
    out_specs = (
        pl.BlockSpec((n, D_MODEL), const2),
        pl.BlockSpec((n, D_KV), const2),
        pl.BlockSpec((n, D_KV), const2),
        pl.BlockSpec((1, CONV_WIDTH - 1, CONV_DIM), per),
        pl.BlockSpec((1, D_SSM, SSM_STATE), per),
    )
    scratch = [
        pltpu.VMEM((n // BLK, D_ATTN, BLK), BF16),
        pltpu.VMEM((n // BLK, D_KV, BLK), F32),
        pltpu.VMEM((n, D_SSM), F32),
        pltpu.VMEM((n, CONV_DIM), F32),
        pltpu.VMEM((n, DT_PAD), F32),
        pltpu.VMEM((HIST + BLK, CONV_DIM), F32),
        pltpu.VMEM((BLK, D_SSM), F32),
        pltpu.VMEM((n // BLK, D_ATTN, BLK), BF16),
        pltpu.VMEM((n, D_SSM), BF16),
        pltpu.VMEM((SSM_STATE, D_SSM), F32),
    ]
    return pl.pallas_call(
        functools.partial(_mix_sample_kernel, nb=nb, t=t),
        grid=(nb + 2,),
        in_specs=in_specs,
        out_specs=out_specs,
        out_shape=out_shape,
        scratch_shapes=scratch,
        compiler_params=pltpu.CompilerParams(dimension_semantics=("arbitrary",), vmem_limit_bytes=VMEM_LIMIT),
        name="mix_sample",
    )(p["sinks"], p["d_skip"], x.reshape(n, D_MODEL), cache_k, cache_v, state_conv, state_ssm, p["ln1_g"],
      p["w_qvt"], p["w_nat"], p["conv_w"], p["conv_b"], p["dt_bias"], p["a_log"], p["ssm_norm_w"], p["w_out"])


def _ffn_kernel(h_ref, g2_ref, wg_ref, wu_ref, wd_ref, gf_ref, o_ref):
    h = h_ref[...]
    u = _rms(h, g2_ref[...]).astype(BF16)
    mid = (_silu(_dot(u, wg_ref[...])) * _dot(u, wu_ref[...])).astype(BF16)
    h2 = h + _dot(mid, wd_ref[...])
    o_ref[...] = _rms(h2, gf_ref[...])


def _ffn(h, p, tm):
    n = h.shape[0]
    d_ff = p["w_gate"].shape[1]
    const2 = lambda i: (0, 0)
    return pl.pallas_call(
        _ffn_kernel,
        grid=(n // tm,),
        in_specs=[
            pl.BlockSpec((tm, D_MODEL), lambda i: (i, 0)),
            pl.BlockSpec((1, D_MODEL), const2),
            pl.BlockSpec((D_MODEL, d_ff), const2),
            pl.BlockSpec((D_MODEL, d_ff), const2),
            pl.BlockSpec((d_ff, D_MODEL), const2),
            pl.BlockSpec((1, D_MODEL), const2),
        ],
        out_specs=pl.BlockSpec((tm, D_MODEL), lambda i: (i, 0)),
        out_shape=jax.ShapeDtypeStruct((n, D_MODEL), F32),
        compiler_params=pltpu.CompilerParams(dimension_semantics=("arbitrary",), vmem_limit_bytes=VMEM_LIMIT),
        name="ffn",
    )(h, p["ln2_g"], p["w_gate"], p["w_up"], p["w_down"], p["final_g"])


def _prep_params(ln1_g, w_in, conv_w, conv_b, dt_bias, a_log, d_skip, ssm_norm_w, sinks, w_out, ln2_g, w_gate,
                 w_up, w_down, final_g):
    pad_h = lambda v: jnp.pad(v.reshape(1, SSM_HEADS), ((0, 0), (0, DT_PAD - SSM_HEADS)))
    o_k, o_v, o_z = D_ATTN, D_ATTN + D_KV, D_ATTN + 2 * D_KV
    w_q, w_k, w_v, w_rest = w_in[:, :o_k], w_in[:, o_k:o_v], w_in[:, o_v:o_z], w_in[:, o_z:]
    w_nat = jnp.concatenate([w_k, w_rest], axis=1).astype(BF16)
    return {
        "ln1_g": ln1_g.reshape(1, D_MODEL),
        "w_qvt": jnp.concatenate([w_q, w_v], axis=1).T.astype(BF16),
        "w_nat": jnp.pad(w_nat, ((0, 0), (0, D_NAT - w_nat.shape[1]))),
        "conv_w": conv_w,
        "conv_b": conv_b.reshape(1, CONV_DIM),
        "dt_bias": pad_h(dt_bias),
        "a_log": pad_h(a_log),
        "d_skip": d_skip,
        "ssm_norm_w": ssm_norm_w.reshape(1, D_SSM),
        "sinks": sinks,
        "w_out": w_out.astype(BF16),
        "ln2_g": ln2_g.reshape(1, D_MODEL),
        "w_gate": w_gate.astype(BF16),
        "w_up": w_up.astype(BF16),
        "w_down": w_down.astype(BF16),
        "final_g": final_g.reshape(1, D_MODEL),
    }


def _layer(x_prompt, x_sample, cache_k, cache_v, state_conv, state_ssm, p, ts, tm):
    b, s, _ = x_prompt.shape
    nb, t, _ = x_sample.shape
    h1p, kp, vp, cp, sp = _mix_prompt(x_prompt, p, ts)
    h1s, kn, vn, cn, sn = _mix_sample(x_sample, cache_k.reshape(nb, WINDOW, D_KV), cache_v.reshape(nb, WINDOW, D_KV),
                                      state_conv, state_ssm.reshape(nb, D_SSM, SSM_STATE), p)
    yp = _ffn(h1p.reshape(b * s, D_MODEL), p, tm).reshape(b, s, D_MODEL)
    ys = _ffn(h1s, p, min(tm, nb * t)).reshape(nb, t, D_MODEL)
    return (yp, ys,
            kp.reshape(b, WINDOW, N_KV_HEADS, HEAD_DIM), vp.reshape(b, WINDOW, N_KV_HEADS, HEAD_DIM), cp,
            sp.reshape(b, SSM_HEADS, SSM_HEAD_DIM, SSM_STATE),
            kn.reshape(nb, t, N_KV_HEADS, HEAD_DIM), vn.reshape(nb, t, N_KV_HEADS, HEAD_DIM), cn,
            sn.reshape(nb, SSM_HEADS, SSM_HEAD_DIM, SSM_STATE))


def kernel(x_prompt, x_sample, cache_k, cache_v, state_conv, state_ssm, ln1_g, w_in, conv_w, conv_b, dt_bias, a_log,
           d_skip, ssm_norm_w, sinks, w_out, ln2_g, w_gate, w_up, w_down, final_g):
    assert w_in.shape[0] == 1, "one layer"
    p = _prep_params(ln1_g[0], w_in[0], conv_w[0], conv_b[0], dt_bias[0], a_log[0], d_skip[0], ssm_norm_w[0],
                     sinks[0], w_out[0], ln2_g[0], w_gate[0], w_up[0], w_down[0], final_g)
    outs = _layer(x_prompt, x_sample, cache_k[0], cache_v[0], state_conv[0], state_ssm[0], p, ts=256, tm=512)
    return tuple(o[None] if i >= 2 else o for i, o in enumerate(outs))
```

```python
import functools

import jax
import jax.numpy as jnp
from jax import lax
from jax.experimental import pallas as pl
from jax.experimental.pallas import tpu as pltpu

F32 = jnp.float32
BF16 = jnp.bfloat16

D_MODEL = 1024
CHUNK = 64
WINDOW = 128
HEAD_DIM = 64
N_HEADS = 16
N_KV_HEADS = 4
Q_PER_KV = N_HEADS // N_KV_HEADS
D_ATTN = N_HEADS * HEAD_DIM
D_KV = N_KV_HEADS * HEAD_DIM
SCALE = HEAD_DIM ** -0.5
SSM_HEADS = 16
SSM_HEAD_DIM = 64
SSM_GROUPS = 4
SSM_STATE = 128
D_SSM = SSM_HEADS * SSM_HEAD_DIM
D_BC = SSM_GROUPS * SSM_STATE
CONV_WIDTH = 4
CONV_DIM = D_SSM + 2 * D_BC
D_MIX = D_ATTN + D_SSM
EPS = 1e-6

LANES = 128
BLK = 2 * CHUNK
DT_PAD = LANES
N_K, N_Z, N_X, N_DT = 0, D_KV, D_KV + D_SSM, D_KV + D_SSM + CONV_DIM
D_NAT = N_DT + DT_PAD
HIST = 8
VMEM_LIMIT = 56 * 1024 * 1024


def _rms(x, g):
    ms = jnp.mean(x * x, axis=-1, keepdims=True)
    return (x * lax.rsqrt(ms + EPS)) * g


def _silu(x):
    h = 0.5 * x
    return h + h * jnp.tanh(h)


def _softplus(x):
    return jnp.maximum(x, 0.0) + jnp.log1p(jnp.exp(-jnp.abs(x)))


def _dot(a, b):
    return jnp.dot(a, b, preferred_element_type=F32)


def _dot_nt(a, b):
    return lax.dot_general(a, b, (((1,), (1,)), ((), ())), preferred_element_type=F32)


def _dot_tn(a, b):
    return lax.dot_general(a, b, (((0,), (0,)), ((), ())), preferred_element_type=F32)


def _left_half(shape):
    return lax.broadcasted_iota(jnp.int32, shape, len(shape) - 1) < HEAD_DIM


def _attn_steps(qt_blk, kwin, vtwin, mask_t, sinks_ref, emit, between=None):
    nq = Q_PER_KV * BLK
    mask4 = jnp.concatenate([mask_t] * Q_PER_KV, axis=1)
    zeros = jnp.zeros((HEAD_DIM, nq), BF16)
    for kv in range(N_KV_HEADS):
        slab, half = kv // 2, kv % 2
        rows = slice(slab * LANES, (slab + 1) * LANES)
        qk = jnp.concatenate([qt_blk[(kv * Q_PER_KV + g) * HEAD_DIM:(kv * Q_PER_KV + g + 1) * HEAD_DIM, :]
                              for g in range(Q_PER_KV)], axis=1)
        rhs = jnp.concatenate([qk, zeros] if half == 0 else [zeros, qk], axis=0)
        s = _dot(kwin[:, rows], rhs)
        if between is not None:
            between()
        s = jnp.where(mask4, s, -jnp.inf)
        sink = jnp.concatenate([jnp.full((1, BLK), sinks_ref[kv * Q_PER_KV + g], F32) for g in range(Q_PER_KV)],
                               axis=1)
        m = jnp.maximum(jnp.max(s, axis=0, keepdims=True), sink)
        e = jnp.exp(s - m)
        denom = jnp.sum(e, axis=0, keepdims=True) + jnp.exp(sink - m)
        p = (e * (1.0 / denom)).astype(BF16)
        o = _dot(vtwin[kv * HEAD_DIM:(kv + 1) * HEAD_DIM, :], p)
        emit(kv, jnp.concatenate([o[:, g * BLK:(g + 1) * BLK] for g in range(Q_PER_KV)], axis=0))
        yield


def _split3(x):
    hi = x.astype(BF16)
    r1 = x - hi.astype(F32)
    mid = r1.astype(BF16)
    lo = (r1 - mid.astype(F32)).astype(BF16)
    return hi, mid, lo


def _ssd_steps(xc, dt, a_row, dskip_ref, ht_ref, y_ref, r0, between=None):
    left = _left_half((BLK, LANES))
    row_i = lax.broadcasted_iota(jnp.int32, (BLK, BLK), 0)
    col_i = lax.broadcasted_iota(jnp.int32, (BLK, BLK), 1)
    tri = col_i <= row_i
    tri_b = tri.astype(BF16)
    da = dt * a_row
    hi, mid, lo = _split3(da)
    parts = _dot(tri_b, jnp.concatenate([hi, mid, lo], axis=1))
    cs = (parts[:, 0:LANES] + parts[:, LANES:2 * LANES]) + parts[:, 2 * LANES:]
    cs_t = cs.T
    if between is not None:
        between()

    def colb(a, h):
        return jnp.broadcast_to(a[:, h:h + 1], (BLK, LANES))

    for g in range(SSM_GROUPS):
        bg = xc((D_SSM + g * SSM_STATE) // LANES)
        cg = xc((D_SSM + D_BC + g * SSM_STATE) // LANES).astype(BF16)
        cb = _dot_nt(cg, bg.astype(BF16))
        gs = slice(g * 2 * LANES, (g + 1) * 2 * LANES)
        ht_g = ht_ref[:, gs]
        yoff = _dot(cg, ht_g.astype(BF16))
        if between is not None:
            between()
        xdec, cdec = [], []
        for hp in range(2):
            h0 = g * 4 + 2 * hp
            h1 = h0 + 1
            cols = slice((g * 2 + hp) * LANES, (g * 2 + hp + 1) * LANES)
            xs2 = xc(g * 2 + hp)
            csb = [colb(cs, h0), colb(cs, h1)]
            cs2 = jnp.where(left, csb[0], csb[1])
            ecs2 = jnp.exp(cs2)
            dec2 = jnp.exp(cs2[BLK - 1:BLK, :] - cs2)
            dt2 = jnp.where(left, colb(dt, h0), colb(dt, h1))
            xd2 = xs2 * dt2
            ms = []
            for idx, h in enumerate((h0, h1)):
                seg = csb[idx] - jnp.broadcast_to(cs_t[h:h + 1, :], (BLK, BLK))
                ms.append((cb * jnp.exp(jnp.where(tri, seg, -jnp.inf))).astype(BF16))
            xd2b = xd2.astype(BF16)
            zero = jnp.zeros_like(xd2b)
            xd_diag = jnp.concatenate([jnp.where(left, xd2b, zero), jnp.where(left, zero, xd2b)], axis=0)
            yd = _dot(jnp.concatenate(ms, axis=1), xd_diag)
            dsk2 = jnp.where(left, dskip_ref[h0], dskip_ref[h1])
            y2 = (yd + yoff[:, hp * LANES:(hp + 1) * LANES] * ecs2) + xs2 * dsk2
            y_ref[pl.ds(r0, BLK), cols] = y2
            xdec.append((xd2 * dec2).astype(BF16))
            cdec.append(ecs2[BLK - 1:BLK, :])
        xdec = jnp.concatenate(xdec, axis=1)
        cdec = jnp.concatenate(cdec, axis=1)
        ht_ref[:, gs] = cdec * ht_g + _dot(bg.T.astype(BF16), xdec)
        yield


def _gated_norm(y, z, w):
    g = y * _silu(z)
    parts = []
    width = D_SSM // SSM_GROUPS
    for i in range(SSM_GROUPS):
        gg = g[:, i * width:(i + 1) * width]
        parts.append(gg * lax.rsqrt(jnp.mean(gg * gg, axis=-1, keepdims=True) + EPS))
    return jnp.concatenate(parts, axis=1) * w


def _conv(xbc_ref, rows, conv_w_ref, conv_b_ref):
    acc = conv_b_ref[...]
    for i in range(CONV_WIDTH):
        off = HIST - (CONV_WIDTH - 1) + i
        acc = acc + xbc_ref[off:off + rows, :] * conv_w_ref[i:i + 1, :]
    return _silu(acc)


CONV_PHASES = 4


def _conv_slab(xbc_ref, xc_ref, c, rows, conv_w_ref, conv_b_ref):
    n = rows // CONV_PHASES
    cols = slice(c * LANES, (c + 1) * LANES)
    taps = [conv_w_ref[i:i + 1, cols] for i in range(CONV_WIDTH)]
    bias = conv_b_ref[:, cols]
    lo = -(CONV_WIDTH - 1)
    shifted = [xbc_ref[c, pl.ds(HIST + s, n, stride=CONV_PHASES), :] for s in range(lo, CONV_PHASES)]
    for r in range(CONV_PHASES):
        acc = bias
        for i in range(CONV_WIDTH):
            acc = acc + shifted[r + i] * taps[i]
        xc_ref[c, pl.ds(r, n, stride=CONV_PHASES), :] = _silu(acc)


def _nat_proj(u, w_nat_ref, lo, hi):
    return _dot(u, w_nat_ref[:, lo:hi])


def _out_proj(x, attn_t, yn, w_out_ref):
    return x + (_dot_tn(attn_t, w_out_ref[0:D_ATTN, :]) + _dot(yn, w_out_ref[D_ATTN:, :]))


def _mix_prompt_kernel(sinks_ref, dskip_ref, xp_ref, xo_ref, g1_ref, w_qvt_ref, w_nat_ref, conv_w_ref, conv_b_ref,
                       dtb_ref, alog_ref, normw_ref, w_out_ref,
                       h1_ref, kout_ref, vout_ref, convout_ref, ssmout_ref,
                       qt_s, kh_s, vth_s, z_s, xbc_s, xc_s, dt_s, y_s, attnt_s, yn_s, ht_s, *, ts, nj):
    t = pl.program_id(0)
    sp = t % 2
    sc = 1 - sp
    jp = t % nj
    jc = (t + nj - 1) % nj

    @pl.when(t == 0)
    def _():
        for ref in (qt_s, kh_s, vth_s, z_s, xbc_s, xc_s, dt_s, attnt_s, yn_s):
            ref[...] = jnp.zeros(ref.shape, ref.dtype)

    @pl.when(jc == 0)
    def _():
        ht_s[...] = jnp.zeros(ht_s.shape, F32)

    def project():
        fresh = jp == 0
        kh_s[sp, 0:WINDOW, :] = jnp.where(fresh, 0.0, kh_s[sc, ts:ts + WINDOW, :])
        vth_s[sp, :, 0:WINDOW] = jnp.where(fresh, 0.0, vth_s[sc, :, ts:ts + WINDOW])
        xbc_s[sp, :, 0:HIST, :] = jnp.where(fresh, 0.0, xbc_s[sc, :, ts:ts + HIST, :])
        u = _rms(xp_ref[0], g1_ref[...]).astype(BF16)
        attn_t = attnt_s[sp]
        yn = yn_s[sp]
        n_piece = 4
        slabs = CONV_DIM // LANES // n_piece

        def xbc(c):
            w = CONV_DIM // n_piece
            piece = _nat_proj(u, w_nat_ref, N_X + c * w, N_X + (c + 1) * w)
            for k in range(slabs):
                xbc_s[sp, c * slabs + k, HIST:HIST + ts, :] = piece[:, k * LANES:(k + 1) * LANES]

        def conv(c):
            for k in range(slabs):
                _conv_slab(xbc_s.at[sp], xc_s.at[sp], c * slabs + k, ts, conv_w_ref, conv_b_ref)

        def q(c):
            w = D_ATTN // n_piece
            qt_s[sp, c * w:(c + 1) * w, :] = (_dot_nt(w_qvt_ref[c * w:(c + 1) * w, :], u) * SCALE).astype(BF16)

        def z(c):
            w = D_SSM // n_piece
            z_s[sp, :, c * w:(c + 1) * w] = _nat_proj(u, w_nat_ref, N_Z + c * w, N_Z + (c + 1) * w)

        def out(c):
            w = D_MODEL // n_piece
            cols = slice(c * w, (c + 1) * w)
            h1_ref[0, :, cols] = xo_ref[0, :, cols] + (_dot_tn(attn_t, w_out_ref[0:D_ATTN, cols])
                                                       + _dot(yn, w_out_ref[D_ATTN:, cols]))

        def v():
            vth_s[sp, :, WINDOW:WINDOW + ts] = _dot_nt(w_qvt_ref[D_ATTN:, :], u)

        def k():
            kh_s[sp, WINDOW:WINDOW + ts, :] = _nat_proj(u, w_nat_ref, N_K, N_Z)

        def dt():
            dt_s[sp] = _softplus(_nat_proj(u, w_nat_ref, N_DT, D_NAT) + dtb_ref[...])

        placement = [
            [(xbc, 0)], [(xbc, 1), (conv, 0)], [(xbc, 2), (conv, 1)], [(xbc, 3), (conv, 2)],
            [(q, 0)], [(q, 1)], [(q, 2)], [(q, 3)], [(z, 0)],
            [(out, 0), (conv, 3)], [(out, 1)], [(out, 2)], [(out, 3)],
            [(z, 1)], [(z, 2)], [(z, 3)], [(v,)], [(k,)],
            [(dt,)],
        ]
        for group in placement:
            yield
            for fn, *args in group:
                fn(*args)

    pieces = project()
    advance = lambda: next(pieces, None)
    advance()

    a_row = -jnp.exp(alog_ref[...])
    key = lax.broadcasted_iota(jnp.int32, (2 * BLK, BLK), 0)
    qc = lax.broadcasted_iota(jnp.int32, (2 * BLK, BLK), 1) // CHUNK
    band = (key >= qc * CHUNK) & (key < (qc + 3) * CHUNK)
    for p in range(ts // BLK):
        r0 = p * BLK
        first = jnp.where(jc == 0, WINDOW, 0) if p == 0 else 0
        mask_t = band & (key >= first)

        def emit(kv, o, r0=r0):
            attnt_s[sc, kv * Q_PER_KV * HEAD_DIM:(kv + 1) * Q_PER_KV * HEAD_DIM, r0:r0 + BLK] = o.astype(BF16)

        attn = _attn_steps(qt_s[sc, :, r0:r0 + BLK], kh_s[sc, r0:r0 + 2 * BLK, :].astype(BF16),
                           vth_s[sc, :, r0:r0 + 2 * BLK].astype(BF16), mask_t, sinks_ref, emit, advance)
        ssd = _ssd_steps(lambda i, r0=r0: xc_s[sc, i, r0:r0 + BLK, :], dt_s[sc, r0:r0 + BLK, :], a_row, dskip_ref,
                         ht_s, y_s, r0, advance)
        for _ in attn:
            pass
        for _ in ssd:
            pass
    for p in range(ts // BLK):
        rows = slice(p * BLK, (p + 1) * BLK)
        advance()
        yn_s[sc, rows, :] = _gated_norm(y_s[rows, :], z_s[sc, rows, :], normw_ref[...]).astype(BF16)
    for _ in pieces:
        pass

    @pl.when(jc == nj - 1)
    def _():
        kout_ref[0] = kh_s[sc, ts:ts + WINDOW, :]
        vout_ref[0] = vth_s[sc, :, ts:ts + WINDOW].T
        for c in range(CONV_DIM // LANES):
            convout_ref[0, :, c * LANES:(c + 1) * LANES] = xbc_s[sc, c, HIST + ts - (CONV_WIDTH - 1):HIST + ts, :]
        ssmout_ref[0] = ht_s[...].T


def _mix_prompt(x, p, ts):
    b, s, _ = x.shape
    nj = s // ts
    nt = b * nj
    const2 = lambda t: (0, 0)
    prev2 = lambda t: (jnp.maximum(t - 2, 0), 0, 0)
    per_stream = lambda t: (jnp.clip(t - 1, 0, nt - 1) // nj, 0, 0)
    smem = pl.BlockSpec(memory_space=pltpu.SMEM)
    in_specs = [
        smem, smem,
        pl.BlockSpec((1, ts, D_MODEL), lambda t: (jnp.minimum(t, nt - 1), 0, 0)),
        pl.BlockSpec((1, ts, D_MODEL), prev2),
        pl.BlockSpec((1, D_MODEL), const2),
        pl.BlockSpec((D_ATTN + D_KV, D_MODEL), const2),
        pl.BlockSpec((D_MODEL, D_NAT), const2),
        pl.BlockSpec((CONV_WIDTH, CONV_DIM), const2),
        pl.BlockSpec((1, CONV_DIM), const2),
        pl.BlockSpec((1, DT_PAD), const2),
        pl.BlockSpec((1, DT_PAD), const2),
        pl.BlockSpec((1, D_SSM), const2),
        pl.BlockSpec((D_MIX, D_MODEL), const2),
    ]
    out_shape = (
        jax.ShapeDtypeStruct((nt, ts, D_MODEL), F32),
        jax.ShapeDtypeStruct((b, WINDOW, D_KV), F32),
        jax.ShapeDtypeStruct((b, WINDOW, D_KV), F32),
        jax.ShapeDtypeStruct((b, CONV_WIDTH - 1, CONV_DIM), F32),
        jax.ShapeDtypeStruct((b, D_SSM, SSM_STATE), F32),
    )
    out_specs = (
        pl.BlockSpec((1, ts, D_MODEL), prev2),
        pl.BlockSpec((1, WINDOW, D_KV), per_stream),
        pl.BlockSpec((1, WINDOW, D_KV), per_stream),
        pl.BlockSpec((1, CONV_WIDTH - 1, CONV_DIM), per_stream),
        pl.BlockSpec((1, D_SSM, SSM_STATE), per_stream),
    )
    scratch = [
        pltpu.VMEM((2, D_ATTN, ts), BF16),
        pltpu.VMEM((2, WINDOW + ts, D_KV), F32),
        pltpu.VMEM((2, D_KV, WINDOW + ts), F32),
        pltpu.VMEM((2, ts, D_SSM), F32),
        pltpu.VMEM((2, CONV_DIM // LANES, HIST + ts, LANES), F32),
        pltpu.VMEM((2, CONV_DIM // LANES, ts, LANES), F32),
        pltpu.VMEM((2, ts, DT_PAD), F32),
        pltpu.VMEM((ts, D_SSM), F32),
        pltpu.VMEM((2, D_ATTN, ts), BF16),
        pltpu.VMEM((2, ts, D_SSM), BF16),
        pltpu.VMEM((SSM_STATE, D_SSM), F32),
    ]
    x3 = x.reshape(nt, ts, D_MODEL)
    outs = pl.pallas_call(
        functools.partial(_mix_prompt_kernel, ts=ts, nj=nj),
        grid=(nt + 2,),
        in_specs=in_specs,
        out_specs=out_specs,
        out_shape=out_shape,
        scratch_shapes=scratch,
        compiler_params=pltpu.CompilerParams(dimension_semantics=("arbitrary",), vmem_limit_bytes=VMEM_LIMIT),
        name="mix_prompt",
    )(p["sinks"], p["d_skip"], x3, x3, p["ln1_g"], p["w_qvt"], p["w_nat"], p["conv_w"], p["conv_b"], p["dt_bias"],
      p["a_log"], p["ssm_norm_w"], p["w_out"])
    return (outs[0].reshape(b, s, D_MODEL),) + tuple(outs[1:])


def _mix_sample_kernel(sinks_ref, dskip_ref, x_ref, ck_ref, cv_ref, sconv_ref, sssm_ref, g1_ref, w_qvt_ref,
                       w_nat_ref, conv_w_ref, conv_b_ref, dtb_ref, alog_ref, normw_ref, w_out_ref,
                       h1_ref, kout_ref, vout_ref, convout_ref, ssmout_ref,
                       qt_s, vt_s, z_s, xbcall_s, dt_s, xbc_s, y_s, attnt_s, yn_s, ht_s, *, nb, t):
    i = pl.program_id(0)
    n = nb * t
    per_tile = BLK // t

    @pl.when(i == 0)
    def _():
        u = _rms(x_ref[...], g1_ref[...]).astype(BF16)
        qt = (_dot_nt(w_qvt_ref[0:D_ATTN, :], u) * SCALE).astype(BF16)
        vt = _dot_nt(w_qvt_ref[D_ATTN:, :], u)
        for c in range(n // BLK):
            qt_s[c] = qt[:, c * BLK:(c + 1) * BLK]
            vt_s[c] = vt[:, c * BLK:(c + 1) * BLK]
        vout_ref[...] = vt.T
        kout_ref[...] = _nat_proj(u, w_nat_ref, N_K, N_Z)
        z_s[...] = _nat_proj(u, w_nat_ref, N_Z, N_X)
        xbcall_s[...] = _nat_proj(u, w_nat_ref, N_X, N_DT)
        dt_s[...] = _softplus(_nat_proj(u, w_nat_ref, N_DT, D_NAT) + dtb_ref[...])
        attnt_s[...] = jnp.zeros(attnt_s.shape, BF16)
        xbc_s[...] = jnp.zeros(xbc_s.shape, F32)

    @pl.when((i >= 1) & (i <= nb))
    def _():
        b = i - 1
        rows = pl.ds(pl.multiple_of(b * t, t), t)
        c = b // per_tile
        lo = (b % per_tile) * t
        kwin = jnp.concatenate([kout_ref[pl.ds(pl.multiple_of(c * BLK, BLK), BLK), :], ck_ref[0]], axis=0)
        vtwin = jnp.concatenate([vt_s[c], cv_ref[0].T], axis=1)
        key = lax.broadcasted_iota(jnp.int32, (2 * BLK, BLK), 0)
        mask_t = ((key >= lo) & (key < lo + t)) | (key >= BLK)
        outs = [None] * N_KV_HEADS

        def emit(kv, o):
            outs[kv] = o

        for _ in _attn_steps(qt_s[c], kwin.astype(BF16), vtwin.astype(BF16), mask_t, sinks_ref, emit):
            pass
        o_t = jnp.concatenate(outs, axis=0)
        lane = lax.broadcasted_iota(jnp.int32, (D_ATTN, BLK), 1)
        attnt_s[c] = jnp.where((lane >= lo) & (lane < lo + t), o_t.astype(BF16), attnt_s[c])

        xbc_s[HIST - (CONV_WIDTH - 1):HIST, :] = sconv_ref[0]
        xbc_s[HIST:HIST + t, :] = xbcall_s[rows, :]
        convout_ref[0] = xbc_s[HIST + t - (CONV_WIDTH - 1):HIST + t, :]
        live = lax.broadcasted_iota(jnp.int32, (BLK, 1), 0) < t
        xc = jnp.where(live, _conv(xbc_s, BLK, conv_w_ref, conv_b_ref), 0.0)
        dt = jnp.concatenate([dt_s[rows, :], jnp.zeros((BLK - t, DT_PAD), F32)], axis=0)
        ht_s[...] = sssm_ref[0].T
        for _ in _ssd_steps(lambda i: xc[:, i * LANES:(i + 1) * LANES], dt, -jnp.exp(alog_ref[...]), dskip_ref, ht_s,
                            y_s, 0):
            pass
        ssmout_ref[0] = ht_s[...].T
        yn_s[rows, :] = _gated_norm(y_s[0:t, :], z_s[rows, :], normw_ref[...]).astype(BF16)

    @pl.when(i == nb + 1)
    def _():
        for c in range(n // BLK):
            r = slice(c * BLK, (c + 1) * BLK)
            h1_ref[r, :] = _out_proj(x_ref[r, :], attnt_s[c], yn_s[r, :], w_out_ref)


def _mix_sample(x, cache_k, cache_v, state_conv, state_ssm, p):
    nb, t, _ = x.shape
    n = nb * t
    assert BLK % t == 0 and n % BLK == 0
    const2 = lambda i: (0, 0)
    per = lambda i: (jnp.clip(i - 1, 0, nb - 1), 0, 0)
    smem = pl.BlockSpec(memory_space=pltpu.SMEM)
    in_specs = [
        smem, smem,
        pl.BlockSpec((n, D_MODEL), const2),
        pl.BlockSpec((1, WINDOW, D_KV), per),
        pl.BlockSpec((1, WINDOW, D_KV), per),
        pl.BlockSpec((1, CONV_WIDTH - 1, CONV_DIM), per),
        pl.BlockSpec((1, D_SSM, SSM_STATE), per),
        pl.BlockSpec((1, D_MODEL), const2),
        pl.BlockSpec((D_ATTN + D_KV, D_MODEL), const2),
        pl.BlockSpec((D_MODEL, D_NAT), const2),
        pl.BlockSpec((CONV_WIDTH, CONV_DIM), const2),
        pl.BlockSpec((1, CONV_DIM), const2),
        pl.BlockSpec((1, DT_PAD), const2),
        pl.BlockSpec((1, DT_PAD), const2),
        pl.BlockSpec((1, D_SSM), const2),
        pl.BlockSpec((D_MIX, D_MODEL), const2),
    ]
    out_shape = (
        jax.ShapeDtypeStruct((n, D_MODEL), F32),
        jax.ShapeDtypeStruct((n, D_KV), F32),
        jax.ShapeDtypeStruct((n, D_KV), F32),
        jax.ShapeDtypeStruct((nb, CONV_WIDTH - 1, CONV_DIM), F32),
        jax.ShapeDtypeStruct((nb, D_SSM, SSM_STATE), F32),
    )
    out_specs = (
        pl.BlockSpec((n, D_MODEL), const2),
        pl.BlockSpec((n, D_KV), const2),
        pl.BlockSpec((n, D_KV), const2),
        pl.BlockSpec((1, CONV_WIDTH - 1, CONV_DIM), per),
        pl.BlockSpec((1, D_SSM, SSM_STATE), per),
    )
    scratch = [
        pltpu.VMEM((n // BLK, D_ATTN, BLK), BF16),
        pltpu.VMEM((n // BLK, D_KV, BLK), F32),
        pltpu.VMEM((n, D_SSM), F32),
        pltpu.VMEM((n, CONV_DIM), F32),
        pltpu.VMEM((n, DT_PAD), F32),
        pltpu.VMEM((HIST + BLK, CONV_DIM), F32),
        pltpu.VMEM((BLK, D_SSM), F32),
        pltpu.VMEM((n // BLK, D_ATTN, BLK), BF16),
        pltpu.VMEM((n, D_SSM), BF16),
        pltpu.VMEM((SSM_STATE, D_SSM), F32),
    ]
    return pl.pallas_call(
        functools.partial(_mix_sample_kernel, nb=nb, t=t),
        grid=(nb + 2,),
        in_specs=in_specs,
        out_specs=out_specs,
        out_shape=out_shape,
        scratch_shapes=scratch,
        compiler_params=pltpu.CompilerParams(dimension_semantics=("arbitrary",), vmem_limit_bytes=VMEM_LIMIT),
        name="mix_sample",
    )(p["sinks"], p["d_skip"], x.reshape(n, D_MODEL), cache_k, cache_v, state_conv, state_ssm, p["ln1_g"],
      p["w_qvt"], p["w_nat"], p["conv_w"], p["conv_b"], p["dt_bias"], p["a_log"], p["ssm_norm_w"], p["w_out"])


def _ffn_kernel(h_ref, g2_ref, wg_ref, wu_ref, wd_ref, gf_ref, o_ref):
    h = h_ref[...]
    u = _rms(h, g2_ref[...]).astype(BF16)
    mid = (_silu(_dot(u, wg_ref[...])) * _dot(u, wu_ref[...])).astype(BF16)
    h2 = h + _dot(mid, wd_ref[...])
    o_ref[...] = _rms(h2, gf_ref[...])


def _ffn(h, p, tm):
    n = h.shape[0]
    d_ff = p["w_gate"].shape[1]
    const2 = lambda i: (0, 0)
    return pl.pallas_call(
        _ffn_kernel,
        grid=(n // tm,),
        in_specs=[
            pl.BlockSpec((tm, D_MODEL), lambda i: (i, 0)),
            pl.BlockSpec((1, D_MODEL), const2),
            pl.BlockSpec((D_MODEL, d_ff), const2),
            pl.BlockSpec((D_MODEL, d_ff), const2),
            pl.BlockSpec((d_ff, D_MODEL), const2),
            pl.BlockSpec((1, D_MODEL), const2),
        ],
        out_specs=pl.BlockSpec((tm, D_MODEL), lambda i: (i, 0)),
        out_shape=jax.ShapeDtypeStruct((n, D_MODEL), F32),
        compiler_params=pltpu.CompilerParams(dimension_semantics=("arbitrary",), vmem_limit_bytes=VMEM_LIMIT),
        name="ffn",
    )(h, p["ln2_g"], p["w_gate"], p["w_up"], p["w_down"], p["final_g"])


def _prep_params(ln1_g, w_in, conv_w, conv_b, dt_bias, a_log, d_skip, ssm_norm_w, sinks, w_out, ln2_g, w_gate,
                 w_up, w_down, final_g):
    pad_h = lambda v: jnp.pad(v.reshape(1, SSM_HEADS), ((0, 0), (0, DT_PAD - SSM_HEADS)))
    o_k, o_v, o_z = D_ATTN, D_ATTN + D_KV, D_ATTN + 2 * D_KV
    w_q, w_k, w_v, w_rest = w_in[:, :o_k], w_in[:, o_k:o_v], w_in[:, o_v:o_z], w_in[:, o_z:]
    w_nat = jnp.concatenate([w_k, w_rest], axis=1).astype(BF16)
    return {
        "ln1_g": ln1_g.reshape(1, D_MODEL),
        "w_qvt": jnp.concatenate([w_q, w_v], axis=1).T.astype(BF16),
        "w_nat": jnp.pad(w_nat, ((0, 0), (0, D_NAT - w_nat.shape[1]))),
        "conv_w": conv_w,
        "conv_b": conv_b.reshape(1, CONV_DIM),
        "dt_bias": pad_h(dt_bias),
        "a_log": pad_h(a_log),
        "d_skip": d_skip,
        "ssm_norm_w": ssm_norm_w.reshape(1, D_SSM),
        "sinks": sinks,
        "w_out": w_out.astype(BF16),
        "ln2_g": ln2_g.reshape(1, D_MODEL),
        "w_gate": w_gate.astype(BF16),
        "w_up": w_up.astype(BF16),
        "w_down": w_down.astype(BF16),
        "final_g": final_g.reshape(1, D_MODEL),
    }


def _layer(x_prompt, x_sample, cache_k, cache_v, state_conv, state_ssm, p, ts, tm):
    b, s, _ = x_prompt.shape
    nb, t, _ = x_sample.shape
    h1p, kp, vp, cp, sp = _mix_prompt(x_prompt, p, ts)
    h1s, kn, vn, cn, sn = _mix_sample(x_sample, cache_k.reshape(nb, WINDOW, D_KV), cache_v.reshape(nb, WINDOW, D_KV),
                                      state_conv, state_ssm.reshape(nb, D_SSM, SSM_STATE), p)
    yp = _ffn(h1p.reshape(b * s, D_MODEL), p, tm).reshape(b, s, D_MODEL)
    ys = _ffn(h1s, p, min(tm, nb * t)).reshape(nb, t, D_MODEL)
    return (yp, ys,
            kp.reshape(b, WINDOW, N_KV_HEADS, HEAD_DIM), vp.reshape(b, WINDOW, N_KV_HEADS, HEAD_DIM), cp,
            sp.reshape(b, SSM_HEADS, SSM_HEAD_DIM, SSM_STATE),
            kn.reshape(nb, t, N_KV_HEADS, HEAD_DIM), vn.reshape(nb, t, N_KV_HEADS, HEAD_DIM), cn,
            sn.reshape(nb, SSM_HEADS, SSM_HEAD_DIM, SSM_STATE))


def kernel(x_prompt, x_sample, cache_k, cache_v, state_conv, state_ssm, ln1_g, w_in, conv_w, conv_b, dt_bias, a_log,
           d_skip, ssm_norm_w, sinks, w_out, ln2_g, w_gate, w_up, w_down, final_g):
    assert w_in.shape[0] == 1, "one layer"
    p = _prep_params(ln1_g[0], w_in[0], conv_w[0], conv_b[0], dt_bias[0], a_log[0], d_skip[0], ssm_norm_w[0],
                     sinks[0], w_out[0], ln2_g[0], w_gate[0], w_up[0], w_down[0], final_g)
    outs = _layer(x_prompt, x_sample, cache_k[0], cache_v[0], state_conv[0], state_ssm[0], p, ts=256, tm=512)
    return tuple(o[None] if i >= 2 else o for i, o in enumerate(outs))
```

```python
import functools

import jax
import jax.numpy as jnp
from jax import lax
from jax.experimental import pallas as pl
from jax.experimental.pallas import tpu as pltpu

F32 = jnp.float32
BF16 = jnp.bfloat16

D_MODEL = 1024
CHUNK = 64
WINDOW = 128
HEAD_DIM = 64
N_HEADS = 16
N_KV_HEADS = 4
Q_PER_KV = N_HEADS // N_KV_HEADS
D_ATTN = N_HEADS * HEAD_DIM
D_KV = N_KV_HEADS * HEAD_DIM
SCALE = HEAD_DIM ** -0.5
SSM_HEADS = 16
SSM_HEAD_DIM = 64
SSM_GROUPS = 4
SSM_STATE = 128
D_SSM = SSM_HEADS * SSM_HEAD_DIM
D_BC = SSM_GROUPS * SSM_STATE
CONV_WIDTH = 4
CONV_DIM = D_SSM + 2 * D_BC
D_MIX = D_ATTN + D_SSM
EPS = 1e-6

LANES = 128
BLK = 2 * CHUNK
DT_PAD = LANES
N_K, N_Z, N_X, N_DT = 0, D_KV, D_KV + D_SSM, D_KV + D_SSM + CONV_DIM
D_NAT = N_DT + DT_PAD
HIST = 8
VMEM_LIMIT = 56 * 1024 * 1024


def _rms(x, g):
    ms = jnp.mean(x * x, axis=-1, keepdims=True)
    return (x * lax.rsqrt(ms + EPS)) * g


def _silu(x):
    h = 0.5 * x
    return h + h * jnp.tanh(h)


def _softplus(x):
    return jnp.maximum(x, 0.0) + jnp.log1p(jnp.exp(-jnp.abs(x)))


def _dot(a, b):
    return jnp.dot(a, b, preferred_element_type=F32)


def _dot_nt(a, b):
    return lax.dot_general(a, b, (((1,), (1,)), ((), ())), preferred_element_type=F32)


def _dot_tn(a, b):
    return lax.dot_general(a, b, (((0,), (0,)), ((), ())), preferred_element_type=F32)


def _left_half(shape):
    return lax.broadcasted_iota(jnp.int32, shape, len(shape) - 1) < HEAD_DIM


def _attn_steps(qt_blk, kwin, vtwin, mask_t, sinks_ref, emit, between=None):
    nq = Q_PER_KV * BLK
    mask4 = jnp.concatenate([mask_t] * Q_PER_KV, axis=1)
    zeros = jnp.zeros((HEAD_DIM, nq), BF16)

    def scores(kv):
        slab, half = kv // 2, kv % 2
        qk = jnp.concatenate([qt_blk[(kv * Q_PER_KV + g) * HEAD_DIM:(kv * Q_PER_KV + g + 1) * HEAD_DIM, :]
                              for g in range(Q_PER_KV)], axis=1)
        rhs = jnp.concatenate([qk, zeros] if half == 0 else [zeros, qk], axis=0)
        return _dot(kwin[:, slab * LANES:(slab + 1) * LANES], rhs)

    s_next = scores(0)
    for kv in range(N_KV_HEADS):
        s = s_next
        if kv + 1 < N_KV_HEADS:
            s_next = scores(kv + 1)
        if between is not None:
            between()
        s = jnp.where(mask4, s, -jnp.inf)
        sink = jnp.concatenate([jnp.full((1, BLK), sinks_ref[kv * Q_PER_KV + g], F32) for g in range(Q_PER_KV)],
                               axis=1)
        m = jnp.maximum(jnp.max(s, axis=0, keepdims=True), sink)
        e = jnp.exp(s - m)
        denom = jnp.sum(e, axis=0, keepdims=True) + jnp.exp(sink - m)
        p = (e * (1.0 / denom)).astype(BF16)
        o = _dot(vtwin[kv * HEAD_DIM:(kv + 1) * HEAD_DIM, :], p)
        emit(kv, jnp.concatenate([o[:, g * BLK:(g + 1) * BLK] for g in range(Q_PER_KV)], axis=0))
        yield


def _split3(x):
    hi = x.astype(BF16)
    r1 = x - hi.astype(F32)
    mid = r1.astype(BF16)
    lo = (r1 - mid.astype(F32)).astype(BF16)
    return hi, mid, lo


def _ssd_steps(xc, dt, a_row, dskip_ref, ht_ref, y_ref, r0, between=None):
    left = _left_half((BLK, LANES))
    row_i = lax.broadcasted_iota(jnp.int32, (BLK, BLK), 0)
    col_i = lax.broadcasted_iota(jnp.int32, (BLK, BLK), 1)
    tri = col_i <= row_i
    tri_b = tri.astype(BF16)
    da = dt * a_row
    hi, mid, lo = _split3(da)
    parts = _dot(tri_b, jnp.concatenate([hi, mid, lo], axis=1))
    cs = (parts[:, 0:LANES] + parts[:, LANES:2 * LANES]) + parts[:, 2 * LANES:]
    cs_t = cs.T
    if between is not None:
        between()

    def colb(a, h):
        return jnp.broadcast_to(a[:, h:h + 1], (BLK, LANES))

    for g in range(SSM_GROUPS):
        bg = xc((D_SSM + g * SSM_STATE) // LANES)
        cg = xc((D_SSM + D_BC + g * SSM_STATE) // LANES).astype(BF16)
        cb = _dot_nt(cg, bg.astype(BF16))
        gs = slice(g * 2 * LANES, (g + 1) * 2 * LANES)
        ht_g = ht_ref[:, gs]
        yoff = _dot(cg, ht_g.astype(BF16))
        if between is not None:
            between()
        xdec, cdec = [], []
        for hp in range(2):
            h0 = g * 4 + 2 * hp
            h1 = h0 + 1
            cols = slice((g * 2 + hp) * LANES, (g * 2 + hp + 1) * LANES)
            xs2 = xc(g * 2 + hp)
            csb = [colb(cs, h0), colb(cs, h1)]
            cs2 = jnp.where(left, csb[0], csb[1])
            ecs2 = jnp.exp(cs2)
            dec2 = jnp.exp(cs2[BLK - 1:BLK, :] - cs2)
            dt2 = jnp.where(left, colb(dt, h0), colb(dt, h1))
            xd2 = xs2 * dt2
            ms = []
            for idx, h in enumerate((h0, h1)):
                seg = csb[idx] - jnp.broadcast_to(cs_t[h:h + 1, :], (BLK, BLK))
                ms.append((cb * jnp.exp(jnp.where(tri, seg, -jnp.inf))).astype(BF16))
            xd2b = xd2.astype(BF16)
            zero = jnp.zeros_like(xd2b)
            xd_diag = jnp.concatenate([jnp.where(left, xd2b, zero), jnp.where(left, zero, xd2b)], axis=0)
            yd = _dot(jnp.concatenate(ms, axis=1), xd_diag)
            dsk2 = jnp.where(left, dskip_ref[h0], dskip_ref[h1])
            y2 = (yd + yoff[:, hp * LANES:(hp + 1) * LANES] * ecs2) + xs2 * dsk2
            y_ref[pl.ds(r0, BLK), cols] = y2
            xdec.append((xd2 * dec2).astype(BF16))
            cdec.append(ecs2[BLK - 1:BLK, :])
        xdec = jnp.concatenate(xdec, axis=1)
        cdec = jnp.concatenate(cdec, axis=1)
        ht_ref[:, gs] = cdec * ht_g + _dot(bg.T.astype(BF16), xdec)
        yield


def _gated_norm(y, z, w):
    g = y * _silu(z)
    parts = []
    width = D_SSM // SSM_GROUPS
    for i in range(SSM_GROUPS):
        gg = g[:, i * width:(i + 1) * width]
        parts.append(gg * lax.rsqrt(jnp.mean(gg * gg, axis=-1, keepdims=True) + EPS))
    return jnp.concatenate(parts, axis=1) * w


def _conv(xbc_ref, rows, conv_w_ref, conv_b_ref):
    acc = conv_b_ref[...]
    for i in range(CONV_WIDTH):
        off = HIST - (CONV_WIDTH - 1) + i
        acc = acc + xbc_ref[off:off + rows, :] * conv_w_ref[i:i + 1, :]
    return _silu(acc)


CONV_PHASES = 4


def _conv_slab(xbc_ref, xc_ref, c, rows, conv_w_ref, conv_b_ref):
    n = rows // CONV_PHASES
    cols = slice(c * LANES, (c + 1) * LANES)
    taps = [conv_w_ref[i:i + 1, cols] for i in range(CONV_WIDTH)]
    bias = conv_b_ref[:, cols]
    lo = -(CONV_WIDTH - 1)
    shifted = [xbc_ref[c, pl.ds(HIST + s, n, stride=CONV_PHASES), :] for s in range(lo, CONV_PHASES)]
    for r in range(CONV_PHASES):
        acc = bias
        for i in range(CONV_WIDTH):
            acc = acc + shifted[r + i] * taps[i]
        xc_ref[c, pl.ds(r, n, stride=CONV_PHASES), :] = _silu(acc)


def _nat_proj(u, w_nat_ref, lo, hi):
    return _dot(u, w_nat_ref[:, lo:hi])


def _out_proj(x, attn_t, yn, w_out_ref):
    return x + (_dot_tn(attn_t, w_out_ref[0:D_ATTN, :]) + _dot(yn, w_out_ref[D_ATTN:, :]))


def _mix_prompt_kernel(sinks_ref, dskip_ref, xp_ref, xo_ref, g1_ref, w_qvt_ref, w_nat_ref, conv_w_ref, conv_b_ref,
                       dtb_ref, alog_ref, normw_ref, w_out_ref,
                       h1_ref, kout_ref, vout_ref, convout_ref, ssmout_ref,
                       qt_s, kh_s, vth_s, z_s, xbc_s, xc_s, dt_s, y_s, attnt_s, yn_s, ht_s, *, ts, nj):
    t = pl.program_id(0)
    sp = t % 2
    sc = 1 - sp
    jp = t % nj
    jc = (t + nj - 1) % nj

    @pl.when(t == 0)
    def _():
        for ref in (qt_s, kh_s, vth_s, z_s, xbc_s, xc_s, dt_s, attnt_s, yn_s):
            ref[...] = jnp.zeros(ref.shape, ref.dtype)

    @pl.when(jc == 0)
    def _():
        ht_s[...] = jnp.zeros(ht_s.shape, F32)

    def project():
        fresh = jp == 0
        kh_s[sp, 0:WINDOW, :] = jnp.where(fresh, 0.0, kh_s[sc, ts:ts + WINDOW, :])
        vth_s[sp, :, 0:WINDOW] = jnp.where(fresh, 0.0, vth_s[sc, :, ts:ts + WINDOW])
        xbc_s[sp, :, 0:HIST, :] = jnp.where(fresh, 0.0, xbc_s[sc, :, ts:ts + HIST, :])
        u = _rms(xp_ref[0], g1_ref[...]).astype(BF16)
        attn_t = attnt_s[sp]
        yn = yn_s[sp]
        n_piece = 4
        slabs = CONV_DIM // LANES // n_piece

        def xbc(c):
            w = CONV_DIM // n_piece
            piece = _nat_proj(u, w_nat_ref, N_X + c * w, N_X + (c + 1) * w)
            for k in range(slabs):
                xbc_s[sp, c * slabs + k, HIST:HIST + ts, :] = piece[:, k * LANES:(k + 1) * LANES]

        def conv(c):
            for k in range(slabs):
                _conv_slab(xbc_s.at[sp], xc_s.at[sp], c * slabs + k, ts, conv_w_ref, conv_b_ref)

        def q(c):
            w = D_ATTN // n_piece
            qt_s[sp, c * w:(c + 1) * w, :] = (_dot_nt(w_qvt_ref[c * w:(c + 1) * w, :], u) * SCALE).astype(BF16)

        def z(c):
            w = D_SSM // n_piece
            z_s[sp, :, c * w:(c + 1) * w] = _nat_proj(u, w_nat_ref, N_Z + c * w, N_Z + (c + 1) * w)

        def out(c):
            w = D_MODEL // n_piece
            cols = slice(c * w, (c + 1) * w)
            h1_ref[0, :, cols] = xo_ref[0, :, cols] + (_dot_tn(attn_t, w_out_ref[0:D_ATTN, cols])
                                                       + _dot(yn, w_out_ref[D_ATTN:, cols]))

        def v():
            vth_s[sp, :, WINDOW:WINDOW + ts] = _dot_nt(w_qvt_ref[D_ATTN:, :], u)

        def k():
            kh_s[sp, WINDOW:WINDOW + ts, :] = _nat_proj(u, w_nat_ref, N_K, N_Z)

        def dt():
            dt_s[sp] = _softplus(_nat_proj(u, w_nat_ref, N_DT, D_NAT) + dtb_ref[...])

        placement = [
            [(xbc, 0)], [(xbc, 1), (conv, 0)], [(xbc, 2), (conv, 1)], [(xbc, 3), (conv, 2)],
            [(q, 0)], [(q, 1)], [(q, 2)], [(q, 3)], [(z, 0)],
            [(out, 0), (conv, 3)], [(out, 1)], [(out, 2)], [(out, 3)],
            [(z, 1)], [(z, 2)], [(z, 3)], [(v,)], [(k,)],
            [(dt,)],
        ]
        for group in placement:
            yield
            for fn, *args in group:
                fn(*args)

    pieces = project()
    advance = lambda: next(pieces, None)
    advance()

    a_row = -jnp.exp(alog_ref[...])
    key = lax.broadcasted_iota(jnp.int32, (2 * BLK, BLK), 0)
    qc = lax.broadcasted_iota(jnp.int32, (2 * BLK, BLK), 1) // CHUNK
    band = (key >= qc * CHUNK) & (key < (qc + 3) * CHUNK)
    for p in range(ts // BLK):
        r0 = p * BLK
        first = jnp.where(jc == 0, WINDOW, 0) if p == 0 else 0
        mask_t = band & (key >= first)

        def emit(kv, o, r0=r0):
            attnt_s[sc, kv * Q_PER_KV * HEAD_DIM:(kv + 1) * Q_PER_KV * HEAD_DIM, r0:r0 + BLK] = o.astype(BF16)

        attn = _attn_steps(qt_s[sc, :, r0:r0 + BLK], kh_s[sc, r0:r0 + 2 * BLK, :].astype(BF16),
                           vth_s[sc, :, r0:r0 + 2 * BLK].astype(BF16), mask_t, sinks_ref, emit, advance)
        ssd = _ssd_steps(lambda i, r0=r0: xc_s[sc, i, r0:r0 + BLK, :], dt_s[sc, r0:r0 + BLK, :], a_row, dskip_ref,
                         ht_s, y_s, r0, advance)
        for _ in attn:
            pass
        for _ in ssd:
            pass
    for p in range(ts // BLK):
        rows = slice(p * BLK, (p + 1) * BLK)
        advance()
        yn_s[sc, rows, :] = _gated_norm(y_s[rows, :], z_s[sc, rows, :], normw_ref[...]).astype(BF16)
    for _ in pieces:
        pass

    @pl.when(jc == nj - 1)
    def _():
        kout_ref[0] = kh_s[sc, ts:ts + WINDOW, :]
        vout_ref[0] = vth_s[sc, :, ts:ts + WINDOW].T
        for c in range(CONV_DIM // LANES):
            convout_ref[0, :, c * LANES:(c + 1) * LANES] = xbc_s[sc, c, HIST + ts - (CONV_WIDTH - 1):HIST + ts, :]
        ssmout_ref[0] = ht_s[...].T


def _mix_prompt(x, p, ts):
    b, s, _ = x.shape
    nj = s // ts
    nt = b * nj
    const2 = lambda t: (0, 0)
    prev2 = lambda t: (jnp.maximum(t - 2, 0), 0, 0)
    per_stream = lambda t: (jnp.clip(t - 1, 0, nt - 1) // nj, 0, 0)
    smem = pl.BlockSpec(memory_space=pltpu.SMEM)
    in_specs = [
        smem, smem,
        pl.BlockSpec((1, ts, D_MODEL), lambda t: (jnp.minimum(t, nt - 1), 0, 0)),
        pl.BlockSpec((1, ts, D_MODEL), prev2),
        pl.BlockSpec((1, D_MODEL), const2),
        pl.BlockSpec((D_ATTN + D_KV, D_MODEL), const2),
        pl.BlockSpec((D_MODEL, D_NAT), const2),
        pl.BlockSpec((CONV_WIDTH, CONV_DIM), const2),
        pl.BlockSpec((1, CONV_DIM), const2),
        pl.BlockSpec((1, DT_PAD), const2),
        pl.BlockSpec((1, DT_PAD), const2),
        pl.BlockSpec((1, D_SSM), const2),
        pl.BlockSpec((D_MIX, D_MODEL), const2),
    ]
    out_shape = (
        jax.ShapeDtypeStruct((nt, ts, D_MODEL), F32),
        jax.ShapeDtypeStruct((b, WINDOW, D_KV), F32),
        jax.ShapeDtypeStruct((b, WINDOW, D_KV), F32),
        jax.ShapeDtypeStruct((b, CONV_WIDTH - 1, CONV_DIM), F32),
        jax.ShapeDtypeStruct((b, D_SSM, SSM_STATE), F32),
    )
    out_specs = (
        pl.BlockSpec((1, ts, D_MODEL), prev2),
        pl.BlockSpec((1, WINDOW, D_KV), per_stream),
        pl.BlockSpec((1, WINDOW, D_KV), per_stream),
        pl.BlockSpec((1, CONV_WIDTH - 1, CONV_DIM), per_stream),
        pl.BlockSpec((1, D_SSM, SSM_STATE), per_stream),
    )
    scratch = [
        pltpu.VMEM((2, D_ATTN, ts), BF16),
        pltpu.VMEM((2, WINDOW + ts, D_KV), F32),
        pltpu.VMEM((2, D_KV, WINDOW + ts), F32),
        pltpu.VMEM((2, ts, D_SSM), F32),
        pltpu.VMEM((2, CONV_DIM // LANES, HIST + ts, LANES), F32),
        pltpu.VMEM((2, CONV_DIM // LANES, ts, LANES), F32),
        pltpu.VMEM((2, ts, DT_PAD), F32),
        pltpu.VMEM((ts, D_SSM), F32),
        pltpu.VMEM((2, D_ATTN, ts), BF16),
        pltpu.VMEM((2, ts, D_SSM), BF16),
        pltpu.VMEM((SSM_STATE, D_SSM), F32),
    ]
    x3 = x.reshape(nt, ts, D_MODEL)
    outs = pl.pallas_call(
        functools.partial(_mix_prompt_kernel, ts=ts, nj=nj),
        grid=(nt + 2,),
        in_specs=in_specs,
        out_specs=out_specs,
        out_shape=out_shape,
        scratch_shapes=scratch,
        compiler_params=pltpu.CompilerParams(dimension_semantics=("arbitrary",), vmem_limit_bytes=VMEM_LIMIT),
        name="mix_prompt",
    )(p["sinks"], p["d_skip"], x3, x3, p["ln1_g"], p["w_qvt"], p["w_nat"], p["conv_w"], p["conv_b"], p["dt_bias"],
      p["a_log"], p["ssm_norm_w"], p["w_out"])
    return (outs[0].reshape(b, s, D_MODEL),) + tuple(outs[1:])


def _mix_sample_kernel(sinks_ref, dskip_ref, x_ref, ck_ref, cv_ref, sconv_ref, sssm_ref, g1_ref, w_qvt_ref,
                       w_nat_ref, conv_w_ref, conv_b_ref, dtb_ref, alog_ref, normw_ref, w_out_ref,
                       h1_ref, kout_ref, vout_ref, convout_ref, ssmout_ref,
                       qt_s, vt_s, z_s, xbcall_s, dt_s, xbc_s, y_s, attnt_s, yn_s, ht_s, *, nb, t):
    i = pl.program_id(0)
    n = nb * t
    per_tile = BLK // t

    @pl.when(i == 0)
    def _():
        u = _rms(x_ref[...], g1_ref[...]).astype(BF16)
        qt = (_dot_nt(w_qvt_ref[0:D_ATTN, :], u) * SCALE).astype(BF16)
        vt = _dot_nt(w_qvt_ref[D_ATTN:, :], u)
        for c in range(n // BLK):
            qt_s[c] = qt[:, c * BLK:(c + 1) * BLK]
            vt_s[c] = vt[:, c * BLK:(c + 1) * BLK]
        vout_ref[...] = vt.T
        kout_ref[...] = _nat_proj(u, w_nat_ref, N_K, N_Z)
        z_s[...] = _nat_proj(u, w_nat_ref, N_Z, N_X)
        xbcall_s[...] = _nat_proj(u, w_nat_ref, N_X, N_DT)
        dt_s[...] = _softplus(_nat_proj(u, w_nat_ref, N_DT, D_NAT) + dtb_ref[...])
        attnt_s[...] = jnp.zeros(attnt_s.shape, BF16)
        xbc_s[...] = jnp.zeros(xbc_s.shape, F32)

    @pl.when((i >= 1) & (i <= nb))
    def _():
        b = i - 1
        rows = pl.ds(pl.multiple_of(b * t, t), t)
        c = b // per_tile
        lo = (b % per_tile) * t
        kwin = jnp.concatenate([kout_ref[pl.ds(pl.multiple_of(c * BLK, BLK), BLK), :], ck_ref[0]], axis=0)
        vtwin = jnp.concatenate([vt_s[c], cv_ref[0].T], axis=1)
        key = lax.broadcasted_iota(jnp.int32, (2 * BLK, BLK), 0)
        mask_t = ((key >= lo) & (key < lo + t)) | (key >= BLK)
        xbc_s[HIST - (CONV_WIDTH - 1):HIST, :] = sconv_ref[0]
        xbc_s[HIST:HIST + t, :] = xbcall_s[rows, :]
        convout_ref[0] = xbc_s[HIST + t - (CONV_WIDTH - 1):HIST + t, :]
        live = lax.broadcasted_iota(jnp.int32, (BLK, 1), 0) < t
        xc = jnp.where(live, _conv(xbc_s, BLK, conv_w_ref, conv_b_ref), 0.0)
        dt = jnp.concatenate([dt_s[rows, :], jnp.zeros((BLK - t, DT_PAD), F32)], axis=0)
        ht_s[...] = sssm_ref[0].T
        ssd = _ssd_steps(lambda i: xc[:, i * LANES:(i + 1) * LANES], dt, -jnp.exp(alog_ref[...]), dskip_ref, ht_s,
                         y_s, 0)

        outs = [None] * N_KV_HEADS

        def emit(kv, o):
            outs[kv] = o

        for _ in _attn_steps(qt_s[c], kwin.astype(BF16), vtwin.astype(BF16), mask_t, sinks_ref, emit,
                             lambda: next(ssd, None)):
            pass
        for _ in ssd:
            pass
        o_t = jnp.concatenate(outs, axis=0)
        lane = lax.broadcasted_iota(jnp.int32, (D_ATTN, BLK), 1)
        attnt_s[c] = jnp.where((lane >= lo) & (lane < lo + t), o_t.astype(BF16), attnt_s[c])
        ssmout_ref[0] = ht_s[...].T
        yn_s[rows, :] = _gated_norm(y_s[0:t, :], z_s[rows, :], normw_ref[...]).astype(BF16)

    @pl.when(i == nb + 1)
    def _():
        for c in range(n // BLK):
            r = slice(c * BLK, (c + 1) * BLK)
            h1_ref[r, :] = _out_proj(x_ref[r, :], attnt_s[c], yn_s[r, :], w_out_ref)


def _mix_sample(x, cache_k, cache_v, state_conv, state_ssm, p):
    nb, t, _ = x.shape
    n = nb * t
    assert BLK % t == 0 and n % BLK == 0
    const2 = lambda i: (0, 0)
    per = lambda i: (jnp.clip(i - 1, 0, nb - 1), 0, 0)
    smem = pl.BlockSpec(memory_space=pltpu.SMEM)
    in_specs = [
        smem, smem,
        pl.BlockSpec((n, D_MODEL), const2),
        pl.BlockSpec((1, WINDOW, D_KV), per),
        pl.BlockSpec((1, WINDOW, D_KV), per),
        pl.BlockSpec((1, CONV_WIDTH - 1, CONV_DIM), per),
        pl.BlockSpec((1, D_SSM, SSM_STATE), per),
        pl.BlockSpec((1, D_MODEL), const2),
        pl.BlockSpec((D_ATTN + D_KV, D_MODEL), const2),
        pl.BlockSpec((D_MODEL, D_NAT), const2),
        pl.BlockSpec((CONV_WIDTH, CONV_DIM), const2),
        pl.BlockSpec((1, CONV_DIM), const2),
        pl.BlockSpec((1, DT_PAD), const2),
        pl.BlockSpec((1, DT_PAD), const2),
        pl.BlockSpec((1, D_SSM), const2),
        pl.BlockSpec((D_MIX, D_MODEL), const2),
    ]
    out_shape = (
        jax.ShapeDtypeStruct((n, D_MODEL), F32),
        jax.ShapeDtypeStruct((n, D_KV), F32),
        jax.ShapeDtypeStruct((n, D_KV), F32),
        jax.ShapeDtypeStruct((nb, CONV_WIDTH - 1, CONV_DIM), F32),
        jax.ShapeDtypeStruct((nb, D_SSM, SSM_STATE), F32),
    )
    out_specs = (
        pl.BlockSpec((n, D_MODEL), const2),
        pl.BlockSpec((n, D_KV), const2),
        pl.BlockSpec((n, D_KV), const2),
        pl.BlockSpec((1, CONV_WIDTH - 1, CONV_DIM), per),
        pl.BlockSpec((1, D_SSM, SSM_STATE), per),
    )
    scratch = [
        pltpu.VMEM((n // BLK, D_ATTN, BLK), BF16),
        pltpu.VMEM((n // BLK, D_KV, BLK), F32),
        pltpu.VMEM((n, D_SSM), F32),
        pltpu.VMEM((n, CONV_DIM), F32),
        pltpu.VMEM((n, DT_PAD), F32),
        pltpu.VMEM((HIST + BLK, CONV_DIM), F32),
        pltpu.VMEM((BLK, D_SSM), F32),
        pltpu.VMEM((n // BLK, D_ATTN, BLK), BF16),
        pltpu.VMEM((n, D_SSM), BF16),
        pltpu.VMEM((SSM_STATE, D_SSM), F32),
    ]
    return pl.pallas_call(
        functools.partial(_mix_sample_kernel, nb=nb, t=t),
        grid=(nb + 2,),
        in_specs=in_specs,
        out_specs=out_specs,
        out_shape=out_shape,
        scratch_shapes=scratch,
        compiler_params=pltpu.CompilerParams(dimension_semantics=("arbitrary",), vmem_limit_bytes=VMEM_LIMIT),
        name="mix_sample",
    )(p["sinks"], p["d_skip"], x.reshape(n, D_MODEL), cache_k, cache_v, state_conv, state_ssm, p["ln1_g"],
      p["w_qvt"], p["w_nat"], p["conv_w"], p["conv_b"], p["dt_bias"], p["a_log"], p["ssm_norm_w"], p["w_out"])


FFN_ROWS = 256


def _ffn_kernel(h_ref, g2_ref, wg_ref, wu_ref, wd_ref, gf_ref, o_ref):
    tm = h_ref.shape[0]
    chunks = [slice(r, min(r + FFN_ROWS, tm)) for r in range(0, tm, FFN_ROWS)]

    def up(rows):
        u = _rms(h_ref[rows, :], g2_ref[...]).astype(BF16)
        return _dot(u, wg_ref[...]), _dot(u, wu_ref[...])

    def down(rows, gate, upv):
        mid = (_silu(gate) * upv).astype(BF16)
        o_ref[rows, :] = _rms(h_ref[rows, :] + _dot(mid, wd_ref[...]), gf_ref[...])

    pending = up(chunks[0])
    for i, rows in enumerate(chunks):
        nxt = up(chunks[i + 1]) if i + 1 < len(chunks) else None
        down(rows, *pending)
        pending = nxt


def _ffn(h, p, tm):
    n = h.shape[0]
    d_ff = p["w_gate"].shape[1]
    const2 = lambda i: (0, 0)
    return pl.pallas_call(
        _ffn_kernel,
        grid=(n // tm,),
        in_specs=[
            pl.BlockSpec((tm, D_MODEL), lambda i: (i, 0)),
            pl.BlockSpec((1, D_MODEL), const2),
            pl.BlockSpec((D_MODEL, d_ff), const2),
            pl.BlockSpec((D_MODEL, d_ff), const2),
            pl.BlockSpec((d_ff, D_MODEL), const2),
            pl.BlockSpec((1, D_MODEL), const2),
        ],
        out_specs=pl.BlockSpec((tm, D_MODEL), lambda i: (i, 0)),
        out_shape=jax.ShapeDtypeStruct((n, D_MODEL), F32),
        compiler_params=pltpu.CompilerParams(dimension_semantics=("arbitrary",), vmem_limit_bytes=VMEM_LIMIT),
        name="ffn",
    )(h, p["ln2_g"], p["w_gate"], p["w_up"], p["w_down"], p["final_g"])


def _prep_params(ln1_g, w_in, conv_w, conv_b, dt_bias, a_log, d_skip, ssm_norm_w, sinks, w_out, ln2_g, w_gate,
                 w_up, w_down, final_g):
    pad_h = lambda v: jnp.pad(v.reshape(1, SSM_HEADS), ((0, 0), (0, DT_PAD - SSM_HEADS)))
    o_k, o_v, o_z = D_ATTN, D_ATTN + D_KV, D_ATTN + 2 * D_KV
    w_q, w_k, w_v, w_rest = w_in[:, :o_k], w_in[:, o_k:o_v], w_in[:, o_v:o_z], w_in[:, o_z:]
    w_nat = jnp.concatenate([w_k, w_rest], axis=1).astype(BF16)
    return {
        "ln1_g": ln1_g.reshape(1, D_MODEL),
        "w_qvt": jnp.concatenate([w_q, w_v], axis=1).T.astype(BF16),
        "w_nat": jnp.pad(w_nat, ((0, 0), (0, D_NAT - w_nat.shape[1]))),
        "conv_w": conv_w,
        "conv_b": conv_b.reshape(1, CONV_DIM),
        "dt_bias": pad_h(dt_bias),
        "a_log": pad_h(a_log),
        "d_skip": d_skip,
        "ssm_norm_w": ssm_norm_w.reshape(1, D_SSM),
        "sinks": sinks,
        "w_out": w_out.astype(BF16),
        "ln2_g": ln2_g.reshape(1, D_MODEL),
        "w_gate": w_gate.astype(BF16),
        "w_up": w_up.astype(BF16),
        "w_down": w_down.astype(BF16),
        "final_g": final_g.reshape(1, D_MODEL),
    }


def _layer(x_prompt, x_sample, cache_k, cache_v, state_conv, state_ssm, p, ts, tm):
    b, s, _ = x_prompt.shape
    nb, t, _ = x_sample.shape
    h1p, kp, vp, cp, sp = _mix_prompt(x_prompt, p, ts)
    h1s, kn, vn, cn, sn = _mix_sample(x_sample, cache_k.reshape(nb, WINDOW, D_KV), cache_v.reshape(nb, WINDOW, D_KV),
                                      state_conv, state_ssm.reshape(nb, D_SSM, SSM_STATE), p)
    yp = _ffn(h1p.reshape(b * s, D_MODEL), p, tm).reshape(b, s, D_MODEL)
    ys = _ffn(h1s, p, min(tm, nb * t)).reshape(nb, t, D_MODEL)
    return (yp, ys,
            kp.reshape(b, WINDOW, N_KV_HEADS, HEAD_DIM), vp.reshape(b, WINDOW, N_KV_HEADS, HEAD_DIM), cp,
            sp.reshape(b, SSM_HEADS, SSM_HEAD_DIM, SSM_STATE),
            kn.reshape(nb, t, N_KV_HEADS, HEAD_DIM), vn.reshape(nb, t, N_KV_HEADS, HEAD_DIM), cn,
            sn.reshape(nb, SSM_HEADS, SSM_HEAD_DIM, SSM_STATE))


def kernel(x_prompt, x_sample, cache_k, cache_v, state_conv, state_ssm, ln1_g, w_in, conv_w, conv_b, dt_bias, a_log,
           d_skip, ssm_norm_w, sinks, w_out, ln2_g, w_gate, w_up, w_down, final_g):
    assert w_in.shape[0] == 1, "one layer"
    p = _prep_params(ln1_g[0], w_in[0], conv_w[0], conv_b[0], dt_bias[0], a_log[0], d_skip[0], ssm_norm_w[0],
                     sinks[0], w_out[0], ln2_g[0], w_gate[0], w_up[0], w_down[0], final_g)
    outs = _layer(x_prompt, x_sample, cache_k[0], cache_v[0], state_conv[0], state_ssm[0], p, ts=256, tm=1024)
    return tuple(o[None] if i >= 2 else o for i, o in enumerate(outs))
```

```python
import functools

import jax
import jax.numpy as jnp
from jax import lax
from jax.experimental import pallas as pl
from jax.experimental.pallas import tpu as pltpu

F32 = jnp.float32
BF16 = jnp.bfloat16

D_MODEL = 1024
CHUNK = 64
WINDOW = 128
HEAD_DIM = 64
N_HEADS = 16
N_KV_HEADS = 4
Q_PER_KV = N_HEADS // N_KV_HEADS
D_ATTN = N_HEADS * HEAD_DIM
D_KV = N_KV_HEADS * HEAD_DIM
SCALE = HEAD_DIM ** -0.5
SSM_HEADS = 16
SSM_HEAD_DIM = 64
SSM_GROUPS = 4
SSM_STATE = 128
D_SSM = SSM_HEADS * SSM_HEAD_DIM
D_BC = SSM_GROUPS * SSM_STATE
CONV_WIDTH = 4
CONV_DIM = D_SSM + 2 * D_BC
D_MIX = D_ATTN + D_SSM
EPS = 1e-6

LANES = 128
BLK = 2 * CHUNK
DT_PAD = LANES
N_K, N_DT, N_Z, N_X = 0, D_KV, D_KV + DT_PAD, D_KV + DT_PAD + D_SSM
D_NAT = N_X + CONV_DIM
HIST = 8
VMEM_LIMIT = 56 * 1024 * 1024


def _rms(x, g):
    ms = jnp.mean(x * x, axis=-1, keepdims=True)
    return (x * lax.rsqrt(ms + EPS)) * g


def _silu(x):
    h = 0.5 * x
    return h + h * jnp.tanh(h)


def _softplus(x):
    return jnp.maximum(x, 0.0) + jnp.log1p(jnp.exp(-jnp.abs(x)))


def _dot(a, b):
    return jnp.dot(a, b, preferred_element_type=F32)


def _dot_nt(a, b):
    return lax.dot_general(a, b, (((1,), (1,)), ((), ())), preferred_element_type=F32)


def _dot_tn(a, b):
    return lax.dot_general(a, b, (((0,), (0,)), ((), ())), preferred_element_type=F32)


def _left_half(shape):
    return lax.broadcasted_iota(jnp.int32, shape, len(shape) - 1) < HEAD_DIM


def _attn_steps(qt_blk, kwin, vtwin, mask_t, sinks_ref, emit, between=None):
    nq = Q_PER_KV * BLK
    mask4 = jnp.concatenate([mask_t] * Q_PER_KV, axis=1)
    zeros = jnp.zeros((HEAD_DIM, nq), BF16)

    def scores(kv):
        slab, half = kv // 2, kv % 2
        qk = jnp.concatenate([qt_blk[(kv * Q_PER_KV + g) * HEAD_DIM:(kv * Q_PER_KV + g + 1) * HEAD_DIM, :]
                              for g in range(Q_PER_KV)], axis=1)
        rhs = jnp.concatenate([qk, zeros] if half == 0 else [zeros, qk], axis=0)
        return _dot(kwin[:, slab * LANES:(slab + 1) * LANES], rhs)

    s_next = scores(0)
    for kv in range(N_KV_HEADS):
        s = s_next
        if kv + 1 < N_KV_HEADS:
            s_next = scores(kv + 1)
        if between is not None:
            between()
        s = jnp.where(mask4, s, -jnp.inf)
        sink = jnp.concatenate([jnp.full((1, BLK), sinks_ref[kv * Q_PER_KV + g], F32) for g in range(Q_PER_KV)],
                               axis=1)
        m = jnp.maximum(jnp.max(s, axis=0, keepdims=True), sink)
        e = jnp.exp(s - m)
        denom = jnp.sum(e, axis=0, keepdims=True) + jnp.exp(sink - m)
        p = (e * (1.0 / denom)).astype(BF16)
        o = _dot(vtwin[kv * HEAD_DIM:(kv + 1) * HEAD_DIM, :], p)
        emit(kv, jnp.concatenate([o[:, g * BLK:(g + 1) * BLK] for g in range(Q_PER_KV)], axis=0))
        yield


def _split3(x):
    hi = x.astype(BF16)
    r1 = x - hi.astype(F32)
    mid = r1.astype(BF16)
    lo = (r1 - mid.astype(F32)).astype(BF16)
    return hi, mid, lo


def _ssd_steps(xc, dt, a_row, dskip_ref, ht_ref, y_ref, r0, between=None):
    left = _left_half((BLK, LANES))
    row_i = lax.broadcasted_iota(jnp.int32, (BLK, BLK), 0)
    col_i = lax.broadcasted_iota(jnp.int32, (BLK, BLK), 1)
    tri = col_i <= row_i
    tri_b = tri.astype(BF16)
    da = dt * a_row
    hi, mid, lo = _split3(da)
    parts = _dot(tri_b, jnp.concatenate([hi, mid, lo], axis=1))
    cs = (parts[:, 0:LANES] + parts[:, LANES:2 * LANES]) + parts[:, 2 * LANES:]
    cs_t = cs.T
    if between is not None:
        between()

    def colb(a, h):
        return jnp.broadcast_to(a[:, h:h + 1], (BLK, LANES))

    def group_products(g):
        bg = xc((D_SSM + g * SSM_STATE) // LANES)
        cg = xc((D_SSM + D_BC + g * SSM_STATE) // LANES).astype(BF16)
        cb = _dot_nt(cg, bg.astype(BF16))
        ht_g = ht_ref[:, g * 2 * LANES:(g + 1) * 2 * LANES]
        yoff = _dot(cg, ht_g.astype(BF16))
        return bg, cb, ht_g, yoff

    nxt = group_products(0)
    for g in range(SSM_GROUPS):
        bg, cb, ht_g, yoff = nxt
        gs = slice(g * 2 * LANES, (g + 1) * 2 * LANES)
        if g + 1 < SSM_GROUPS:
            nxt = group_products(g + 1)
        if between is not None:
            between()
        xdec, cdec, late = [], [], []
        for hp in range(2):
            h0 = g * 4 + 2 * hp
            h1 = h0 + 1
            cols = slice((g * 2 + hp) * LANES, (g * 2 + hp + 1) * LANES)
            xs2 = xc(g * 2 + hp)
            csb = [colb(cs, h0), colb(cs, h1)]
            cs2 = jnp.where(left, csb[0], csb[1])
            ecs2 = jnp.exp(cs2)
            dec2 = jnp.exp(cs2[BLK - 1:BLK, :] - cs2)
            dt2 = jnp.where(left, colb(dt, h0), colb(dt, h1))
            xd2 = xs2 * dt2
            ms = []
            for idx, h in enumerate((h0, h1)):
                seg = csb[idx] - jnp.broadcast_to(cs_t[h:h + 1, :], (BLK, BLK))
                ms.append((cb * jnp.exp(jnp.where(tri, seg, -jnp.inf))).astype(BF16))
            xd2b = xd2.astype(BF16)
            zero = jnp.zeros_like(xd2b)
            xd_diag = jnp.concatenate([jnp.where(left, xd2b, zero), jnp.where(left, zero, xd2b)], axis=0)
            yd = _dot(jnp.concatenate(ms, axis=1), xd_diag)
            dsk2 = jnp.where(left, dskip_ref[h0], dskip_ref[h1])
            late.append((cols, yd, yoff[:, hp * LANES:(hp + 1) * LANES] * ecs2, xs2 * dsk2))
            xdec.append((xd2 * dec2).astype(BF16))
            cdec.append(ecs2[BLK - 1:BLK, :])
        xdec = jnp.concatenate(xdec, axis=1)
        cdec = jnp.concatenate(cdec, axis=1)
        ht_ref[:, gs] = cdec * ht_g + _dot(bg.T.astype(BF16), xdec)
        for cols, yd, off, skip in late:
            y_ref[pl.ds(r0, BLK), cols] = (yd + off) + skip
        yield


def _gated_norm(y, z, w):
    g = y * _silu(z)
    parts = []
    width = D_SSM // SSM_GROUPS
    for i in range(SSM_GROUPS):
        gg = g[:, i * width:(i + 1) * width]
        parts.append(gg * lax.rsqrt(jnp.mean(gg * gg, axis=-1, keepdims=True) + EPS))
    return jnp.concatenate(parts, axis=1) * w


def _conv(xbc_ref, rows, conv_w_ref, conv_b_ref):
    acc = conv_b_ref[...]
    for i in range(CONV_WIDTH):
        off = HIST - (CONV_WIDTH - 1) + i
        acc = acc + xbc_ref[off:off + rows, :] * conv_w_ref[i:i + 1, :]
    return _silu(acc)


CONV_PHASES = 4


def _conv_slab(xbc_ref, xc_ref, c, rows, conv_w_ref, conv_b_ref):
    n = rows // CONV_PHASES
    cols = slice(c * LANES, (c + 1) * LANES)
    taps = [conv_w_ref[i:i + 1, cols] for i in range(CONV_WIDTH)]
    bias = conv_b_ref[:, cols]
    lo = -(CONV_WIDTH - 1)
    shifted = [xbc_ref[c, pl.ds(HIST + s, n, stride=CONV_PHASES), :] for s in range(lo, CONV_PHASES)]
    for r in range(CONV_PHASES):
        acc = bias
        for i in range(CONV_WIDTH):
            acc = acc + shifted[r + i] * taps[i]
        xc_ref[c, pl.ds(r, n, stride=CONV_PHASES), :] = _silu(acc)


def _nat_proj(u, w_nat_ref, lo, hi):
    return _dot(u, w_nat_ref[:, lo:hi])


def _out_proj(x, attn_t, yn, w_out_ref):
    return x + (_dot_tn(attn_t, w_out_ref[0:D_ATTN, :]) + _dot(yn, w_out_ref[D_ATTN:, :]))


def _mix_prompt_kernel(sinks_ref, dskip_ref, xp_ref, xo_ref, g1_ref, w_qvt_ref, w_nat_ref, conv_w_ref, conv_b_ref,
                       dtb_ref, alog_ref, normw_ref, w_out_ref,
                       h1_ref, kout_ref, vout_ref, convout_ref, ssmout_ref,
                       qt_s, kh_s, vth_s, z_s, xbc_s, xc_s, dt_s, y_s, attnt_s, yn_s, ht_s, *, ts, nj):
    t = pl.program_id(0)
    sp = t % 2
    sc = 1 - sp
    jp = t % nj
    jc = (t + nj - 1) % nj

    @pl.when(t == 0)
    def _():
        for ref in (qt_s, kh_s, vth_s, z_s, xbc_s, xc_s, dt_s, attnt_s, yn_s):
            ref[...] = jnp.zeros(ref.shape, ref.dtype)

    @pl.when(jc == 0)
    def _():
        ht_s[...] = jnp.zeros(ht_s.shape, F32)

    def project():
        fresh = jp == 0
        kh_s[sp, 0:WINDOW, :] = jnp.where(fresh, 0.0, kh_s[sc, ts:ts + WINDOW, :])
        vth_s[sp, :, 0:WINDOW] = jnp.where(fresh, 0.0, vth_s[sc, :, ts:ts + WINDOW])
        xbc_s[sp, :, 0:HIST, :] = jnp.where(fresh, 0.0, xbc_s[sc, :, ts:ts + HIST, :])
        u = _rms(xp_ref[0], g1_ref[...]).astype(BF16)
        attn_t = attnt_s[sp]
        yn = yn_s[sp]
        n_piece = 4
        slabs = CONV_DIM // LANES // n_piece

        def xbc(c):
            w = CONV_DIM // n_piece
            piece = _nat_proj(u, w_nat_ref, N_X + c * w, N_X + (c + 1) * w)
            for k in range(slabs):
                xbc_s[sp, c * slabs + k, HIST:HIST + ts, :] = piece[:, k * LANES:(k + 1) * LANES]

        def conv(c):
            for k in range(slabs):
                _conv_slab(xbc_s.at[sp], xc_s.at[sp], c * slabs + k, ts, conv_w_ref, conv_b_ref)

        def q(c):
            w = D_ATTN // n_piece
            qt_s[sp, c * w:(c + 1) * w, :] = (_dot_nt(w_qvt_ref[c * w:(c + 1) * w, :], u) * SCALE).astype(BF16)

        def z(c):
            w = D_SSM // n_piece
            z_s[sp, :, c * w:(c + 1) * w] = _nat_proj(u, w_nat_ref, N_Z + c * w, N_Z + (c + 1) * w)

        def out(c):
            w = D_MODEL // n_piece
            cols = slice(c * w, (c + 1) * w)
            h1_ref[0, :, cols] = xo_ref[0, :, cols] + (_dot_tn(attn_t, w_out_ref[0:D_ATTN, cols])
                                                       + _dot(yn, w_out_ref[D_ATTN:, cols]))

        def v():
            vth_s[sp, :, WINDOW:WINDOW + ts] = _dot_nt(w_qvt_ref[D_ATTN:, :], u)

        def k():
            kh_s[sp, WINDOW:WINDOW + ts, :] = _nat_proj(u, w_nat_ref, N_K, N_DT)

        def dt():
            dt_s[sp] = _softplus(_nat_proj(u, w_nat_ref, N_DT, N_Z) + dtb_ref[...])

        placement = [
            [(xbc, 0)], [(xbc, 1), (conv, 0)], [(xbc, 2), (conv, 1)], [(xbc, 3), (conv, 2)],
            [(q, 0)], [(q, 1)], [(q, 2)], [(q, 3)], [(z, 0)],
            [(out, 0), (conv, 3)], [(out, 1)], [(out, 2)], [(out, 3)],
            [(z, 1)], [(z, 2)], [(z, 3)], [(v,)], [(k,)],
            [(dt,)],
        ]
        for group in placement:
            yield
            for fn, *args in group:
                fn(*args)

    pieces = project()
    advance = lambda: next(pieces, None)
    advance()

    a_row = -jnp.exp(alog_ref[...])
    key = lax.broadcasted_iota(jnp.int32, (2 * BLK, BLK), 0)
    qc = lax.broadcasted_iota(jnp.int32, (2 * BLK, BLK), 1) // CHUNK
    band = (key >= qc * CHUNK) & (key < (qc + 3) * CHUNK)
    for p in range(ts // BLK):
        r0 = p * BLK
        first = jnp.where(jc == 0, WINDOW, 0) if p == 0 else 0
        mask_t = band & (key >= first)

        def emit(kv, o, r0=r0):
            attnt_s[sc, kv * Q_PER_KV * HEAD_DIM:(kv + 1) * Q_PER_KV * HEAD_DIM, r0:r0 + BLK] = o.astype(BF16)

        attn = _attn_steps(qt_s[sc, :, r0:r0 + BLK], kh_s[sc, r0:r0 + 2 * BLK, :].astype(BF16),
                           vth_s[sc, :, r0:r0 + 2 * BLK].astype(BF16), mask_t, sinks_ref, emit, advance)
        ssd = _ssd_steps(lambda i, r0=r0: xc_s[sc, i, r0:r0 + BLK, :], dt_s[sc, r0:r0 + BLK, :], a_row, dskip_ref,
                         ht_s, y_s, r0, advance)
        for _ in attn:
            pass
        for _ in ssd:
            pass
    for p in range(ts // BLK):
        rows = slice(p * BLK, (p + 1) * BLK)
        advance()
        yn_s[sc, rows, :] = _gated_norm(y_s[rows, :], z_s[sc, rows, :], normw_ref[...]).astype(BF16)
    for _ in pieces:
        pass

    @pl.when(jc == nj - 1)
    def _():
        kout_ref[0] = kh_s[sc, ts:ts + WINDOW, :]
        vout_ref[0] = vth_s[sc, :, ts:ts + WINDOW].T
        for c in range(CONV_DIM // LANES):
            convout_ref[0, :, c * LANES:(c + 1) * LANES] = xbc_s[sc, c, HIST + ts - (CONV_WIDTH - 1):HIST + ts, :]
        ssmout_ref[0] = ht_s[...].T


def _mix_prompt(x, p, ts):
    b, s, _ = x.shape
    nj = s // ts
    nt = b * nj
    const2 = lambda t: (0, 0)
    prev2 = lambda t: (jnp.maximum(t - 2, 0), 0, 0)
    per_stream = lambda t: (jnp.clip(t - 1, 0, nt - 1) // nj, 0, 0)
    smem = pl.BlockSpec(memory_space=pltpu.SMEM)
    in_specs = [
        smem, smem,
        pl.BlockSpec((1, ts, D_MODEL), lambda t: (jnp.minimum(t, nt - 1), 0, 0)),
        pl.BlockSpec((1, ts, D_MODEL), prev2),
        pl.BlockSpec((1, D_MODEL), const2),
        pl.BlockSpec((D_ATTN + D_KV, D_MODEL), const2),
        pl.BlockSpec((D_MODEL, D_NAT), const2),
        pl.BlockSpec((CONV_WIDTH, CONV_DIM), const2),
        pl.BlockSpec((1, CONV_DIM), const2),
        pl.BlockSpec((1, DT_PAD), const2),
        pl.BlockSpec((1, DT_PAD), const2),
        pl.BlockSpec((1, D_SSM), const2),
        pl.BlockSpec((D_MIX, D_MODEL), const2),
    ]
    out_shape = (
        jax.ShapeDtypeStruct((nt, ts, D_MODEL), F32),
        jax.ShapeDtypeStruct((b, WINDOW, D_KV), F32),
        jax.ShapeDtypeStruct((b, WINDOW, D_KV), F32),
        jax.ShapeDtypeStruct((b, CONV_WIDTH - 1, CONV_DIM), F32),
        jax.ShapeDtypeStruct((b, D_SSM, SSM_STATE), F32),
    )
    out_specs = (
        pl.BlockSpec((1, ts, D_MODEL), prev2),
        pl.BlockSpec((1, WINDOW, D_KV), per_stream),
        pl.BlockSpec((1, WINDOW, D_KV), per_stream),
        pl.BlockSpec((1, CONV_WIDTH - 1, CONV_DIM), per_stream),
        pl.BlockSpec((1, D_SSM, SSM_STATE), per_stream),
    )
    scratch = [
        pltpu.VMEM((2, D_ATTN, ts), BF16),
        pltpu.VMEM((2, WINDOW + ts, D_KV), F32),
        pltpu.VMEM((2, D_KV, WINDOW + ts), F32),
        pltpu.VMEM((2, ts, D_SSM), F32),
        pltpu.VMEM((2, CONV_DIM // LANES, HIST + ts, LANES), F32),
        pltpu.VMEM((2, CONV_DIM // LANES, ts, LANES), F32),
        pltpu.VMEM((2, ts, DT_PAD), F32),
        pltpu.VMEM((ts, D_SSM), F32),
        pltpu.VMEM((2, D_ATTN, ts), BF16),
        pltpu.VMEM((2, ts, D_SSM), BF16),
        pltpu.VMEM((SSM_STATE, D_SSM), F32),
    ]
    x3 = x.reshape(nt, ts, D_MODEL)
    outs = pl.pallas_call(
        functools.partial(_mix_prompt_kernel, ts=ts, nj=nj),
        grid=(nt + 2,),
        in_specs=in_specs,
        out_specs=out_specs,
        out_shape=out_shape,
        scratch_shapes=scratch,
        compiler_params=pltpu.CompilerParams(dimension_semantics=("arbitrary",), vmem_limit_bytes=VMEM_LIMIT),
        name="mix_prompt",
    )(p["sinks"], p["d_skip"], x3, x3, p["ln1_g"], p["w_qvt"], p["w_nat"], p["conv_w"], p["conv_b"], p["dt_bias"],
      p["a_log"], p["ssm_norm_w"], p["w_out"])
    return (outs[0].reshape(b, s, D_MODEL),) + tuple(outs[1:])


def _mix_sample_kernel(sinks_ref, dskip_ref, x_ref, ck_ref, cv_ref, sconv_ref, sssm_ref, g1_ref, w_qvt_ref,
                       w_nat_ref, conv_w_ref, conv_b_ref, dtb_ref, alog_ref, normw_ref, w_out_ref,
                       h1_ref, kout_ref, vout_ref, convout_ref, ssmout_ref,
                       qt_s, vt_s, z_s, xbcall_s, dt_s, xbc_s, y_s, attnt_s, yn_s, ht_s, *, nb, t):
    i = pl.program_id(0)
    n = nb * t
    per_tile = BLK // t

    @pl.when(i == 0)
    def _():
        u = _rms(x_ref[...], g1_ref[...]).astype(BF16)
        qt = (_dot_nt(w_qvt_ref[0:D_ATTN, :], u) * SCALE).astype(BF16)
        vt = _dot_nt(w_qvt_ref[D_ATTN:, :], u)
        for c in range(n // BLK):
            qt_s[c] = qt[:, c * BLK:(c + 1) * BLK]
            vt_s[c] = vt[:, c * BLK:(c + 1) * BLK]
        vout_ref[...] = vt.T
        kout_ref[...] = _nat_proj(u, w_nat_ref, N_K, N_DT)
        z_s[...] = _nat_proj(u, w_nat_ref, N_Z, N_X)
        xbcall_s[...] = _nat_proj(u, w_nat_ref, N_X, D_NAT)
        dt_s[...] = _softplus(_nat_proj(u, w_nat_ref, N_DT, N_Z) + dtb_ref[...])
        attnt_s[...] = jnp.zeros(attnt_s.shape, BF16)
        xbc_s[...] = jnp.zeros(xbc_s.shape, F32)

    @pl.when((i >= 1) & (i <= nb))
    def _():
        b = i - 1
        rows = pl.ds(pl.multiple_of(b * t, t), t)
        c = b // per_tile
        lo = (b % per_tile) * t
        kwin = jnp.concatenate([kout_ref[pl.ds(pl.multiple_of(c * BLK, BLK), BLK), :], ck_ref[0]], axis=0)
        vtwin = jnp.concatenate([vt_s[c], cv_ref[0].T], axis=1)
        key = lax.broadcasted_iota(jnp.int32, (2 * BLK, BLK), 0)
        mask_t = ((key >= lo) & (key < lo + t)) | (key >= BLK)
        xbc_s[HIST - (CONV_WIDTH - 1):HIST, :] = sconv_ref[0]
        xbc_s[HIST:HIST + t, :] = xbcall_s[rows, :]
        convout_ref[0] = xbc_s[HIST + t - (CONV_WIDTH - 1):HIST + t, :]
        live = lax.broadcasted_iota(jnp.int32, (BLK, 1), 0) < t
        xc = jnp.where(live, _conv(xbc_s, BLK, conv_w_ref, conv_b_ref), 0.0)
        dt = jnp.concatenate([dt_s[rows, :], jnp.zeros((BLK - t, DT_PAD), F32)], axis=0)
        ht_s[...] = sssm_ref[0].T
        ssd = _ssd_steps(lambda i: xc[:, i * LANES:(i + 1) * LANES], dt, -jnp.exp(alog_ref[...]), dskip_ref, ht_s,
                         y_s, 0)

        outs = [None] * N_KV_HEADS

        def emit(kv, o):
            outs[kv] = o

        for _ in _attn_steps(qt_s[c], kwin.astype(BF16), vtwin.astype(BF16), mask_t, sinks_ref, emit,
                             lambda: next(ssd, None)):
            pass
        for _ in ssd:
            pass
        o_t = jnp.concatenate(outs, axis=0)
        lane = lax.broadcasted_iota(jnp.int32, (D_ATTN, BLK), 1)
        attnt_s[c] = jnp.where((lane >= lo) & (lane < lo + t), o_t.astype(BF16), attnt_s[c])
        ssmout_ref[0] = ht_s[...].T
        yn_s[rows, :] = _gated_norm(y_s[0:t, :], z_s[rows, :], normw_ref[...]).astype(BF16)

    @pl.when(i == nb + 1)
    def _():
        for c in range(n // BLK):
            r = slice(c * BLK, (c + 1) * BLK)
            h1_ref[r, :] = _out_proj(x_ref[r, :], attnt_s[c], yn_s[r, :], w_out_ref)


def _mix_sample(x, cache_k, cache_v, state_conv, state_ssm, p):
    nb, t, _ = x.shape
    n = nb * t
    assert BLK % t == 0 and n % BLK == 0
    const2 = lambda i: (0, 0)
    per = lambda i: (jnp.clip(i - 1, 0, nb - 1), 0, 0)
    smem = pl.BlockSpec(memory_space=pltpu.SMEM)
    in_specs = [
        smem, smem,
        pl.BlockSpec((n, D_MODEL), const2),
        pl.BlockSpec((1, WINDOW, D_KV), per),
        pl.BlockSpec((1, WINDOW, D_KV), per),
        pl.BlockSpec((1, CONV_WIDTH - 1, CONV_DIM), per),
        pl.BlockSpec((1, D_SSM, SSM_STATE), per),
        pl.BlockSpec((1, D_MODEL), const2),
        pl.BlockSpec((D_ATTN + D_KV, D_MODEL), const2),
        pl.BlockSpec((D_MODEL, D_NAT), const2),
        pl.BlockSpec((CONV_WIDTH, CONV_DIM), const2),
        pl.BlockSpec((1, CONV_DIM), const2),
        pl.BlockSpec((1, DT_PAD), const2),
        pl.BlockSpec((1, DT_PAD), const2),
        pl.BlockSpec((1, D_SSM), const2),
        pl.BlockSpec((D_MIX, D_MODEL), const2),
    ]
    out_shape = (
        jax.ShapeDtypeStruct((n, D_MODEL), F32),
        jax.ShapeDtypeStruct((n, D_KV), F32),
        jax.ShapeDtypeStruct((n, D_KV), F32),
        jax.ShapeDtypeStruct((nb, CONV_WIDTH - 1, CONV_DIM), F32),
        jax.ShapeDtypeStruct((nb, D_SSM, SSM_STATE), F32),
    )
    out_specs = (
        pl.BlockSpec((n, D_MODEL), const2),
        pl.BlockSpec((n, D_KV), const2),
        pl.BlockSpec((n, D_KV), const2),
        pl.BlockSpec((1, CONV_WIDTH - 1, CONV_DIM), per),
        pl.BlockSpec((1, D_SSM, SSM_STATE), per),
    )
    scratch = [
        pltpu.VMEM((n // BLK, D_ATTN, BLK), BF16),
        pltpu.VMEM((n // BLK, D_KV, BLK), F32),
        pltpu.VMEM((n, D_SSM), F32),
        pltpu.VMEM((n, CONV_DIM), F32),
        pltpu.VMEM((n, DT_PAD), F32),
        pltpu.VMEM((HIST + BLK, CONV_DIM), F32),
        pltpu.VMEM((BLK, D_SSM), F32),
        pltpu.VMEM((n // BLK, D_ATTN, BLK), BF16),
        pltpu.VMEM((n, D_SSM), BF16),
        pltpu.VMEM((SSM_STATE, D_SSM), F32),
    ]
    return pl.pallas_call(
        functools.partial(_mix_sample_kernel, nb=nb, t=t),
        grid=(nb + 2,),
        in_specs=in_specs,
        out_specs=out_specs,
        out_shape=out_shape,
        scratch_shapes=scratch,
        compiler_params=pltpu.CompilerParams(dimension_semantics=("arbitrary",), vmem_limit_bytes=VMEM_LIMIT),
        name="mix_sample",
    )(p["sinks"], p["d_skip"], x.reshape(n, D_MODEL), cache_k, cache_v, state_conv, state_ssm, p["ln1_g"],
      p["w_qvt"], p["w_nat"], p["conv_w"], p["conv_b"], p["dt_bias"], p["a_log"], p["ssm_norm_w"], p["w_out"])


FFN_ROWS = 256


def _ffn_kernel(h_ref, g2_ref, wg_ref, wu_ref, wd_ref, gf_ref, o_ref):
    tm = h_ref.shape[0]
    chunks = [slice(r, min(r + FFN_ROWS, tm)) for r in range(0, tm, FFN_ROWS)]

    def up(rows):
        u = _rms(h_ref[rows, :], g2_ref[...]).astype(BF16)
        return _dot(u, wg_ref[...]), _dot(u, wu_ref[...])

    def down(rows, gate, upv):
        mid = (_silu(gate) * upv).astype(BF16)
        o_ref[rows, :] = _rms(h_ref[rows, :] + _dot(mid, wd_ref[...]), gf_ref[...])

    pending = up(chunks[0])
    for i, rows in enumerate(chunks):
        nxt = up(chunks[i + 1]) if i + 1 < len(chunks) else None
        down(rows, *pending)
        pending = nxt


def _ffn(h, p, tm):
    n = h.shape[0]
    d_ff = p["w_gate"].shape[1]
    const2 = lambda i: (0, 0)
    return pl.pallas_call(
        _ffn_kernel,
        grid=(n // tm,),
        in_specs=[
            pl.BlockSpec((tm, D_MODEL), lambda i: (i, 0)),
            pl.BlockSpec((1, D_MODEL), const2),
            pl.BlockSpec((D_MODEL, d_ff), const2),
            pl.BlockSpec((D_MODEL, d_ff), const2),
            pl.BlockSpec((d_ff, D_MODEL), const2),
            pl.BlockSpec((1, D_MODEL), const2),
        ],
        out_specs=pl.BlockSpec((tm, D_MODEL), lambda i: (i, 0)),
        out_shape=jax.ShapeDtypeStruct((n, D_MODEL), F32),
        compiler_params=pltpu.CompilerParams(dimension_semantics=("arbitrary",), vmem_limit_bytes=VMEM_LIMIT),
        name="ffn",
    )(h, p["ln2_g"], p["w_gate"], p["w_up"], p["w_down"], p["final_g"])


def _prep_params(ln1_g, w_in, conv_w, conv_b, dt_bias, a_log, d_skip, ssm_norm_w, sinks, w_out, ln2_g, w_gate,
                 w_up, w_down, final_g):
    pad_h = lambda v: jnp.pad(v.reshape(1, SSM_HEADS), ((0, 0), (0, DT_PAD - SSM_HEADS)))
    o_k, o_v, o_z = D_ATTN, D_ATTN + D_KV, D_ATTN + 2 * D_KV
    w_q, w_k, w_v = w_in[:, :o_k], w_in[:, o_k:o_v], w_in[:, o_v:o_z]
    w_zx, w_dt = w_in[:, o_z:o_z + D_SSM + CONV_DIM], w_in[:, o_z + D_SSM + CONV_DIM:]
    w_dt = jnp.pad(w_dt, ((0, 0), (0, DT_PAD - w_dt.shape[1])))
    return {
        "ln1_g": ln1_g.reshape(1, D_MODEL),
        "w_qvt": jnp.concatenate([w_q, w_v], axis=1).T.astype(BF16),
        "w_nat": jnp.concatenate([w_k, w_dt, w_zx], axis=1).astype(BF16),
        "conv_w": conv_w,
        "conv_b": conv_b.reshape(1, CONV_DIM),
        "dt_bias": pad_h(dt_bias),
        "a_log": pad_h(a_log),
        "d_skip": d_skip,
        "ssm_norm_w": ssm_norm_w.reshape(1, D_SSM),
        "sinks": sinks,
        "w_out": w_out.astype(BF16),
        "ln2_g": ln2_g.reshape(1, D_MODEL),
        "w_gate": w_gate.astype(BF16),
        "w_up": w_up.astype(BF16),
        "w_down": w_down.astype(BF16),
        "final_g": final_g.reshape(1, D_MODEL),
    }


def _layer(x_prompt, x_sample, cache_k, cache_v, state_conv, state_ssm, p, ts, tm):
    b, s, _ = x_prompt.shape
    nb, t, _ = x_sample.shape
    h1p, kp, vp, cp, sp = _mix_prompt(x_prompt, p, ts)
    h1s, kn, vn, cn, sn = _mix_sample(x_sample, cache_k.reshape(nb, WINDOW, D_KV), cache_v.reshape(nb, WINDOW, D_KV),
                                      state_conv, state_ssm.reshape(nb, D_SSM, SSM_STATE), p)
    yp = _ffn(h1p.reshape(b * s, D_MODEL), p, tm).reshape(b, s, D_MODEL)
    ys = _ffn(h1s, p, min(tm, nb * t)).reshape(nb, t, D_MODEL)
    return (yp, ys,
            kp.reshape(b, WINDOW, N_KV_HEADS, HEAD_DIM), vp.reshape(b, WINDOW, N_KV_HEADS, HEAD_DIM), cp,
            sp.reshape(b, SSM_HEADS, SSM_HEAD_DIM, SSM_STATE),
            kn.reshape(nb, t, N_KV_HEADS, HEAD_DIM), vn.reshape(nb, t, N_KV_HEADS, HEAD_DIM), cn,
            sn.reshape(nb, SSM_HEADS, SSM_HEAD_DIM, SSM_STATE))


def kernel(x_prompt, x_sample, cache_k, cache_v, state_conv, state_ssm, ln1_g, w_in, conv_w, conv_b, dt_bias, a_log,
           d_skip, ssm_norm_w, sinks, w_out, ln2_g, w_gate, w_up, w_down, final_g):
    assert w_in.shape[0] == 1, "one layer"
    p = _prep_params(ln1_g[0], w_in[0], conv_w[0], conv_b[0], dt_bias[0], a_log[0], d_skip[0], ssm_norm_w[0],
                     sinks[0], w_out[0], ln2_g[0], w_gate[0], w_up[0], w_down[0], final_g)
    outs = _layer(x_prompt, x_sample, cache_k[0], cache_v[0], state_conv[0], state_ssm[0], p, ts=256, tm=1024)
    return tuple(o[None] if i >= 2 else o for i, o in enumerate(outs))
```

```python
import functools

import jax
import jax.numpy as jnp
from jax import lax
from jax.experimental import pallas as pl
from jax.experimental.pallas import tpu as pltpu

F32 = jnp.float32
BF16 = jnp.bfloat16

D_MODEL = 1024
CHUNK = 64
WINDOW = 128
HEAD_DIM = 64
N_HEADS = 16
N_KV_HEADS = 4
Q_PER_KV = N_HEADS // N_KV_HEADS
D_ATTN = N_HEADS * HEAD_DIM
D_KV = N_KV_HEADS * HEAD_DIM
SCALE = HEAD_DIM ** -0.5
SSM_HEADS = 16
SSM_HEAD_DIM = 64
SSM_GROUPS = 4
SSM_STATE = 128
D_SSM = SSM_HEADS * SSM_HEAD_DIM
D_BC = SSM_GROUPS * SSM_STATE
CONV_WIDTH = 4
CONV_DIM = D_SSM + 2 * D_BC
D_MIX = D_ATTN + D_SSM
EPS = 1e-6

LANES = 128
BLK = 2 * CHUNK
DT_PAD = LANES
N_K, N_DT, N_Z, N_X = 0, D_KV, D_KV + DT_PAD, D_KV + DT_PAD + D_SSM
D_NAT = N_X + CONV_DIM
HIST = 8
VMEM_LIMIT = 56 * 1024 * 1024


def _rms(x, g):
    ms = jnp.mean(x * x, axis=-1, keepdims=True)
    return (x * lax.rsqrt(ms + EPS)) * g


def _silu(x):
    h = 0.5 * x
    return h + h * jnp.tanh(h)


def _softplus(x):
    return jnp.maximum(x, 0.0) + jnp.log1p(jnp.exp(-jnp.abs(x)))


def _dot(a, b):
    return jnp.dot(a, b, preferred_element_type=F32)


def _dot_nt(a, b):
    return lax.dot_general(a, b, (((1,), (1,)), ((), ())), preferred_element_type=F32)


def _dot_tn(a, b):
    return lax.dot_general(a, b, (((0,), (0,)), ((), ())), preferred_element_type=F32)


def _left_half(shape):
    return lax.broadcasted_iota(jnp.int32, shape, len(shape) - 1) < HEAD_DIM


def _attn_steps(qt_blk, kwin, vtwin, mask_t, sinks_ref, emit, between=None):
    nq = Q_PER_KV * BLK
    mask4 = jnp.concatenate([mask_t] * Q_PER_KV, axis=1)
    zeros = jnp.zeros((HEAD_DIM, nq), BF16)

    def scores(kv):
        slab, half = kv // 2, kv % 2
        qk = jnp.concatenate([qt_blk[(kv * Q_PER_KV + g) * HEAD_DIM:(kv * Q_PER_KV + g + 1) * HEAD_DIM, :]
                              for g in range(Q_PER_KV)], axis=1)
        rhs = jnp.concatenate([qk, zeros] if half == 0 else [zeros, qk], axis=0)
        return _dot(kwin[:, slab * LANES:(slab + 1) * LANES], rhs)

    s_next = scores(0)
    for kv in range(N_KV_HEADS):
        s = s_next
        if kv + 1 < N_KV_HEADS:
            s_next = scores(kv + 1)
        if between is not None:
            between()
        s = jnp.where(mask4, s, -jnp.inf)
        sink = jnp.concatenate([jnp.full((1, BLK), sinks_ref[kv * Q_PER_KV + g], F32) for g in range(Q_PER_KV)],
                               axis=1)
        m = jnp.maximum(jnp.max(s, axis=0, keepdims=True), sink)
        e = jnp.exp(s - m)
        denom = jnp.sum(e, axis=0, keepdims=True) + jnp.exp(sink - m)
        p = (e * (1.0 / denom)).astype(BF16)
        o = _dot(vtwin[kv * HEAD_DIM:(kv + 1) * HEAD_DIM, :], p)
        emit(kv, jnp.concatenate([o[:, g * BLK:(g + 1) * BLK] for g in range(Q_PER_KV)], axis=0))
        yield


def _split3(x):
    hi = x.astype(BF16)
    r1 = x - hi.astype(F32)
    mid = r1.astype(BF16)
    lo = (r1 - mid.astype(F32)).astype(BF16)
    return hi, mid, lo


def _ssd_steps(xc, dt, a_row, dskip_ref, ht_ref, y_ref, r0, between=None):
    left = _left_half((BLK, LANES))
    row_i = lax.broadcasted_iota(jnp.int32, (BLK, BLK), 0)
    col_i = lax.broadcasted_iota(jnp.int32, (BLK, BLK), 1)
    tri = col_i <= row_i
    tri_b = tri.astype(BF16)
    da = dt * a_row
    hi, mid, lo = _split3(da)
    parts = _dot(tri_b, jnp.concatenate([hi, mid, lo], axis=1))
    cs = (parts[:, 0:LANES] + parts[:, LANES:2 * LANES]) + parts[:, 2 * LANES:]
    cs_t = cs.T
    if between is not None:
        between()

    def colb(a, h):
        return jnp.broadcast_to(a[:, h:h + 1], (BLK, LANES))

    def group_products(g):
        bg = xc((D_SSM + g * SSM_STATE) // LANES)
        cg = xc((D_SSM + D_BC + g * SSM_STATE) // LANES).astype(BF16)
        cb = _dot_nt(cg, bg.astype(BF16))
        ht_g = ht_ref[:, g * 2 * LANES:(g + 1) * 2 * LANES]
        yoff = _dot(cg, ht_g.astype(BF16))
        return bg, cb, ht_g, yoff

    nxt = group_products(0)
    for g in range(SSM_GROUPS):
        bg, cb, ht_g, yoff = nxt
        gs = slice(g * 2 * LANES, (g + 1) * 2 * LANES)
        if g + 1 < SSM_GROUPS:
            nxt = group_products(g + 1)
        if between is not None:
            between()
        xdec, cdec, late = [], [], []
        for hp in range(2):
            h0 = g * 4 + 2 * hp
            h1 = h0 + 1
            cols = slice((g * 2 + hp) * LANES, (g * 2 + hp + 1) * LANES)
            xs2 = xc(g * 2 + hp)
            csb = [colb(cs, h0), colb(cs, h1)]
            cs2 = jnp.where(left, csb[0], csb[1])
            ecs2 = jnp.exp(cs2)
            dec2 = jnp.exp(cs2[BLK - 1:BLK, :] - cs2)
            dt2 = jnp.where(left, colb(dt, h0), colb(dt, h1))
            xd2 = xs2 * dt2
            ms = []
            for idx, h in enumerate((h0, h1)):
                seg = csb[idx] - jnp.broadcast_to(cs_t[h:h + 1, :], (BLK, BLK))
                ms.append((cb * jnp.exp(jnp.where(tri, seg, -jnp.inf))).astype(BF16))
            xd2b = xd2.astype(BF16)
            zero = jnp.zeros_like(xd2b)
            xd_diag = jnp.concatenate([jnp.where(left, xd2b, zero), jnp.where(left, zero, xd2b)], axis=0)
            yd = _dot(jnp.concatenate(ms, axis=1), xd_diag)
            dsk2 = jnp.where(left, dskip_ref[h0], dskip_ref[h1])
            late.append((cols, yd, yoff[:, hp * LANES:(hp + 1) * LANES] * ecs2, xs2 * dsk2))
            xdec.append((xd2 * dec2).astype(BF16))
            cdec.append(ecs2[BLK - 1:BLK, :])
        xdec = jnp.concatenate(xdec, axis=1)
        cdec = jnp.concatenate(cdec, axis=1)
        ht_ref[:, gs] = cdec * ht_g + _dot(bg.T.astype(BF16), xdec)
        for cols, yd, off, skip in late:
            y_ref[pl.ds(r0, BLK), cols] = (yd + off) + skip
        yield


def _gated_norm(y, z, w):
    g = y * _silu(z)
    parts = []
    width = D_SSM // SSM_GROUPS
    for i in range(SSM_GROUPS):
        gg = g[:, i * width:(i + 1) * width]
        parts.append(gg * lax.rsqrt(jnp.mean(gg * gg, axis=-1, keepdims=True) + EPS))
    return jnp.concatenate(parts, axis=1) * w


def _conv(xbc_ref, rows, conv_w_ref, conv_b_ref):
    acc = conv_b_ref[...]
    for i in range(CONV_WIDTH):
        off = HIST - (CONV_WIDTH - 1) + i
        acc = acc + xbc_ref[off:off + rows, :] * conv_w_ref[i:i + 1, :]
    return _silu(acc)


CONV_PHASES = 4


def _conv_slab(xbc_ref, xc_ref, c, rows, conv_w_ref, conv_b_ref):
    n = rows // CONV_PHASES
    cols = slice(c * LANES, (c + 1) * LANES)
    taps = [conv_w_ref[i:i + 1, cols] for i in range(CONV_WIDTH)]
    bias = conv_b_ref[:, cols]
    lo = -(CONV_WIDTH - 1)
    shifted = [xbc_ref[c, pl.ds(HIST + s, n, stride=CONV_PHASES), :] for s in range(lo, CONV_PHASES)]
    for r in range(CONV_PHASES):
        acc = bias
        for i in range(CONV_WIDTH):
            acc = acc + shifted[r + i] * taps[i]
        xc_ref[c, pl.ds(r, n, stride=CONV_PHASES), :] = _silu(acc)


def _nat_proj(u, w_nat_ref, lo, hi):
    return _dot(u, w_nat_ref[:, lo:hi])


def _out_proj(x, attn_t, yn, w_out_ref):
    return x + (_dot_tn(attn_t, w_out_ref[0:D_ATTN, :]) + _dot(yn, w_out_ref[D_ATTN:, :]))


def _mix_prompt_kernel(sinks_ref, dskip_ref, xp_ref, xo_ref, g1_ref, w_qvt_ref, w_nat_ref, conv_w_ref, conv_b_ref,
                       dtb_ref, alog_ref, normw_ref, w_out_ref,
                       h1_ref, kout_ref, vout_ref, convout_ref, ssmout_ref,
                       qt_s, kh_s, vth_s, z_s, xbc_s, xc_s, dt_s, y_s, attnt_s, yn_s, ht_s, *, ts, nj):
    t = pl.program_id(0)
    sp = t % 2
    sc = 1 - sp
    jp = t % nj
    jc = (t + nj - 1) % nj

    @pl.when(t == 0)
    def _():
        for ref in (qt_s, kh_s, vth_s, z_s, xbc_s, xc_s, dt_s, attnt_s, yn_s):
            ref[...] = jnp.zeros(ref.shape, ref.dtype)

    @pl.when(jc == 0)
    def _():
        ht_s[...] = jnp.zeros(ht_s.shape, F32)

    def project():
        fresh = jp == 0
        kh_s[sp, 0:WINDOW, :] = jnp.where(fresh, 0.0, kh_s[sc, ts:ts + WINDOW, :])
        vth_s[sp, :, 0:WINDOW] = jnp.where(fresh, 0.0, vth_s[sc, :, ts:ts + WINDOW])
        xbc_s[sp, :, 0:HIST, :] = jnp.where(fresh, 0.0, xbc_s[sc, :, ts:ts + HIST, :])
        u = _rms(xp_ref[0], g1_ref[...]).astype(BF16)
        attn_t = attnt_s[sp]
        yn = yn_s[sp]
        n_piece = 4
        slabs = CONV_DIM // LANES // n_piece

        def xbc(c):
            w = CONV_DIM // n_piece
            piece = _nat_proj(u, w_nat_ref, N_X + c * w, N_X + (c + 1) * w)
            for k in range(slabs):
                xbc_s[sp, c * slabs + k, HIST:HIST + ts, :] = piece[:, k * LANES:(k + 1) * LANES]

        def conv(c):
            for k in range(slabs):
                _conv_slab(xbc_s.at[sp], xc_s.at[sp], c * slabs + k, ts, conv_w_ref, conv_b_ref)

        def q(c):
            w = D_ATTN // n_piece
            qt_s[sp, c * w:(c + 1) * w, :] = (_dot_nt(w_qvt_ref[c * w:(c + 1) * w, :], u) * SCALE).astype(BF16)

        def z(c):
            w = D_SSM // n_piece
            z_s[sp, :, c * w:(c + 1) * w] = _nat_proj(u, w_nat_ref, N_Z + c * w, N_Z + (c + 1) * w)

        def out(c):
            w = D_MODEL // n_piece
            cols = slice(c * w, (c + 1) * w)
            h1_ref[0, :, cols] = xo_ref[0, :, cols] + (_dot_tn(attn_t, w_out_ref[0:D_ATTN, cols])
                                                       + _dot(yn, w_out_ref[D_ATTN:, cols]))

        def v():
            vth_s[sp, :, WINDOW:WINDOW + ts] = _dot_nt(w_qvt_ref[D_ATTN:, :], u)

        def k():
            kh_s[sp, WINDOW:WINDOW + ts, :] = _nat_proj(u, w_nat_ref, N_K, N_DT)

        def dt():
            dt_s[sp] = _softplus(_nat_proj(u, w_nat_ref, N_DT, N_Z) + dtb_ref[...])

        placement = [
            [(xbc, 0)], [(xbc, 1), (conv, 0)], [(xbc, 2), (conv, 1)], [(xbc, 3), (conv, 2)],
            [(q, 0)], [(q, 1)], [(q, 2)], [(q, 3)], [(z, 0)],
            [(out, 0), (conv, 3)], [(out, 1)], [(out, 2)], [(out, 3)],
            [(z, 1)], [(z, 2)], [(z, 3)], [(v,)], [(k,)],
            [(dt,)],
        ]
        for group in placement:
            yield
            for fn, *args in group:
                fn(*args)

    pieces = project()
    advance = lambda: next(pieces, None)
    advance()

    a_row = -jnp.exp(alog_ref[...])
    key = lax.broadcasted_iota(jnp.int32, (2 * BLK, BLK), 0)
    qc = lax.broadcasted_iota(jnp.int32, (2 * BLK, BLK), 1) // CHUNK
    band = (key >= qc * CHUNK) & (key < (qc + 3) * CHUNK)
    for p in range(ts // BLK):
        r0 = p * BLK
        first = jnp.where(jc == 0, WINDOW, 0) if p == 0 else 0
        mask_t = band & (key >= first)

        def emit(kv, o, r0=r0):
            attnt_s[sc, kv * Q_PER_KV * HEAD_DIM:(kv + 1) * Q_PER_KV * HEAD_DIM, r0:r0 + BLK] = o.astype(BF16)

        attn = _attn_steps(qt_s[sc, :, r0:r0 + BLK], kh_s[sc, r0:r0 + 2 * BLK, :].astype(BF16),
                           vth_s[sc, :, r0:r0 + 2 * BLK].astype(BF16), mask_t, sinks_ref, emit, advance)
        ssd = _ssd_steps(lambda i, r0=r0: xc_s[sc, i, r0:r0 + BLK, :], dt_s[sc, r0:r0 + BLK, :], a_row, dskip_ref,
                         ht_s, y_s, r0, advance)
        for _ in attn:
            pass
        for _ in ssd:
            pass
    for p in range(ts // BLK):
        rows = slice(p * BLK, (p + 1) * BLK)
        advance()
        yn_s[sc, rows, :] = _gated_norm(y_s[rows, :], z_s[sc, rows, :], normw_ref[...]).astype(BF16)
    for _ in pieces:
        pass

    @pl.when(jc == nj - 1)
    def _():
        kout_ref[0] = kh_s[sc, ts:ts + WINDOW, :]
        vout_ref[0] = vth_s[sc, :, ts:ts + WINDOW].T
        for c in range(CONV_DIM // LANES):
            convout_ref[0, :, c * LANES:(c + 1) * LANES] = xbc_s[sc, c, HIST + ts - (CONV_WIDTH - 1):HIST + ts, :]
        ssmout_ref[0] = ht_s[...].T


def _mix_prompt(x, p, ts):
    b, s, _ = x.shape
    nj = s // ts
    nt = b * nj
    const2 = lambda t: (0, 0)
    prev2 = lambda t: (jnp.maximum(t - 2, 0), 0, 0)
    per_stream = lambda t: (jnp.clip(t - 1, 0, nt - 1) // nj, 0, 0)
    smem = pl.BlockSpec(memory_space=pltpu.SMEM)
    in_specs = [
        smem, smem,
        pl.BlockSpec((1, ts, D_MODEL), lambda t: (jnp.minimum(t, nt - 1), 0, 0)),
        pl.BlockSpec((1, ts, D_MODEL), prev2),
        pl.BlockSpec((1, D_MODEL), const2),
        pl.BlockSpec((D_ATTN + D_KV, D_MODEL), const2),
        pl.BlockSpec((D_MODEL, D_NAT), const2),
        pl.BlockSpec((CONV_WIDTH, CONV_DIM), const2),
        pl.BlockSpec((1, CONV_DIM), const2),
        pl.BlockSpec((1, DT_PAD), const2),
        pl.BlockSpec((1, DT_PAD), const2),
        pl.BlockSpec((1, D_SSM), const2),
        pl.BlockSpec((D_MIX, D_MODEL), const2),
    ]
    out_shape = (
        jax.ShapeDtypeStruct((nt, ts, D_MODEL), F32),
        jax.ShapeDtypeStruct((b, WINDOW, D_KV), F32),
        jax.ShapeDtypeStruct((b, WINDOW, D_KV), F32),
        jax.ShapeDtypeStruct((b, CONV_WIDTH - 1, CONV_DIM), F32),
        jax.ShapeDtypeStruct((b, D_SSM, SSM_STATE), F32),
    )
    out_specs = (
        pl.BlockSpec((1, ts, D_MODEL), prev2),
        pl.BlockSpec((1, WINDOW, D_KV), per_stream),
        pl.BlockSpec((1, WINDOW, D_KV), per_stream),
        pl.BlockSpec((1, CONV_WIDTH - 1, CONV_DIM), per_stream),
        pl.BlockSpec((1, D_SSM, SSM_STATE), per_stream),
    )
    scratch = [
        pltpu.VMEM((2, D_ATTN, ts), BF16),
        pltpu.VMEM((2, WINDOW + ts, D_KV), F32),
        pltpu.VMEM((2, D_KV, WINDOW + ts), F32),
        pltpu.VMEM((2, ts, D_SSM), F32),
        pltpu.VMEM((2, CONV_DIM // LANES, HIST + ts, LANES), F32),
        pltpu.VMEM((2, CONV_DIM // LANES, ts, LANES), F32),
        pltpu.VMEM((2, ts, DT_PAD), F32),
        pltpu.VMEM((ts, D_SSM), F32),
        pltpu.VMEM((2, D_ATTN, ts), BF16),
        pltpu.VMEM((2, ts, D_SSM), BF16),
        pltpu.VMEM((SSM_STATE, D_SSM), F32),
    ]
    x3 = x.reshape(nt, ts, D_MODEL)
    outs = pl.pallas_call(
        functools.partial(_mix_prompt_kernel, ts=ts, nj=nj),
        grid=(nt + 2,),
        in_specs=in_specs,
        out_specs=out_specs,
        out_shape=out_shape,
        scratch_shapes=scratch,
        compiler_params=pltpu.CompilerParams(dimension_semantics=("arbitrary",), vmem_limit_bytes=VMEM_LIMIT),
        name="mix_prompt",
    )(p["sinks"], p["d_skip"], x3, x3, p["ln1_g"], p["w_qvt"], p["w_nat"], p["conv_w"], p["conv_b"], p["dt_bias"],
      p["a_log"], p["ssm_norm_w"], p["w_out"])
    return (outs[0].reshape(b, s, D_MODEL),) + tuple(outs[1:])


def _mix_sample_kernel(sinks_ref, dskip_ref, x_ref, ck_ref, cv_ref, sconv_ref, sssm_ref, g1_ref, w_qvt_ref,
                       w_nat_ref, conv_w_ref, conv_b_ref, dtb_ref, alog_ref, normw_ref, w_out_ref,
                       h1_ref, kout_ref, vout_ref, convout_ref, ssmout_ref,
                       qt_s, vt_s, z_s, xbcall_s, dt_s, xbc_s, y_s, attnt_s, yn_s, ht_s, *, nb, t):
    i = pl.program_id(0)
    n = nb * t
    per_tile = BLK // t

    @pl.when(i == 0)
    def _():
        u = _rms(x_ref[...], g1_ref[...]).astype(BF16)
        qt = (_dot_nt(w_qvt_ref[0:D_ATTN, :], u) * SCALE).astype(BF16)
        vt = _dot_nt(w_qvt_ref[D_ATTN:, :], u)
        for c in range(n // BLK):
            qt_s[c] = qt[:, c * BLK:(c + 1) * BLK]
            vt_s[c] = vt[:, c * BLK:(c + 1) * BLK]
        vout_ref[...] = vt.T
        kout_ref[...] = _nat_proj(u, w_nat_ref, N_K, N_DT)
        z_s[...] = _nat_proj(u, w_nat_ref, N_Z, N_X)
        xbcall_s[...] = _nat_proj(u, w_nat_ref, N_X, D_NAT)
        dt_s[...] = _softplus(_nat_proj(u, w_nat_ref, N_DT, N_Z) + dtb_ref[...])
        attnt_s[...] = jnp.zeros(attnt_s.shape, BF16)
        xbc_s[...] = jnp.zeros(xbc_s.shape, F32)

    @pl.when((i >= 1) & (i <= nb))
    def _():
        b = i - 1
        rows = pl.ds(pl.multiple_of(b * t, t), t)
        c = b // per_tile
        lo = (b % per_tile) * t
        kwin = jnp.concatenate([kout_ref[pl.ds(pl.multiple_of(c * BLK, BLK), BLK), :], ck_ref[0]], axis=0)
        vtwin = jnp.concatenate([vt_s[c], cv_ref[0].T], axis=1)
        key = lax.broadcasted_iota(jnp.int32, (2 * BLK, BLK), 0)
        mask_t = ((key >= lo) & (key < lo + t)) | (key >= BLK)
        xbc_s[HIST - (CONV_WIDTH - 1):HIST, :] = sconv_ref[0]
        xbc_s[HIST:HIST + t, :] = xbcall_s[rows, :]
        convout_ref[0] = xbc_s[HIST + t - (CONV_WIDTH - 1):HIST + t, :]
        live = lax.broadcasted_iota(jnp.int32, (BLK, 1), 0) < t
        xc = jnp.where(live, _conv(xbc_s, BLK, conv_w_ref, conv_b_ref), 0.0)
        dt = jnp.concatenate([dt_s[rows, :], jnp.zeros((BLK - t, DT_PAD), F32)], axis=0)
        ht_s[...] = sssm_ref[0].T
        ssd = _ssd_steps(lambda i: xc[:, i * LANES:(i + 1) * LANES], dt, -jnp.exp(alog_ref[...]), dskip_ref, ht_s,
                         y_s, 0)

        outs = [None] * N_KV_HEADS

        def emit(kv, o):
            outs[kv] = o

        for _ in _attn_steps(qt_s[c], kwin.astype(BF16), vtwin.astype(BF16), mask_t, sinks_ref, emit,
                             lambda: next(ssd, None)):
            pass
        for _ in ssd:
            pass
        o_t = jnp.concatenate(outs, axis=0)
        lane = lax.broadcasted_iota(jnp.int32, (D_ATTN, BLK), 1)
        attnt_s[c] = jnp.where((lane >= lo) & (lane < lo + t), o_t.astype(BF16), attnt_s[c])
        ssmout_ref[0] = ht_s[...].T
        yn_s[rows, :] = _gated_norm(y_s[0:t, :], z_s[rows, :], normw_ref[...]).astype(BF16)

    @pl.when(i == nb + 1)
    def _():
        for c in range(n // BLK):
            r = slice(c * BLK, (c + 1) * BLK)
            h1_ref[r, :] = _out_proj(x_ref[r, :], attnt_s[c], yn_s[r, :], w_out_ref)


def _mix_sample(x, cache_k, cache_v, state_conv, state_ssm, p):
    nb, t, _ = x.shape
    n = nb * t
    assert BLK % t == 0 and n % BLK == 0
    const2 = lambda i: (0, 0)
    per = lambda i: (jnp.clip(i - 1, 0, nb - 1), 0, 0)
    smem = pl.BlockSpec(memory_space=pltpu.SMEM)
    in_specs = [
        smem, smem,
        pl.BlockSpec((n, D_MODEL), const2),
        pl.BlockSpec((1, WINDOW, D_KV), per),
        pl.BlockSpec((1, WINDOW, D_KV), per),
        pl.BlockSpec((1, CONV_WIDTH - 1, CONV_DIM), per),
        pl.BlockSpec((1, D_SSM, SSM_STATE), per),
        pl.BlockSpec((1, D_MODEL), const2),
        pl.BlockSpec((D_ATTN + D_KV, D_MODEL), const2),
        pl.BlockSpec((D_MODEL, D_NAT), const2),
        pl.BlockSpec((CONV_WIDTH, CONV_DIM), const2),
        pl.BlockSpec((1, CONV_DIM), const2),
        pl.BlockSpec((1, DT_PAD), const2),
        pl.BlockSpec((1, DT_PAD), const2),
        pl.BlockSpec((1, D_SSM), const2),
        pl.BlockSpec((D_MIX, D_MODEL), const2),
    ]
    out_shape = (
        jax.ShapeDtypeStruct((n, D_MODEL), F32),
        jax.ShapeDtypeStruct((n, D_KV), F32),
        jax.ShapeDtypeStruct((n, D_KV), F32),
        jax.ShapeDtypeStruct((nb, CONV_WIDTH - 1, CONV_DIM), F32),
        jax.ShapeDtypeStruct((nb, D_SSM, SSM_STATE), F32),
    )
    out_specs = (
        pl.BlockSpec((n, D_MODEL), const2),
        pl.BlockSpec((n, D_KV), const2),
        pl.BlockSpec((n, D_KV), const2),
        pl.BlockSpec((1, CONV_WIDTH - 1, CONV_DIM), per),
        pl.BlockSpec((1, D_SSM, SSM_STATE), per),
    )
    scratch = [
        pltpu.VMEM((n // BLK, D_ATTN, BLK), BF16),
        pltpu.VMEM((n // BLK, D_KV, BLK), F32),
        pltpu.VMEM((n, D_SSM), F32),
        pltpu.VMEM((n, CONV_DIM), F32),
        pltpu.VMEM((n, DT_PAD), F32),
        pltpu.VMEM((HIST + BLK, CONV_DIM), F32),
        pltpu.VMEM((BLK, D_SSM), F32),
        pltpu.VMEM((n // BLK, D_ATTN, BLK), BF16),
        pltpu.VMEM((n, D_SSM), BF16),
        pltpu.VMEM((SSM_STATE, D_SSM), F32),
    ]
    return pl.pallas_call(
        functools.partial(_mix_sample_kernel, nb=nb, t=t),
        grid=(nb + 2,),
        in_specs=in_specs,
        out_specs=out_specs,
        out_shape=out_shape,
        scratch_shapes=scratch,
        compiler_params=pltpu.CompilerParams(dimension_semantics=("arbitrary",), vmem_limit_bytes=VMEM_LIMIT),
        name="mix_sample",
    )(p["sinks"], p["d_skip"], x.reshape(n, D_MODEL), cache_k, cache_v, state_conv, state_ssm, p["ln1_g"],
      p["w_qvt"], p["w_nat"], p["conv_w"], p["conv_b"], p["dt_bias"], p["a_log"], p["ssm_norm_w"], p["w_out"])


FFN_ROWS = 256


def _ffn_kernel(h_ref, hx_ref, g2_ref, wg_ref, wu_ref, wd_ref, gf_ref, o_ref, ox_ref):
    _ffn_rows(h_ref, g2_ref, wg_ref, wu_ref, wd_ref, gf_ref, o_ref)

    @pl.when(pl.program_id(0) == pl.num_programs(0) - 1)
    def _():
        _ffn_rows(hx_ref, g2_ref, wg_ref, wu_ref, wd_ref, gf_ref, ox_ref)


def _ffn_rows(h_ref, g2_ref, wg_ref, wu_ref, wd_ref, gf_ref, o_ref):
    tm = h_ref.shape[0]
    chunks = [slice(r, min(r + FFN_ROWS, tm)) for r in range(0, tm, FFN_ROWS)]

    def up(rows):
        u = _rms(h_ref[rows, :], g2_ref[...]).astype(BF16)
        return _dot(u, wg_ref[...]), _dot(u, wu_ref[...])

    def down(rows, gate, upv):
        mid = (_silu(gate) * upv).astype(BF16)
        o_ref[rows, :] = _rms(h_ref[rows, :] + _dot(mid, wd_ref[...]), gf_ref[...])

    pending = up(chunks[0])
    for i, rows in enumerate(chunks):
        nxt = up(chunks[i + 1]) if i + 1 < len(chunks) else None
        down(rows, *pending)
        pending = nxt


def _ffn(h, hx, p, tm):
    n, nx = h.shape[0], hx.shape[0]
    d_ff = p["w_gate"].shape[1]
    const2 = lambda i: (0, 0)
    return pl.pallas_call(
        _ffn_kernel,
        grid=(n // tm,),
        in_specs=[
            pl.BlockSpec((tm, D_MODEL), lambda i: (i, 0)),
            pl.BlockSpec((nx, D_MODEL), const2),
            pl.BlockSpec((1, D_MODEL), const2),
            pl.BlockSpec((D_MODEL, d_ff), const2),
            pl.BlockSpec((D_MODEL, d_ff), const2),
            pl.BlockSpec((d_ff, D_MODEL), const2),
            pl.BlockSpec((1, D_MODEL), const2),
        ],
        out_specs=(pl.BlockSpec((tm, D_MODEL), lambda i: (i, 0)), pl.BlockSpec((nx, D_MODEL), const2)),
        out_shape=(jax.ShapeDtypeStruct((n, D_MODEL), F32), jax.ShapeDtypeStruct((nx, D_MODEL), F32)),
        compiler_params=pltpu.CompilerParams(dimension_semantics=("arbitrary",), vmem_limit_bytes=VMEM_LIMIT),
        name="ffn",
    )(h, hx, p["ln2_g"], p["w_gate"], p["w_up"], p["w_down"], p["final_g"])


def _prep_params(ln1_g, w_in, conv_w, conv_b, dt_bias, a_log, d_skip, ssm_norm_w, sinks, w_out, ln2_g, w_gate,
                 w_up, w_down, final_g):
    pad_h = lambda v: jnp.pad(v.reshape(1, SSM_HEADS), ((0, 0), (0, DT_PAD - SSM_HEADS)))
    o_k, o_v, o_z = D_ATTN, D_ATTN + D_KV, D_ATTN + 2 * D_KV
    w_q, w_k, w_v = w_in[:, :o_k], w_in[:, o_k:o_v], w_in[:, o_v:o_z]
    w_zx, w_dt = w_in[:, o_z:o_z + D_SSM + CONV_DIM], w_in[:, o_z + D_SSM + CONV_DIM:]
    w_dt = jnp.pad(w_dt, ((0, 0), (0, DT_PAD - w_dt.shape[1])))
    return {
        "ln1_g": ln1_g.reshape(1, D_MODEL),
        "w_qvt": jnp.concatenate([w_q, w_v], axis=1).T.astype(BF16),
        "w_nat": jnp.concatenate([w_k, w_dt, w_zx], axis=1).astype(BF16),
        "conv_w": conv_w,
        "conv_b": conv_b.reshape(1, CONV_DIM),
        "dt_bias": pad_h(dt_bias),
        "a_log": pad_h(a_log),
        "d_skip": d_skip,
        "ssm_norm_w": ssm_norm_w.reshape(1, D_SSM),
        "sinks": sinks,
        "w_out": w_out.astype(BF16),
        "ln2_g": ln2_g.reshape(1, D_MODEL),
        "w_gate": w_gate.astype(BF16),
        "w_up": w_up.astype(BF16),
        "w_down": w_down.astype(BF16),
        "final_g": final_g.reshape(1, D_MODEL),
    }


def _layer(x_prompt, x_sample, cache_k, cache_v, state_conv, state_ssm, p, ts, tm):
    b, s, _ = x_prompt.shape
    nb, t, _ = x_sample.shape
    h1p, kp, vp, cp, sp = _mix_prompt(x_prompt, p, ts)
    h1s, kn, vn, cn, sn = _mix_sample(x_sample, cache_k.reshape(nb, WINDOW, D_KV), cache_v.reshape(nb, WINDOW, D_KV),
                                      state_conv, state_ssm.reshape(nb, D_SSM, SSM_STATE), p)
    yp, ys = _ffn(h1p.reshape(b * s, D_MODEL), h1s, p, tm)
    yp, ys = yp.reshape(b, s, D_MODEL), ys.reshape(nb, t, D_MODEL)
    return (yp, ys,
            kp.reshape(b, WINDOW, N_KV_HEADS, HEAD_DIM), vp.reshape(b, WINDOW, N_KV_HEADS, HEAD_DIM), cp,
            sp.reshape(b, SSM_HEADS, SSM_HEAD_DIM, SSM_STATE),
            kn.reshape(nb, t, N_KV_HEADS, HEAD_DIM), vn.reshape(nb, t, N_KV_HEADS, HEAD_DIM), cn,
            sn.reshape(nb, SSM_HEADS, SSM_HEAD_DIM, SSM_STATE))


def kernel(x_prompt, x_sample, cache_k, cache_v, state_conv, state_ssm, ln1_g, w_in, conv_w, conv_b, dt_bias, a_log,
           d_skip, ssm_norm_w, sinks, w_out, ln2_g, w_gate, w_up, w_down, final_g):
    assert w_in.shape[0] == 1, "one layer"
    p = _prep_params(ln1_g[0], w_in[0], conv_w[0], conv_b[0], dt_bias[0], a_log[0], d_skip[0], ssm_norm_w[0],
                     sinks[0], w_out[0], ln2_g[0], w_gate[0], w_up[0], w_down[0], final_g)
    outs = _layer(x_prompt, x_sample, cache_k[0], cache_v[0], state_conv[0], state_ssm[0], p, ts=256, tm=1024)
    return tuple(o[None] if i >= 2 else o for i, o in enumerate(outs))
```

```python
import functools

import jax
import jax.numpy as jnp
from jax import lax
from jax.experimental import pallas as pl
from jax.experimental.pallas import tpu as pltpu

F32 = jnp.float32
BF16 = jnp.bfloat16

D_MODEL = 1024
CHUNK = 64
WINDOW = 128
HEAD_DIM = 64
N_HEADS = 16
N_KV_HEADS = 4
Q_PER_KV = N_HEADS // N_KV_HEADS
D_ATTN = N_HEADS * HEAD_DIM
D_KV = N_KV_HEADS * HEAD_DIM
SCALE = HEAD_DIM ** -0.5
SSM_HEADS = 16
SSM_HEAD_DIM = 64
SSM_GROUPS = 4
SSM_STATE = 128
D_SSM = SSM_HEADS * SSM_HEAD_DIM
D_BC = SSM_GROUPS * SSM_STATE
CONV_WIDTH = 4
CONV_DIM = D_SSM + 2 * D_BC
D_MIX = D_ATTN + D_SSM
EPS = 1e-6

LANES = 128
BLK = 2 * CHUNK
DT_PAD = LANES
N_K, N_DT, N_Z, N_X = 0, D_KV, D_KV + DT_PAD, D_KV + DT_PAD + D_SSM
D_NAT = N_X + CONV_DIM
HIST = 8
VMEM_LIMIT = 56 * 1024 * 1024


def _rms(x, g):
    ms = jnp.mean(x * x, axis=-1, keepdims=True)
    return (x * lax.rsqrt(ms + EPS)) * g


def _silu(x):
    h = 0.5 * x
    return h + h * jnp.tanh(h)


def _softplus(x):
    return jnp.maximum(x, 0.0) + jnp.log1p(jnp.exp(-jnp.abs(x)))


def _dot(a, b):
    return jnp.dot(a, b, preferred_element_type=F32)


def _dot_nt(a, b):
    return lax.dot_general(a, b, (((1,), (1,)), ((), ())), preferred_element_type=F32)


def _dot_tn(a, b):
    return lax.dot_general(a, b, (((0,), (0,)), ((), ())), preferred_element_type=F32)


def _left_half(shape):
    return lax.broadcasted_iota(jnp.int32, shape, len(shape) - 1) < HEAD_DIM


def _attn_steps(qt_blk, kwin, vtwin, mask_t, sinks_ref, emit, between=None):
    nq = Q_PER_KV * BLK
    mask4 = jnp.concatenate([mask_t] * Q_PER_KV, axis=1)
    zeros = jnp.zeros((HEAD_DIM, nq), BF16)

    def scores(kv):
        slab, half = kv // 2, kv % 2
        qk = jnp.concatenate([qt_blk[(kv * Q_PER_KV + g) * HEAD_DIM:(kv * Q_PER_KV + g + 1) * HEAD_DIM, :]
                              for g in range(Q_PER_KV)], axis=1)
        rhs = jnp.concatenate([qk, zeros] if half == 0 else [zeros, qk], axis=0)
        return _dot(kwin[:, slab * LANES:(slab + 1) * LANES], rhs)

    s_next = scores(0)
    for kv in range(N_KV_HEADS):
        s = s_next
        if kv + 1 < N_KV_HEADS:
            s_next = scores(kv + 1)
        if between is not None:
            between()
        s = jnp.where(mask4, s, -jnp.inf)
        sink = jnp.concatenate([jnp.full((1, BLK), sinks_ref[kv * Q_PER_KV + g], F32) for g in range(Q_PER_KV)],
                               axis=1)
        m = jnp.maximum(jnp.max(s, axis=0, keepdims=True), sink)
        e = jnp.exp(s - m)
        denom = jnp.sum(e, axis=0, keepdims=True) + jnp.exp(sink - m)
        p = (e * (1.0 / denom)).astype(BF16)
        o = _dot(vtwin[kv * HEAD_DIM:(kv + 1) * HEAD_DIM, :], p)
        emit(kv, jnp.concatenate([o[:, g * BLK:(g + 1) * BLK] for g in range(Q_PER_KV)], axis=0))
        yield


def _split3(x):
    hi = x.astype(BF16)
    r1 = x - hi.astype(F32)
    mid = r1.astype(BF16)
    lo = (r1 - mid.astype(F32)).astype(BF16)
    return hi, mid, lo


def _ssd_steps(xc, dt, a_row, dskip_ref, ht_ref, y_ref, r0, between=None):
    left = _left_half((BLK, LANES))
    row_i = lax.broadcasted_iota(jnp.int32, (BLK, BLK), 0)
    col_i = lax.broadcasted_iota(jnp.int32, (BLK, BLK), 1)
    tri = col_i <= row_i
    tri_b = tri.astype(BF16)
    da = dt * a_row
    hi, mid, lo = _split3(da)
    parts = _dot(tri_b, jnp.concatenate([hi, mid, lo], axis=1))
    cs = (parts[:, 0:LANES] + parts[:, LANES:2 * LANES]) + parts[:, 2 * LANES:]
    cs_t = cs.T
    if between is not None:
        between()

    def colb(a, h):
        return jnp.broadcast_to(a[:, h:h + 1], (BLK, LANES))

    def group_products(g):
        bg = xc((D_SSM + g * SSM_STATE) // LANES)
        cg = xc((D_SSM + D_BC + g * SSM_STATE) // LANES).astype(BF16)
        cb = _dot_nt(cg, bg.astype(BF16))
        ht_g = ht_ref[:, g * 2 * LANES:(g + 1) * 2 * LANES]
        yoff = _dot(cg, ht_g.astype(BF16))
        return bg, cb, ht_g, yoff

    nxt = group_products(0)
    for g in range(SSM_GROUPS):
        bg, cb, ht_g, yoff = nxt
        gs = slice(g * 2 * LANES, (g + 1) * 2 * LANES)
        if g + 1 < SSM_GROUPS:
            nxt = group_products(g + 1)
        if between is not None:
            between()
        xdec, cdec, late = [], [], []
        for hp in range(2):
            h0 = g * 4 + 2 * hp
            h1 = h0 + 1
            cols = slice((g * 2 + hp) * LANES, (g * 2 + hp + 1) * LANES)
            xs2 = xc(g * 2 + hp)
            csb = [colb(cs, h0), colb(cs, h1)]
            cs2 = jnp.where(left, csb[0], csb[1])
            ecs2 = jnp.exp(cs2)
            dec2 = jnp.exp(cs2[BLK - 1:BLK, :] - cs2)
            dt2 = jnp.where(left, colb(dt, h0), colb(dt, h1))
            xd2 = xs2 * dt2
            ms = []
            for idx, h in enumerate((h0, h1)):
                seg = csb[idx] - jnp.broadcast_to(cs_t[h:h + 1, :], (BLK, BLK))
                ms.append((cb * jnp.exp(jnp.where(tri, seg, -jnp.inf))).astype(BF16))
            xd2b = xd2.astype(BF16)
            zero = jnp.zeros_like(xd2b)
            xd_diag = jnp.concatenate([jnp.where(left, xd2b, zero), jnp.where(left, zero, xd2b)], axis=0)
            yd = _dot(jnp.concatenate(ms, axis=1), xd_diag)
            dsk2 = jnp.where(left, dskip_ref[h0], dskip_ref[h1])
            late.append((cols, yd, yoff[:, hp * LANES:(hp + 1) * LANES] * ecs2, xs2 * dsk2))
            xdec.append((xd2 * dec2).astype(BF16))
            cdec.append(ecs2[BLK - 1:BLK, :])
        xdec = jnp.concatenate(xdec, axis=1)
        cdec = jnp.concatenate(cdec, axis=1)
        ht_ref[:, gs] = cdec * ht_g + _dot(bg.T.astype(BF16), xdec)
        for cols, yd, off, skip in late:
            y_ref[pl.ds(r0, BLK), cols] = (yd + off) + skip
        yield


def _gated_norm(y, z, w):
    g = y * _silu(z)
    parts = []
    width = D_SSM // SSM_GROUPS
    for i in range(SSM_GROUPS):
        gg = g[:, i * width:(i + 1) * width]
        parts.append(gg * lax.rsqrt(jnp.mean(gg * gg, axis=-1, keepdims=True) + EPS))
    return jnp.concatenate(parts, axis=1) * w


def _conv(xbc_ref, rows, conv_w_ref, conv_b_ref):
    acc = conv_b_ref[...]
    for i in range(CONV_WIDTH):
        off = HIST - (CONV_WIDTH - 1) + i
        acc = acc + xbc_ref[off:off + rows, :] * conv_w_ref[i:i + 1, :]
    return _silu(acc)


CONV_PHASES = 4


def _conv_slab(xbc_ref, xc_ref, c, rows, conv_w_ref, conv_b_ref):
    n = rows // CONV_PHASES
    cols = slice(c * LANES, (c + 1) * LANES)
    taps = [conv_w_ref[i:i + 1, cols] for i in range(CONV_WIDTH)]
    bias = conv_b_ref[:, cols]
    lo = -(CONV_WIDTH - 1)
    shifted = [xbc_ref[c, pl.ds(HIST + s, n, stride=CONV_PHASES), :] for s in range(lo, CONV_PHASES)]
    for r in range(CONV_PHASES):
        acc = bias
        for i in range(CONV_WIDTH):
            acc = acc + shifted[r + i] * taps[i]
        xc_ref[c, pl.ds(r, n, stride=CONV_PHASES), :] = _silu(acc)


def _nat_proj(u, w_nat_ref, lo, hi):
    return _dot(u, w_nat_ref[:, lo:hi])


def _out_proj(x, attn_t, yn, w_out_ref):
    return x + (_dot_tn(attn_t, w_out_ref[0:D_ATTN, :]) + _dot(yn, w_out_ref[D_ATTN:, :]))


def _mix_prompt_kernel(sinks_ref, dskip_ref, xp_ref, g1_ref, w_qvt_ref, w_nat_ref, conv_w_ref, conv_b_ref,
                       dtb_ref, alog_ref, normw_ref, w_out_ref,
                       h1_ref, kout_ref, vout_ref, convout_ref, ssmout_ref,
                       qt_s, kh_s, vth_s, z_s, xbc_s, xc_s, dt_s, y_s, attnt_s, yn_s, xr_s, ht_s, *, ts, nj):
    t = pl.program_id(0)
    sp = t % 2
    sc = 1 - sp
    jp = t % nj
    jc = (t + nj - 1) % nj

    @pl.when(t == 0)
    def _():
        for ref in (qt_s, kh_s, vth_s, z_s, xbc_s, xc_s, dt_s, attnt_s, yn_s, xr_s):
            ref[...] = jnp.zeros(ref.shape, ref.dtype)

    @pl.when(jc == 0)
    def _():
        ht_s[...] = jnp.zeros(ht_s.shape, F32)

    def project():
        fresh = jp == 0
        kh_s[sp, 0:WINDOW, :] = jnp.where(fresh, 0.0, kh_s[sc, ts:ts + WINDOW, :])
        vth_s[sp, :, 0:WINDOW] = jnp.where(fresh, 0.0, vth_s[sc, :, ts:ts + WINDOW])
        xbc_s[sp, :, 0:HIST, :] = jnp.where(fresh, 0.0, xbc_s[sc, :, ts:ts + HIST, :])
        x = xp_ref[0]
        xr_s[t % 3] = x
        x_res = xr_s.at[(t + 1) % 3]
        u = _rms(x, g1_ref[...]).astype(BF16)
        attn_t = attnt_s[sp]
        yn = yn_s[sp]
        n_piece = 4
        slabs = CONV_DIM // LANES // n_piece

        def xbc(c):
            w = CONV_DIM // n_piece
            piece = _nat_proj(u, w_nat_ref, N_X + c * w, N_X + (c + 1) * w)
            for k in range(slabs):
                xbc_s[sp, c * slabs + k, HIST:HIST + ts, :] = piece[:, k * LANES:(k + 1) * LANES]

        def conv(c):
            for k in range(slabs):
                _conv_slab(xbc_s.at[sp], xc_s.at[sp], c * slabs + k, ts, conv_w_ref, conv_b_ref)

        def q(c):
            w = D_ATTN // n_piece
            qt_s[sp, c * w:(c + 1) * w, :] = (_dot_nt(w_qvt_ref[c * w:(c + 1) * w, :], u) * SCALE).astype(BF16)

        def z(c):
            w = D_SSM // n_piece
            z_s[sp, :, c * w:(c + 1) * w] = _nat_proj(u, w_nat_ref, N_Z + c * w, N_Z + (c + 1) * w)

        def out(c):
            w = D_MODEL // n_piece
            cols = slice(c * w, (c + 1) * w)
            h1_ref[0, :, cols] = x_res[:, cols] + (_dot_tn(attn_t, w_out_ref[0:D_ATTN, cols])
                                                       + _dot(yn, w_out_ref[D_ATTN:, cols]))

        def v():
            vth_s[sp, :, WINDOW:WINDOW + ts] = _dot_nt(w_qvt_ref[D_ATTN:, :], u)

        def k():
            kh_s[sp, WINDOW:WINDOW + ts, :] = _nat_proj(u, w_nat_ref, N_K, N_DT)

        def dt():
            dt_s[sp] = _softplus(_nat_proj(u, w_nat_ref, N_DT, N_Z) + dtb_ref[...])

        placement = [
            [(xbc, 0)], [(xbc, 1), (conv, 0)], [(xbc, 2), (conv, 1)], [(xbc, 3), (conv, 2)],
            [(q, 0)], [(q, 1)], [(q, 2)], [(q, 3)], [(z, 0)],
            [(out, 0), (conv, 3)], [(out, 1)], [(out, 2)], [(out, 3)],
            [(z, 1)], [(z, 2)], [(z, 3)], [(v,)], [(k,)],
            [(dt,)],
        ]
        for group in placement:
            yield
            for fn, *args in group:
                fn(*args)

    pieces = project()
    advance = lambda: next(pieces, None)
    advance()

    a_row = -jnp.exp(alog_ref[...])
    key = lax.broadcasted_iota(jnp.int32, (2 * BLK, BLK), 0)
    qc = lax.broadcasted_iota(jnp.int32, (2 * BLK, BLK), 1) // CHUNK
    band = (key >= qc * CHUNK) & (key < (qc + 3) * CHUNK)
    for p in range(ts // BLK):
        r0 = p * BLK
        first = jnp.where(jc == 0, WINDOW, 0) if p == 0 else 0
        mask_t = band & (key >= first)

        def emit(kv, o, r0=r0):
            attnt_s[sc, kv * Q_PER_KV * HEAD_DIM:(kv + 1) * Q_PER_KV * HEAD_DIM, r0:r0 + BLK] = o.astype(BF16)

        attn = _attn_steps(qt_s[sc, :, r0:r0 + BLK], kh_s[sc, r0:r0 + 2 * BLK, :].astype(BF16),
                           vth_s[sc, :, r0:r0 + 2 * BLK].astype(BF16), mask_t, sinks_ref, emit, advance)
        ssd = _ssd_steps(lambda i, r0=r0: xc_s[sc, i, r0:r0 + BLK, :], dt_s[sc, r0:r0 + BLK, :], a_row, dskip_ref,
                         ht_s, y_s, r0, advance)
        for _ in attn:
            pass
        for _ in ssd:
            pass
    for p in range(ts // BLK):
        rows = slice(p * BLK, (p + 1) * BLK)
        advance()
        yn_s[sc, rows, :] = _gated_norm(y_s[rows, :], z_s[sc, rows, :], normw_ref[...]).astype(BF16)
    for _ in pieces:
        pass

    @pl.when(jc == nj - 1)
    def _():
        kout_ref[0] = kh_s[sc, ts:ts + WINDOW, :]
        vout_ref[0] = vth_s[sc, :, ts:ts + WINDOW].T
        for c in range(CONV_DIM // LANES):
            convout_ref[0, :, c * LANES:(c + 1) * LANES] = xbc_s[sc, c, HIST + ts - (CONV_WIDTH - 1):HIST + ts, :]
        ssmout_ref[0] = ht_s[...].T


def _mix_prompt(x, p, ts):
    b, s, _ = x.shape
    nj = s // ts
    nt = b * nj
    const2 = lambda t: (0, 0)
    prev2 = lambda t: (jnp.maximum(t - 2, 0), 0, 0)
    per_stream = lambda t: (jnp.clip(t - 1, 0, nt - 1) // nj, 0, 0)
    smem = pl.BlockSpec(memory_space=pltpu.SMEM)
    in_specs = [
        smem, smem,
        pl.BlockSpec((1, ts, D_MODEL), lambda t: (jnp.minimum(t, nt - 1), 0, 0)),
        pl.BlockSpec((1, D_MODEL), const2),
        pl.BlockSpec((D_ATTN + D_KV, D_MODEL), const2),
        pl.BlockSpec((D_MODEL, D_NAT), const2),
        pl.BlockSpec((CONV_WIDTH, CONV_DIM), const2),
        pl.BlockSpec((1, CONV_DIM), const2),
        pl.BlockSpec((1, DT_PAD), const2),
        pl.BlockSpec((1, DT_PAD), const2),
        pl.BlockSpec((1, D_SSM), const2),
        pl.BlockSpec((D_MIX, D_MODEL), const2),
    ]
    out_shape = (
        jax.ShapeDtypeStruct((nt, ts, D_MODEL), F32),
        jax.ShapeDtypeStruct((b, WINDOW, D_KV), F32),
        jax.ShapeDtypeStruct((b, WINDOW, D_KV), F32),
        jax.ShapeDtypeStruct((b, CONV_WIDTH - 1, CONV_DIM), F32),
        jax.ShapeDtypeStruct((b, D_SSM, SSM_STATE), F32),
    )
    out_specs = (
        pl.BlockSpec((1, ts, D_MODEL), prev2),
        pl.BlockSpec((1, WINDOW, D_KV), per_stream),
        pl.BlockSpec((1, WINDOW, D_KV), per_stream),
        pl.BlockSpec((1, CONV_WIDTH - 1, CONV_DIM), per_stream),
        pl.BlockSpec((1, D_SSM, SSM_STATE), per_stream),
    )
    scratch = [
        pltpu.VMEM((2, D_ATTN, ts), BF16),
        pltpu.VMEM((2, WINDOW + ts, D_KV), F32),
        pltpu.VMEM((2, D_KV, WINDOW + ts), F32),
        pltpu.VMEM((2, ts, D_SSM), F32),
        pltpu.VMEM((2, CONV_DIM // LANES, HIST + ts, LANES), F32),
        pltpu.VMEM((2, CONV_DIM // LANES, ts, LANES), F32),
        pltpu.VMEM((2, ts, DT_PAD), F32),
        pltpu.VMEM((ts, D_SSM), F32),
        pltpu.VMEM((2, D_ATTN, ts), BF16),
        pltpu.VMEM((2, ts, D_SSM), BF16),
        pltpu.VMEM((3, ts, D_MODEL), F32),
        pltpu.VMEM((SSM_STATE, D_SSM), F32),
    ]
    x3 = x.reshape(nt, ts, D_MODEL)
    outs = pl.pallas_call(
        functools.partial(_mix_prompt_kernel, ts=ts, nj=nj),
        grid=(nt + 2,),
        in_specs=in_specs,
        out_specs=out_specs,
        out_shape=out_shape,
        scratch_shapes=scratch,
        compiler_params=pltpu.CompilerParams(dimension_semantics=("arbitrary",), vmem_limit_bytes=VMEM_LIMIT),
        name="mix_prompt",
    )(p["sinks"], p["d_skip"], x3, p["ln1_g"], p["w_qvt"], p["w_nat"], p["conv_w"], p["conv_b"], p["dt_bias"],
      p["a_log"], p["ssm_norm_w"], p["w_out"])
    return (outs[0].reshape(b, s, D_MODEL),) + tuple(outs[1:])


def _mix_sample_kernel(sinks_ref, dskip_ref, x_ref, ck_ref, cv_ref, sconv_ref, sssm_ref, g1_ref, w_qvt_ref,
                       w_nat_ref, conv_w_ref, conv_b_ref, dtb_ref, alog_ref, normw_ref, w_out_ref,
                       h1_ref, kout_ref, vout_ref, convout_ref, ssmout_ref,
                       qt_s, vt_s, z_s, xbcall_s, dt_s, xbc_s, y_s, attnt_s, yn_s, ht_s, *, nb, t):
    i = pl.program_id(0)
    n = nb * t
    per_tile = BLK // t

    @pl.when(i == 0)
    def _():
        u = _rms(x_ref[...], g1_ref[...]).astype(BF16)
        qt = (_dot_nt(w_qvt_ref[0:D_ATTN, :], u) * SCALE).astype(BF16)
        vt = _dot_nt(w_qvt_ref[D_ATTN:, :], u)
        for c in range(n // BLK):
            qt_s[c] = qt[:, c * BLK:(c + 1) * BLK]
            vt_s[c] = vt[:, c * BLK:(c + 1) * BLK]
        vout_ref[...] = vt.T
        kout_ref[...] = _nat_proj(u, w_nat_ref, N_K, N_DT)
        z_s[...] = _nat_proj(u, w_nat_ref, N_Z, N_X)
        xbcall_s[...] = _nat_proj(u, w_nat_ref, N_X, D_NAT)
        dt_s[...] = _softplus(_nat_proj(u, w_nat_ref, N_DT, N_Z) + dtb_ref[...])
        attnt_s[...] = jnp.zeros(attnt_s.shape, BF16)
        xbc_s[...] = jnp.zeros(xbc_s.shape, F32)

    @pl.when((i >= 1) & (i <= nb))
    def _():
        b = i - 1
        rows = pl.ds(pl.multiple_of(b * t, t), t)
        c = b // per_tile
        lo = (b % per_tile) * t
        kwin = jnp.concatenate([kout_ref[pl.ds(pl.multiple_of(c * BLK, BLK), BLK), :], ck_ref[0]], axis=0)
        vtwin = jnp.concatenate([vt_s[c], cv_ref[0].T], axis=1)
        key = lax.broadcasted_iota(jnp.int32, (2 * BLK, BLK), 0)
        mask_t = ((key >= lo) & (key < lo + t)) | (key >= BLK)
        xbc_s[HIST - (CONV_WIDTH - 1):HIST, :] = sconv_ref[0]
        xbc_s[HIST:HIST + t, :] = xbcall_s[rows, :]
        convout_ref[0] = xbc_s[HIST + t - (CONV_WIDTH - 1):HIST + t, :]
        live = lax.broadcasted_iota(jnp.int32, (BLK, 1), 0) < t
        xc = jnp.where(live, _conv(xbc_s, BLK, conv_w_ref, conv_b_ref), 0.0)
        dt = jnp.concatenate([dt_s[rows, :], jnp.zeros((BLK - t, DT_PAD), F32)], axis=0)
        ht_s[...] = sssm_ref[0].T
        ssd = _ssd_steps(lambda i: xc[:, i * LANES:(i + 1) * LANES], dt, -jnp.exp(alog_ref[...]), dskip_ref, ht_s,
                         y_s, 0)

        outs = [None] * N_KV_HEADS

        def emit(kv, o):
            outs[kv] = o

        for _ in _attn_steps(qt_s[c], kwin.astype(BF16), vtwin.astype(BF16), mask_t, sinks_ref, emit,
                             lambda: next(ssd, None)):
            pass
        for _ in ssd:
            pass
        o_t = jnp.concatenate(outs, axis=0)
        lane = lax.broadcasted_iota(jnp.int32, (D_ATTN, BLK), 1)
        attnt_s[c] = jnp.where((lane >= lo) & (lane < lo + t), o_t.astype(BF16), attnt_s[c])
        ssmout_ref[0] = ht_s[...].T
        yn_s[rows, :] = _gated_norm(y_s[0:t, :], z_s[rows, :], normw_ref[...]).astype(BF16)

    @pl.when(i == nb + 1)
    def _():
        for c in range(n // BLK):
            r = slice(c * BLK, (c + 1) * BLK)
            h1_ref[r, :] = _out_proj(x_ref[r, :], attnt_s[c], yn_s[r, :], w_out_ref)


def _mix_sample(x, cache_k, cache_v, state_conv, state_ssm, p):
    nb, t, _ = x.shape
    n = nb * t
    assert BLK % t == 0 and n % BLK == 0
    const2 = lambda i: (0, 0)
    per = lambda i: (jnp.clip(i - 1, 0, nb - 1), 0, 0)
    smem = pl.BlockSpec(memory_space=pltpu.SMEM)
    in_specs = [
        smem, smem,
        pl.BlockSpec((n, D_MODEL), const2),
        pl.BlockSpec((1, WINDOW, D_KV), per),
        pl.BlockSpec((1, WINDOW, D_KV), per),
        pl.BlockSpec((1, CONV_WIDTH - 1, CONV_DIM), per),
        pl.BlockSpec((1, D_SSM, SSM_STATE), per),
        pl.BlockSpec((1, D_MODEL), const2),
        pl.BlockSpec((D_ATTN + D_KV, D_MODEL), const2),
        pl.BlockSpec((D_MODEL, D_NAT), const2),
        pl.BlockSpec((CONV_WIDTH, CONV_DIM), const2),
        pl.BlockSpec((1, CONV_DIM), const2),
        pl.BlockSpec((1, DT_PAD), const2),
        pl.BlockSpec((1, DT_PAD), const2),
        pl.BlockSpec((1, D_SSM), const2),
        pl.BlockSpec((D_MIX, D_MODEL), const2),
    ]
    out_shape = (
        jax.ShapeDtypeStruct((n, D_MODEL), F32),
        jax.ShapeDtypeStruct((n, D_KV), F32),
        jax.ShapeDtypeStruct((n, D_KV), F32),
        jax.ShapeDtypeStruct((nb, CONV_WIDTH - 1, CONV_DIM), F32),
        jax.ShapeDtypeStruct((nb, D_SSM, SSM_STATE), F32),
    )
    out_specs = (
        pl.BlockSpec((n, D_MODEL), const2),
        pl.BlockSpec((n, D_KV), const2),
        pl.BlockSpec((n, D_KV), const2),
        pl.BlockSpec((1, CONV_WIDTH - 1, CONV_DIM), per),
        pl.BlockSpec((1, D_SSM, SSM_STATE), per),
    )
    scratch = [
        pltpu.VMEM((n // BLK, D_ATTN, BLK), BF16),
        pltpu.VMEM((n // BLK, D_KV, BLK), F32),
        pltpu.VMEM((n, D_SSM), F32),
        pltpu.VMEM((n, CONV_DIM), F32),
        pltpu.VMEM((n, DT_PAD), F32),
        pltpu.VMEM((HIST + BLK, CONV_DIM), F32),
        pltpu.VMEM((BLK, D_SSM), F32),
        pltpu.VMEM((n // BLK, D_ATTN, BLK), BF16),
        pltpu.VMEM((n, D_SSM), BF16),
        pltpu.VMEM((SSM_STATE, D_SSM), F32),
    ]
    return pl.pallas_call(
        functools.partial(_mix_sample_kernel, nb=nb, t=t),
        grid=(nb + 2,),
        in_specs=in_specs,
        out_specs=out_specs,
        out_shape=out_shape,
        scratch_shapes=scratch,
        compiler_params=pltpu.CompilerParams(dimension_semantics=("arbitrary",), vmem_limit_bytes=VMEM_LIMIT),
        name="mix_sample",
    )(p["sinks"], p["d_skip"], x.reshape(n, D_MODEL), cache_k, cache_v, state_conv, state_ssm, p["ln1_g"],
      p["w_qvt"], p["w_nat"], p["conv_w"], p["conv_b"], p["dt_bias"], p["a_log"], p["ssm_norm_w"], p["w_out"])


FFN_ROWS = 256


def _ffn_kernel(h_ref, hx_ref, g2_ref, wg_ref, wu_ref, wd_ref, gf_ref, o_ref, ox_ref):
    _ffn_rows(h_ref, g2_ref, wg_ref, wu_ref, wd_ref, gf_ref, o_ref)

    @pl.when(pl.program_id(0) == pl.num_programs(0) - 1)
    def _():
        _ffn_rows(hx_ref, g2_ref, wg_ref, wu_ref, wd_ref, gf_ref, ox_ref)


def _ffn_rows(h_ref, g2_ref, wg_ref, wu_ref, wd_ref, gf_ref, o_ref):
    tm = h_ref.shape[0]
    chunks = [slice(r, min(r + FFN_ROWS, tm)) for r in range(0, tm, FFN_ROWS)]

    def up(rows):
        u = _rms(h_ref[rows, :], g2_ref[...]).astype(BF16)
        return _dot(u, wg_ref[...]), _dot(u, wu_ref[...])

    def down(rows, gate, upv):
        mid = (_silu(gate) * upv).astype(BF16)
        o_ref[rows, :] = _rms(h_ref[rows, :] + _dot(mid, wd_ref[...]), gf_ref[...])

    pending = up(chunks[0])
    for i, rows in enumerate(chunks):
        nxt = up(chunks[i + 1]) if i + 1 < len(chunks) else None
        down(rows, *pending)
        pending = nxt


def _ffn(h, hx, p, tm):
    n, nx = h.shape[0], hx.shape[0]
    d_ff = p["w_gate"].shape[1]
    const2 = lambda i: (0, 0)
    return pl.pallas_call(
        _ffn_kernel,
        grid=(n // tm,),
        in_specs=[
            pl.BlockSpec((tm, D_MODEL), lambda i: (i, 0)),
            pl.BlockSpec((nx, D_MODEL), const2),
            pl.BlockSpec((1, D_MODEL), const2),
            pl.BlockSpec((D_MODEL, d_ff), const2),
            pl.BlockSpec((D_MODEL, d_ff), const2),
            pl.BlockSpec((d_ff, D_MODEL), const2),
            pl.BlockSpec((1, D_MODEL), const2),
        ],
        out_specs=(pl.BlockSpec((tm, D_MODEL), lambda i: (i, 0)), pl.BlockSpec((nx, D_MODEL), const2)),
        out_shape=(jax.ShapeDtypeStruct((n, D_MODEL), F32), jax.ShapeDtypeStruct((nx, D_MODEL), F32)),
        compiler_params=pltpu.CompilerParams(dimension_semantics=("arbitrary",), vmem_limit_bytes=VMEM_LIMIT),
        name="ffn",
    )(h, hx, p["ln2_g"], p["w_gate"], p["w_up"], p["w_down"], p["final_g"])


def _prep_params(ln1_g, w_in, conv_w, conv_b, dt_bias, a_log, d_skip, ssm_norm_w, sinks, w_out, ln2_g, w_gate,
                 w_up, w_down, final_g):
    pad_h = lambda v: jnp.pad(v.reshape(1, SSM_HEADS), ((0, 0), (0, DT_PAD - SSM_HEADS)))
    o_k, o_v, o_z = D_ATTN, D_ATTN + D_KV, D_ATTN + 2 * D_KV
    w_q, w_k, w_v = w_in[:, :o_k], w_in[:, o_k:o_v], w_in[:, o_v:o_z]
    w_zx, w_dt = w_in[:, o_z:o_z + D_SSM + CONV_DIM], w_in[:, o_z + D_SSM + CONV_DIM:]
    w_dt = jnp.pad(w_dt, ((0, 0), (0, DT_PAD - w_dt.shape[1])))
    return {
        "ln1_g": ln1_g.reshape(1, D_MODEL),
        "w_qvt": jnp.concatenate([w_q, w_v], axis=1).T.astype(BF16),
        "w_nat": jnp.concatenate([w_k, w_dt, w_zx], axis=1).astype(BF16),
        "conv_w": conv_w,
        "conv_b": conv_b.reshape(1, CONV_DIM),
        "dt_bias": pad_h(dt_bias),
        "a_log": pad_h(a_log),
        "d_skip": d_skip,
        "ssm_norm_w": ssm_norm_w.reshape(1, D_SSM),
        "sinks": sinks,
        "w_out": w_out.astype(BF16),
        "ln2_g": ln2_g.reshape(1, D_MODEL),
        "w_gate": w_gate.astype(BF16),
        "w_up": w_up.astype(BF16),
        "w_down": w_down.astype(BF16),
        "final_g": final_g.reshape(1, D_MODEL),
    }


def _layer(x_prompt, x_sample, cache_k, cache_v, state_conv, state_ssm, p, ts, tm):
    b, s, _ = x_prompt.shape
    nb, t, _ = x_sample.shape
    h1p, kp, vp, cp, sp = _mix_prompt(x_prompt, p, ts)
    h1s, kn, vn, cn, sn = _mix_sample(x_sample, cache_k.reshape(nb, WINDOW, D_KV), cache_v.reshape(nb, WINDOW, D_KV),
                                      state_conv, state_ssm.reshape(nb, D_SSM, SSM_STATE), p)
    yp, ys = _ffn(h1p.reshape(b * s, D_MODEL), h1s, p, tm)
    yp, ys = yp.reshape(b, s, D_MODEL), ys.reshape(nb, t, D_MODEL)
    return (yp, ys,
            kp.reshape(b, WINDOW, N_KV_HEADS, HEAD_DIM), vp.reshape(b, WINDOW, N_KV_HEADS, HEAD_DIM), cp,
            sp.reshape(b, SSM_HEADS, SSM_HEAD_DIM, SSM_STATE),
            kn.reshape(nb, t, N_KV_HEADS, HEAD_DIM), vn.reshape(nb, t, N_KV_HEADS, HEAD_DIM), cn,
            sn.reshape(nb, SSM_HEADS, SSM_HEAD_DIM, SSM_STATE))


def kernel(x_prompt, x_sample, cache_k, cache_v, state_conv, state_ssm, ln1_g, w_in, conv_w, conv_b, dt_bias, a_log,
           d_skip, ssm_norm_w, sinks, w_out, ln2_g, w_gate, w_up, w_down, final_g):
    assert w_in.shape[0] == 1, "one layer"
    p = _prep_params(ln1_g[0], w_in[0], conv_w[0], conv_b[0], dt_bias[0], a_log[0], d_skip[0], ssm_norm_w[0],
                     sinks[0], w_out[0], ln2_g[0], w_gate[0], w_up[0], w_down[0], final_g)
    outs = _layer(x_prompt, x_sample, cache_k[0], cache_v[0], state_conv[0], state_ssm[0], p, ts=256, tm=1024)
    return tuple(o[None] if i >= 2 else o for i, o in enumerate(outs))
```

```python
import functools

import jax
import jax.numpy as jnp
from jax import lax
from jax.experimental import pallas as pl
from jax.experimental.pallas import tpu as pltpu

F32 = jnp.float32
BF16 = jnp.bfloat16

D_MODEL = 1024
CHUNK = 64
WINDOW = 128
HEAD_DIM = 64
N_HEADS = 16
N_KV_HEADS = 4
Q_PER_KV = N_HEADS // N_KV_HEADS
D_ATTN = N_HEADS * HEAD_DIM
D_KV = N_KV_HEADS * HEAD_DIM
SCALE = HEAD_DIM ** -0.5
SSM_HEADS = 16
SSM_HEAD_DIM = 64
SSM_GROUPS = 4
SSM_STATE = 128
D_SSM = SSM_HEADS * SSM_HEAD_DIM
D_BC = SSM_GROUPS * SSM_STATE
CONV_WIDTH = 4
CONV_DIM = D_SSM + 2 * D_BC
D_MIX = D_ATTN + D_SSM
EPS = 1e-6

LANES = 128
BLK = 2 * CHUNK
DT_PAD = LANES
N_K, N_DT, N_Z, N_X = 0, D_KV, D_KV + DT_PAD, D_KV + DT_PAD + D_SSM
D_NAT = N_X + CONV_DIM
HIST = 8
VMEM_LIMIT = 56 * 1024 * 1024


def _rms(x, g):
    ms = jnp.mean(x * x, axis=-1, keepdims=True)
    return (x * lax.rsqrt(ms + EPS)) * g


def _silu(x):
    h = 0.5 * x
    return h + h * jnp.tanh(h)


def _softplus(x):
    return jnp.maximum(x, 0.0) + jnp.log1p(jnp.exp(-jnp.abs(x)))


def _dot(a, b):
    return jnp.dot(a, b, preferred_element_type=F32)


def _dot_nt(a, b):
    return lax.dot_general(a, b, (((1,), (1,)), ((), ())), preferred_element_type=F32)


def _dot_tn(a, b):
    return lax.dot_general(a, b, (((0,), (0,)), ((), ())), preferred_element_type=F32)


def _left_half(shape):
    return lax.broadcasted_iota(jnp.int32, shape, len(shape) - 1) < HEAD_DIM


def _attn_steps(qt_blk, kwin, vtwin, mask_t, sinks_ref, emit, between=None):
    nq = Q_PER_KV * BLK
    mask4 = jnp.concatenate([mask_t] * Q_PER_KV, axis=1)
    zeros = jnp.zeros((HEAD_DIM, nq), BF16)

    def scores(kv):
        slab, half = kv // 2, kv % 2
        qk = jnp.concatenate([qt_blk[(kv * Q_PER_KV + g) * HEAD_DIM:(kv * Q_PER_KV + g + 1) * HEAD_DIM, :]
                              for g in range(Q_PER_KV)], axis=1)
        rhs = jnp.concatenate([qk, zeros] if half == 0 else [zeros, qk], axis=0)
        return _dot(kwin[:, slab * LANES:(slab + 1) * LANES], rhs)

    s_next = scores(0)
    for kv in range(N_KV_HEADS):
        s = s_next
        if kv + 1 < N_KV_HEADS:
            s_next = scores(kv + 1)
        if between is not None:
            between()
        s = jnp.where(mask4, s, -jnp.inf)
        sink = jnp.concatenate([jnp.full((1, BLK), sinks_ref[kv * Q_PER_KV + g], F32) for g in range(Q_PER_KV)],
                               axis=1)
        m = jnp.maximum(jnp.max(s, axis=0, keepdims=True), sink)
        e = jnp.exp(s - m)
        denom = jnp.sum(e, axis=0, keepdims=True) + jnp.exp(sink - m)
        p = (e * (1.0 / denom)).astype(BF16)
        o = _dot(vtwin[kv * HEAD_DIM:(kv + 1) * HEAD_DIM, :], p)
        emit(kv, jnp.concatenate([o[:, g * BLK:(g + 1) * BLK] for g in range(Q_PER_KV)], axis=0))
        yield


def _split3(x):
    hi = x.astype(BF16)
    r1 = x - hi.astype(F32)
    mid = r1.astype(BF16)
    lo = (r1 - mid.astype(F32)).astype(BF16)
    return hi, mid, lo


def _ssd_steps(xc, dt, a_row, dskip_ref, ht_ref, y_ref, r0, between=None):
    left = _left_half((BLK, LANES))
    row_i = lax.broadcasted_iota(jnp.int32, (BLK, BLK), 0)
    col_i = lax.broadcasted_iota(jnp.int32, (BLK, BLK), 1)
    tri = col_i <= row_i
    tri_b = tri.astype(BF16)
    da = dt * a_row
    hi, mid, lo = _split3(da)
    parts = _dot(tri_b, jnp.concatenate([hi, mid, lo], axis=1))
    cs = (parts[:, 0:LANES] + parts[:, LANES:2 * LANES]) + parts[:, 2 * LANES:]
    cs_t = cs.T
    if between is not None:
        between()

    def colb(a, h):
        return jnp.broadcast_to(a[:, h:h + 1], (BLK, LANES))

    def group_products(g):
        bg = xc((D_SSM + g * SSM_STATE) // LANES)
        cg = xc((D_SSM + D_BC + g * SSM_STATE) // LANES).astype(BF16)
        cb = _dot_nt(cg, bg.astype(BF16))
        ht_g = ht_ref[:, g * 2 * LANES:(g + 1) * 2 * LANES]
        yoff = _dot(cg, ht_g.astype(BF16))
        return bg, cb, ht_g, yoff

    nxt = group_products(0)
    for g in range(SSM_GROUPS):
        bg, cb, ht_g, yoff = nxt
        gs = slice(g * 2 * LANES, (g + 1) * 2 * LANES)
        if g + 1 < SSM_GROUPS:
            nxt = group_products(g + 1)
        if between is not None:
            between()
        xdec, cdec, late = [], [], []
        for hp in range(2):
            h0 = g * 4 + 2 * hp
            h1 = h0 + 1
            cols = slice((g * 2 + hp) * LANES, (g * 2 + hp + 1) * LANES)
            xs2 = xc(g * 2 + hp)
            csb = [colb(cs, h0), colb(cs, h1)]
            cs2 = jnp.where(left, csb[0], csb[1])
            ecs2 = jnp.exp(cs2)
            dec2 = jnp.exp(cs2[BLK - 1:BLK, :] - cs2)
            dt2 = jnp.where(left, colb(dt, h0), colb(dt, h1))
            xd2 = xs2 * dt2
            ms = []
            for idx, h in enumerate((h0, h1)):
                seg = csb[idx] - jnp.broadcast_to(cs_t[h:h + 1, :], (BLK, BLK))
                ms.append((cb * jnp.exp(jnp.where(tri, seg, -jnp.inf))).astype(BF16))
            xd2b = xd2.astype(BF16)
            zero = jnp.zeros_like(xd2b)
            xd_diag = jnp.concatenate([jnp.where(left, xd2b, zero), jnp.where(left, zero, xd2b)], axis=0)
            yd = _dot(jnp.concatenate(ms, axis=1), xd_diag)
            dsk2 = jnp.where(left, dskip_ref[h0], dskip_ref[h1])
            late.append((cols, yd, yoff[:, hp * LANES:(hp + 1) * LANES] * ecs2, xs2 * dsk2))
            xdec.append((xd2 * dec2).astype(BF16))
            cdec.append(ecs2[BLK - 1:BLK, :])
        xdec = jnp.concatenate(xdec, axis=1)
        cdec = jnp.concatenate(cdec, axis=1)
        ht_ref[:, gs] = cdec * ht_g + _dot(bg.T.astype(BF16), xdec)
        for cols, yd, off, skip in late:
            y_ref[pl.ds(r0, BLK), cols] = (yd + off) + skip
        yield


def _gated_norm(y, z, w):
    g = y * _silu(z)
    parts = []
    width = D_SSM // SSM_GROUPS
    for i in range(SSM_GROUPS):
        gg = g[:, i * width:(i + 1) * width]
        parts.append(gg * lax.rsqrt(jnp.mean(gg * gg, axis=-1, keepdims=True) + EPS))
    return jnp.concatenate(parts, axis=1) * w


def _conv(xbc_ref, rows, conv_w_ref, conv_b_ref):
    acc = conv_b_ref[...]
    for i in range(CONV_WIDTH):
        off = HIST - (CONV_WIDTH - 1) + i
        acc = acc + xbc_ref[off:off + rows, :] * conv_w_ref[i:i + 1, :]
    return _silu(acc)


CONV_PHASES = 4


def _conv_slab(xbc_ref, xc_ref, c, rows, conv_w_ref, conv_b_ref):
    n = rows // CONV_PHASES
    cols = slice(c * LANES, (c + 1) * LANES)
    taps = [conv_w_ref[i:i + 1, cols] for i in range(CONV_WIDTH)]
    bias = conv_b_ref[:, cols]
    lo = -(CONV_WIDTH - 1)
    shifted = [xbc_ref[c, pl.ds(HIST + s, n, stride=CONV_PHASES), :] for s in range(lo, CONV_PHASES)]
    for r in range(CONV_PHASES):
        acc = bias
        for i in range(CONV_WIDTH):
            acc = acc + shifted[r + i] * taps[i]
        xc_ref[c, pl.ds(r, n, stride=CONV_PHASES), :] = _silu(acc)


def _nat_proj(u, w_nat_ref, lo, hi):
    return _dot(u, w_nat_ref[:, lo:hi])


def _out_proj(x, attn_t, yn, w_out_ref):
    return x + (_dot_tn(attn_t, w_out_ref[0:D_ATTN, :]) + _dot(yn, w_out_ref[D_ATTN:, :]))


def _mix_prompt_kernel(sinks_ref, dskip_ref, xp_ref, xo_ref, g1_ref, w_qvt_ref, w_nat_ref, conv_w_ref, conv_b_ref,
                       dtb_ref, alog_ref, normw_ref, w_out_ref,
                       h1_ref, kout_ref, vout_ref, convout_ref, ssmout_ref,
                       qt_s, kh_s, vth_s, z_s, xbc_s, xc_s, dt_s, y_s, attnt_s, yn_s, ht_s, *, ts, nj):
    t = pl.program_id(0)
    sp = t % 2
    sc = 1 - sp
    jp = t % nj
    jc = (t + nj - 1) % nj

    @pl.when(t == 0)
    def _():
        for ref in (qt_s, kh_s, vth_s, z_s, xbc_s, xc_s, dt_s, attnt_s, yn_s):
            ref[...] = jnp.zeros(ref.shape, ref.dtype)

    @pl.when(jc == 0)
    def _():
        ht_s[...] = jnp.zeros(ht_s.shape, F32)

    def project():
        fresh = jp == 0
        kh_s[sp, 0:WINDOW, :] = jnp.where(fresh, 0.0, kh_s[sc, ts:ts + WINDOW, :])
        vth_s[sp, :, 0:WINDOW] = jnp.where(fresh, 0.0, vth_s[sc, :, ts:ts + WINDOW])
        xbc_s[sp, :, 0:HIST, :] = jnp.where(fresh, 0.0, xbc_s[sc, :, ts:ts + HIST, :])
        u = _rms(xp_ref[0], g1_ref[...]).astype(BF16)
        attn_t = attnt_s[sp]
        yn = yn_s[sp]
        n_piece = 4
        slabs = CONV_DIM // LANES // n_piece

        def xbc(c):
            w = CONV_DIM // n_piece
            piece = _nat_proj(u, w_nat_ref, N_X + c * w, N_X + (c + 1) * w)
            for k in range(slabs):
                xbc_s[sp, c * slabs + k, HIST:HIST + ts, :] = piece[:, k * LANES:(k + 1) * LANES]

        def conv(c):
            for k in range(slabs):
                _conv_slab(xbc_s.at[sp], xc_s.at[sp], c * slabs + k, ts, conv_w_ref, conv_b_ref)

        def q(c):
            w = D_ATTN // n_piece
            qt_s[sp, c * w:(c + 1) * w, :] = (_dot_nt(w_qvt_ref[c * w:(c + 1) * w, :], u) * SCALE).astype(BF16)

        def z(c):
            w = D_SSM // n_piece
            z_s[sp, :, c * w:(c + 1) * w] = _nat_proj(u, w_nat_ref, N_Z + c * w, N_Z + (c + 1) * w)

        def out(c):
            w = D_MODEL // n_piece
            cols = slice(c * w, (c + 1) * w)
            h1_ref[0, :, cols] = xo_ref[0, :, cols] + (_dot_tn(attn_t, w_out_ref[0:D_ATTN, cols])
                                                       + _dot(yn, w_out_ref[D_ATTN:, cols]))

        def v():
            vth_s[sp, :, WINDOW:WINDOW + ts] = _dot_nt(w_qvt_ref[D_ATTN:, :], u)

        def k():
            kh_s[sp, WINDOW:WINDOW + ts, :] = _nat_proj(u, w_nat_ref, N_K, N_DT)

        def dt():
            dt_s[sp] = _softplus(_nat_proj(u, w_nat_ref, N_DT, N_Z) + dtb_ref[...])

        placement = [
            [(xbc, 0)], [(xbc, 1), (conv, 0)], [(xbc, 2), (conv, 1)], [(xbc, 3), (conv, 2)],
            [(q, 0)], [(q, 1)], [(q, 2)], [(q, 3)], [(z, 0)],
            [(out, 0), (conv, 3)], [(out, 1)], [(out, 2)], [(out, 3)],
            [(z, 1)], [(z, 2)], [(z, 3)], [(v,)], [(k,)],
            [(dt,)],
        ]
        for group in placement:
            yield
            for fn, *args in group:
                fn(*args)

    pieces = project()
    advance = lambda: next(pieces, None)
    advance()

    a_row = -jnp.exp(alog_ref[...])
    key = lax.broadcasted_iota(jnp.int32, (2 * BLK, BLK), 0)
    qc = lax.broadcasted_iota(jnp.int32, (2 * BLK, BLK), 1) // CHUNK
    band = (key >= qc * CHUNK) & (key < (qc + 3) * CHUNK)
    for p in range(ts // BLK):
        r0 = p * BLK
        first = jnp.where(jc == 0, WINDOW, 0) if p == 0 else 0
        mask_t = band & (key >= first)

        def emit(kv, o, r0=r0):
            attnt_s[sc, kv * Q_PER_KV * HEAD_DIM:(kv + 1) * Q_PER_KV * HEAD_DIM, r0:r0 + BLK] = o.astype(BF16)

        attn = _attn_steps(qt_s[sc, :, r0:r0 + BLK], kh_s[sc, r0:r0 + 2 * BLK, :].astype(BF16),
                           vth_s[sc, :, r0:r0 + 2 * BLK].astype(BF16), mask_t, sinks_ref, emit, advance)
        ssd = _ssd_steps(lambda i, r0=r0: xc_s[sc, i, r0:r0 + BLK, :], dt_s[sc, r0:r0 + BLK, :], a_row, dskip_ref,
                         ht_s, y_s, r0, advance)
        for _ in attn:
            pass
        for _ in ssd:
            pass
    for p in range(ts // BLK):
        rows = slice(p * BLK, (p + 1) * BLK)
        advance()
        yn_s[sc, rows, :] = _gated_norm(y_s[rows, :], z_s[sc, rows, :], normw_ref[...]).astype(BF16)
    for _ in pieces:
        pass

    @pl.when(jc == nj - 1)
    def _():
        kout_ref[0] = kh_s[sc, ts:ts + WINDOW, :]
        vout_ref[0] = vth_s[sc, :, ts:ts + WINDOW].T
        for c in range(CONV_DIM // LANES):
            convout_ref[0, :, c * LANES:(c + 1) * LANES] = xbc_s[sc, c, HIST + ts - (CONV_WIDTH - 1):HIST + ts, :]
        ssmout_ref[0] = ht_s[...].T


def _mix_prompt(x, p, ts):
    b, s, _ = x.shape
    nj = s // ts
    nt = b * nj
    const2 = lambda t: (0, 0)
    prev2 = lambda t: (jnp.maximum(t - 2, 0), 0, 0)
    per_stream = lambda t: (jnp.clip(t - 1, 0, nt - 1) // nj, 0, 0)
    smem = pl.BlockSpec(memory_space=pltpu.SMEM)
    in_specs = [
        smem, smem,
        pl.BlockSpec((1, ts, D_MODEL), lambda t: (jnp.minimum(t, nt - 1), 0, 0)),
        pl.BlockSpec((1, ts, D_MODEL), prev2),
        pl.BlockSpec((1, D_MODEL), const2),
        pl.BlockSpec((D_ATTN + D_KV, D_MODEL), const2),
        pl.BlockSpec((D_MODEL, D_NAT), const2),
        pl.BlockSpec((CONV_WIDTH, CONV_DIM), const2),
        pl.BlockSpec((1, CONV_DIM), const2),
        pl.BlockSpec((1, DT_PAD), const2),
        pl.BlockSpec((1, DT_PAD), const2),
        pl.BlockSpec((1, D_SSM), const2),
        pl.BlockSpec((D_MIX, D_MODEL), const2),
    ]
    out_shape = (
        jax.ShapeDtypeStruct((nt, ts, D_MODEL), F32),
        jax.ShapeDtypeStruct((b, WINDOW, D_KV), F32),
        jax.ShapeDtypeStruct((b, WINDOW, D_KV), F32),
        jax.ShapeDtypeStruct((b, CONV_WIDTH - 1, CONV_DIM), F32),
        jax.ShapeDtypeStruct((b, D_SSM, SSM_STATE), F32),
    )
    out_specs = (
        pl.BlockSpec((1, ts, D_MODEL), prev2),
        pl.BlockSpec((1, WINDOW, D_KV), per_stream),
        pl.BlockSpec((1, WINDOW, D_KV), per_stream),
        pl.BlockSpec((1, CONV_WIDTH - 1, CONV_DIM), per_stream),
        pl.BlockSpec((1, D_SSM, SSM_STATE), per_stream),
    )
    scratch = [
        pltpu.VMEM((2, D_ATTN, ts), BF16),
        pltpu.VMEM((2, WINDOW + ts, D_KV), F32),
        pltpu.VMEM((2, D_KV, WINDOW + ts), F32),
        pltpu.VMEM((2, ts, D_SSM), F32),
        pltpu.VMEM((2, CONV_DIM // LANES, HIST + ts, LANES), F32),
        pltpu.VMEM((2, CONV_DIM // LANES, ts, LANES), F32),
        pltpu.VMEM((2, ts, DT_PAD), F32),
        pltpu.VMEM((ts, D_SSM), F32),
        pltpu.VMEM((2, D_ATTN, ts), BF16),
        pltpu.VMEM((2, ts, D_SSM), BF16),
        pltpu.VMEM((SSM_STATE, D_SSM), F32),
    ]
    x3 = x.reshape(nt, ts, D_MODEL)
    outs = pl.pallas_call(
        functools.partial(_mix_prompt_kernel, ts=ts, nj=nj),
        grid=(nt + 2,),
        in_specs=in_specs,
        out_specs=out_specs,
        out_shape=out_shape,
        scratch_shapes=scratch,
        compiler_params=pltpu.CompilerParams(dimension_semantics=("arbitrary",), vmem_limit_bytes=VMEM_LIMIT),
        name="mix_prompt",
    )(p["sinks"], p["d_skip"], x3, x3, p["ln1_g"], p["w_qvt"], p["w_nat"], p["conv_w"], p["conv_b"], p["dt_bias"],
      p["a_log"], p["ssm_norm_w"], p["w_out"])
    return (outs[0].reshape(b, s, D_MODEL),) + tuple(outs[1:])


def _mix_sample_kernel(sinks_ref, dskip_ref, x_ref, ck_ref, cv_ref, sconv_ref, sssm_ref, g1_ref, w_qvt_ref,
                       w_nat_ref, conv_w_ref, conv_b_ref, dtb_ref, alog_ref, normw_ref, w_out_ref,
                       h1_ref, kout_ref, vout_ref, convout_ref, ssmout_ref,
                       qt_s, vt_s, z_s, xbcall_s, dt_s, xbc_s, y_s, attnt_s, yn_s, ht_s, *, nb, t):
    i = pl.program_id(0)
    n = nb * t
    per_tile = BLK // t

    @pl.when(i == 0)
    def _():
        u = _rms(x_ref[...], g1_ref[...]).astype(BF16)
        qt = (_dot_nt(w_qvt_ref[0:D_ATTN, :], u) * SCALE).astype(BF16)
        vt = _dot_nt(w_qvt_ref[D_ATTN:, :], u)
        for c in range(n // BLK):
            qt_s[c] = qt[:, c * BLK:(c + 1) * BLK]
            vt_s[c] = vt[:, c * BLK:(c + 1) * BLK]
        vout_ref[...] = vt.T
        kout_ref[...] = _nat_proj(u, w_nat_ref, N_K, N_DT)
        z_s[...] = _nat_proj(u, w_nat_ref, N_Z, N_X)
        xbcall_s[...] = _nat_proj(u, w_nat_ref, N_X, D_NAT)
        dt_s[...] = _softplus(_nat_proj(u, w_nat_ref, N_DT, N_Z) + dtb_ref[...])
        attnt_s[...] = jnp.zeros(attnt_s.shape, BF16)
        xbc_s[...] = jnp.zeros(xbc_s.shape, F32)

    @pl.when((i >= 1) & (i <= nb))
    def _():
        b = i - 1
        rows = pl.ds(pl.multiple_of(b * t, t), t)
        c = b // per_tile
        lo = (b % per_tile) * t
        kwin = jnp.concatenate([kout_ref[pl.ds(pl.multiple_of(c * BLK, BLK), BLK), :], ck_ref[0]], axis=0)
        vtwin = jnp.concatenate([vt_s[c], cv_ref[0].T], axis=1)
        key = lax.broadcasted_iota(jnp.int32, (2 * BLK, BLK), 0)
        mask_t = ((key >= lo) & (key < lo + t)) | (key >= BLK)
        xbc_s[HIST - (CONV_WIDTH - 1):HIST, :] = sconv_ref[0]
        xbc_s[HIST:HIST + t, :] = xbcall_s[rows, :]
        convout_ref[0] = xbc_s[HIST + t - (CONV_WIDTH - 1):HIST + t, :]
        live = lax.broadcasted_iota(jnp.int32, (BLK, 1), 0) < t
        xc = jnp.where(live, _conv(xbc_s, BLK, conv_w_ref, conv_b_ref), 0.0)
        dt = jnp.concatenate([dt_s[rows, :], jnp.zeros((BLK - t, DT_PAD), F32)], axis=0)
        ht_s[...] = sssm_ref[0].T
        ssd = _ssd_steps(lambda i: xc[:, i * LANES:(i + 1) * LANES], dt, -jnp.exp(alog_ref[...]), dskip_ref, ht_s,
                         y_s, 0)

        outs = [None] * N_KV_HEADS

        def emit(kv, o):
            outs[kv] = o

        for _ in _attn_steps(qt_s[c], kwin.astype(BF16), vtwin.astype(BF16), mask_t, sinks_ref, emit,
                             lambda: next(ssd, None)):
            pass
        for _ in ssd:
            pass
        o_t = jnp.concatenate(outs, axis=0)
        lane = lax.broadcasted_iota(jnp.int32, (D_ATTN, BLK), 1)
        attnt_s[c] = jnp.where((lane >= lo) & (lane < lo + t), o_t.astype(BF16), attnt_s[c])
        ssmout_ref[0] = ht_s[...].T
        yn_s[rows, :] = _gated_norm(y_s[0:t, :], z_s[rows, :], normw_ref[...]).astype(BF16)

    @pl.when(i == nb + 1)
    def _():
        for c in range(n // BLK):
            r = slice(c * BLK, (c + 1) * BLK)
            h1_ref[r, :] = _out_proj(x_ref[r, :], attnt_s[c], yn_s[r, :], w_out_ref)


def _mix_sample(x, cache_k, cache_v, state_conv, state_ssm, p):
    nb, t, _ = x.shape
    n = nb * t
    assert BLK % t == 0 and n % BLK == 0
    const2 = lambda i: (0, 0)
    per = lambda i: (jnp.clip(i - 1, 0, nb - 1), 0, 0)
    smem = pl.BlockSpec(memory_space=pltpu.SMEM)
    in_specs = [
        smem, smem,
        pl.BlockSpec((n, D_MODEL), const2),
        pl.BlockSpec((1, WINDOW, D_KV), per),
        pl.BlockSpec((1, WINDOW, D_KV), per),
        pl.BlockSpec((1, CONV_WIDTH - 1, CONV_DIM), per),
        pl.BlockSpec((1, D_SSM, SSM_STATE), per),
        pl.BlockSpec((1, D_MODEL), const2),
        pl.BlockSpec((D_ATTN + D_KV, D_MODEL), const2),
        pl.BlockSpec((D_MODEL, D_NAT), const2),
        pl.BlockSpec((CONV_WIDTH, CONV_DIM), const2),
        pl.BlockSpec((1, CONV_DIM), const2),
        pl.BlockSpec((1, DT_PAD), const2),
        pl.BlockSpec((1, DT_PAD), const2),
        pl.BlockSpec((1, D_SSM), const2),
        pl.BlockSpec((D_MIX, D_MODEL), const2),
    ]
    out_shape = (
        jax.ShapeDtypeStruct((n, D_MODEL), F32),
        jax.ShapeDtypeStruct((n, D_KV), F32),
        jax.ShapeDtypeStruct((n, D_KV), F32),
        jax.ShapeDtypeStruct((nb, CONV_WIDTH - 1, CONV_DIM), F32),
        jax.ShapeDtypeStruct((nb, D_SSM, SSM_STATE), F32),
    )
    out_specs = (
        pl.BlockSpec((n, D_MODEL), const2),
        pl.BlockSpec((n, D_KV), const2),
        pl.BlockSpec((n, D_KV), const2),
        pl.BlockSpec((1, CONV_WIDTH - 1, CONV_DIM), per),
        pl.BlockSpec((1, D_SSM, SSM_STATE), per),
    )
    scratch = [
        pltpu.VMEM((n // BLK, D_ATTN, BLK), BF16),
        pltpu.VMEM((n // BLK, D_KV, BLK), F32),
        pltpu.VMEM((n, D_SSM), F32),
        pltpu.VMEM((n, CONV_DIM), F32),
        pltpu.VMEM((n, DT_PAD), F32),
        pltpu.VMEM((HIST + BLK, CONV_DIM), F32),
        pltpu.VMEM((BLK, D_SSM), F32),
        pltpu.VMEM((n // BLK, D_ATTN, BLK), BF16),
        pltpu.VMEM((n, D_SSM), BF16),
        pltpu.VMEM((SSM_STATE, D_SSM), F32),
    ]
    return pl.pallas_call(
        functools.partial(_mix_sample_kernel, nb=nb, t=t),
        grid=(nb + 2,),
        in_specs=in_specs,
        out_specs=out_specs,
        out_shape=out_shape,
        scratch_shapes=scratch,
        compiler_params=pltpu.CompilerParams(dimension_semantics=("arbitrary",), vmem_limit_bytes=VMEM_LIMIT),
        name="mix_sample",
    )(p["sinks"], p["d_skip"], x.reshape(n, D_MODEL), cache_k, cache_v, state_conv, state_ssm, p["ln1_g"],
      p["w_qvt"], p["w_nat"], p["conv_w"], p["conv_b"], p["dt_bias"], p["a_log"], p["ssm_norm_w"], p["w_out"])


FFN_ROWS = 512


def _ffn_kernel(h_ref, hx_ref, g2_ref, wg_ref, wu_ref, wd_ref, gf_ref, o_ref, ox_ref):
    _ffn_rows(h_ref, g2_ref, wg_ref, wu_ref, wd_ref, gf_ref, o_ref)

    @pl.when(pl.program_id(0) == pl.num_programs(0) - 1)
    def _():
        _ffn_rows(hx_ref, g2_ref, wg_ref, wu_ref, wd_ref, gf_ref, ox_ref)


def _ffn_rows(h_ref, g2_ref, wg_ref, wu_ref, wd_ref, gf_ref, o_ref):
    tm = h_ref.shape[0]
    chunks = [slice(r, min(r + FFN_ROWS, tm)) for r in range(0, tm, FFN_ROWS)]

    def up(rows):
        u = _rms(h_ref[rows, :], g2_ref[...]).astype(BF16)
        return _dot(u, wg_ref[...]), _dot(u, wu_ref[...])

    def down(rows, gate, upv):
        mid = (_silu(gate) * upv).astype(BF16)
        o_ref[rows, :] = _rms(h_ref[rows, :] + _dot(mid, wd_ref[...]), gf_ref[...])

    pending = up(chunks[0])
    for i, rows in enumerate(chunks):
        nxt = up(chunks[i + 1]) if i + 1 < len(chunks) else None
        down(rows, *pending)
        pending = nxt


def _ffn(h, hx, p, tm):
    n, nx = h.shape[0], hx.shape[0]
    d_ff = p["w_gate"].shape[1]
    const2 = lambda i: (0, 0)
    return pl.pallas_call(
        _ffn_kernel,
        grid=(n // tm,),
        in_specs=[
            pl.BlockSpec((tm, D_MODEL), lambda i: (i, 0)),
            pl.BlockSpec((nx, D_MODEL), const2),
            pl.BlockSpec((1, D_MODEL), const2),
            pl.BlockSpec((D_MODEL, d_ff), const2),
            pl.BlockSpec((D_MODEL, d_ff), const2),
            pl.BlockSpec((d_ff, D_MODEL), const2),
            pl.BlockSpec((1, D_MODEL), const2),
        ],
        out_specs=(pl.BlockSpec((tm, D_MODEL), lambda i: (i, 0)), pl.BlockSpec((nx, D_MODEL), const2)),
        out_shape=(jax.ShapeDtypeStruct((n, D_MODEL), F32), jax.ShapeDtypeStruct((nx, D_MODEL), F32)),
        compiler_params=pltpu.CompilerParams(dimension_semantics=("arbitrary",), vmem_limit_bytes=VMEM_LIMIT),
        name="ffn",
    )(h, hx, p["ln2_g"], p["w_gate"], p["w_up"], p["w_down"], p["final_g"])


def _prep_params(ln1_g, w_in, conv_w, conv_b, dt_bias, a_log, d_skip, ssm_norm_w, sinks, w_out, ln2_g, w_gate,
                 w_up, w_down, final_g):
    pad_h = lambda v: jnp.pad(v.reshape(1, SSM_HEADS), ((0, 0), (0, DT_PAD - SSM_HEADS)))
    o_k, o_v, o_z = D_ATTN, D_ATTN + D_KV, D_ATTN + 2 * D_KV
    w_q, w_k, w_v = w_in[:, :o_k], w_in[:, o_k:o_v], w_in[:, o_v:o_z]
    w_zx, w_dt = w_in[:, o_z:o_z + D_SSM + CONV_DIM], w_in[:, o_z + D_SSM + CONV_DIM:]
    w_dt = jnp.pad(w_dt, ((0, 0), (0, DT_PAD - w_dt.shape[1])))
    return {
        "ln1_g": ln1_g.reshape(1, D_MODEL),
        "w_qvt": jnp.concatenate([w_q, w_v], axis=1).T.astype(BF16),
        "w_nat": jnp.concatenate([w_k, w_dt, w_zx], axis=1).astype(BF16),
        "conv_w": conv_w,
        "conv_b": conv_b.reshape(1, CONV_DIM),
        "dt_bias": pad_h(dt_bias),
        "a_log": pad_h(a_log),
        "d_skip": d_skip,
        "ssm_norm_w": ssm_norm_w.reshape(1, D_SSM),
        "sinks": sinks,
        "w_out": w_out.astype(BF16),
        "ln2_g": ln2_g.reshape(1, D_MODEL),
        "w_gate": w_gate.astype(BF16),
        "w_up": w_up.astype(BF16),
        "w_down": w_down.astype(BF16),
        "final_g": final_g.reshape(1, D_MODEL),
    }


def _layer(x_prompt, x_sample, cache_k, cache_v, state_conv, state_ssm, p, ts, tm):
    b, s, _ = x_prompt.shape
    nb, t, _ = x_sample.shape
    h1p, kp, vp, cp, sp = _mix_prompt(x_prompt, p, ts)
    h1s, kn, vn, cn, sn = _mix_sample(x_sample, cache_k.reshape(nb, WINDOW, D_KV), cache_v.reshape(nb, WINDOW, D_KV),
                                      state_conv, state_ssm.reshape(nb, D_SSM, SSM_STATE), p)
    yp, ys = _ffn(h1p.reshape(b * s, D_MODEL), h1s, p, tm)
    yp, ys = yp.reshape(b, s, D_MODEL), ys.reshape(nb, t, D_MODEL)
    return (yp, ys,
            kp.reshape(b, WINDOW, N_KV_HEADS, HEAD_DIM), vp.reshape(b, WINDOW, N_KV_HEADS, HEAD_DIM), cp,
            sp.reshape(b, SSM_HEADS, SSM_HEAD_DIM, SSM_STATE),
            kn.reshape(nb, t, N_KV_HEADS, HEAD_DIM), vn.reshape(nb, t, N_KV_HEADS, HEAD_DIM), cn,
            sn.reshape(nb, SSM_HEADS, SSM_HEAD_DIM, SSM_STATE))


def kernel(x_prompt, x_sample, cache_k, cache_v, state_conv, state_ssm, ln1_g, w_in, conv_w, conv_b, dt_bias, a_log,
           d_skip, ssm_norm_w, sinks, w_out, ln2_g, w_gate, w_up, w_down, final_g):
    assert w_in.shape[0] == 1, "one layer"
    p = _prep_params(ln1_g[0], w_in[0], conv_w[0], conv_b[0], dt_bias[0], a_log[0], d_skip[0], ssm_norm_w[0],
                     sinks[0], w_out[0], ln2_g[0], w_gate[0], w_up[0], w_down[0], final_g)
    outs = _layer(x_prompt, x_sample, cache_k[0], cache_v[0], state_conv[0], state_ssm[0], p, ts=256, tm=1024)
    return tuple(o[None] if i >= 2 else o for i, o in enumerate(outs))
```

```python
import functools

import jax
import jax.numpy as jnp
from jax import lax
from jax.experimental import pallas as pl
from jax.experimental.pallas import tpu as pltpu

F32 = jnp.float32
BF16 = jnp.bfloat16

D_MODEL = 1024
CHUNK = 64
WINDOW = 128
HEAD_DIM = 64
N_HEADS = 16
N_KV_HEADS = 4
Q_PER_KV = N_HEADS // N_KV_HEADS
D_ATTN = N_HEADS * HEAD_DIM
D_KV = N_KV_HEADS * HEAD_DIM
SCALE = HEAD_DIM ** -0.5
SSM_HEADS = 16
SSM_HEAD_DIM = 64
SSM_GROUPS = 4
SSM_STATE = 128
D_SSM = SSM_HEADS * SSM_HEAD_DIM
D_BC = SSM_GROUPS * SSM_STATE
CONV_WIDTH = 4
CONV_DIM = D_SSM + 2 * D_BC
D_MIX = D_ATTN + D_SSM
EPS = 1e-6

LANES = 128
BLK = 2 * CHUNK
DT_PAD = LANES
N_K, N_DT, N_Z, N_X = 0, D_KV, D_KV + DT_PAD, D_KV + DT_PAD + D_SSM
D_NAT = N_X + CONV_DIM
HIST = 8
VMEM_LIMIT = 56 * 1024 * 1024


def _rms(x, g):
    ms = jnp.mean(x * x, axis=-1, keepdims=True)
    return (x * lax.rsqrt(ms + EPS)) * g


def _silu(x):
    h = 0.5 * x
    return h + h * jnp.tanh(h)


def _softplus(x):
    return jnp.maximum(x, 0.0) + jnp.log1p(jnp.exp(-jnp.abs(x)))


def _dot(a, b):
    return jnp.dot(a, b, preferred_element_type=F32)


def _dot_nt(a, b):
    return lax.dot_general(a, b, (((1,), (1,)), ((), ())), preferred_element_type=F32)


def _dot_tn(a, b):
    return lax.dot_general(a, b, (((0,), (0,)), ((), ())), preferred_element_type=F32)


def _left_half(shape):
    return lax.broadcasted_iota(jnp.int32, shape, len(shape) - 1) < HEAD_DIM


def _attn_steps(qt_blk, kwin, vtwin, mask_t, sinks_ref, emit, between=None):
    nq = Q_PER_KV * BLK
    mask4 = jnp.concatenate([mask_t] * Q_PER_KV, axis=1)
    zeros = jnp.zeros((HEAD_DIM, nq), BF16)

    def scores(kv):
        slab, half = kv // 2, kv % 2
        qk = jnp.concatenate([qt_blk[(kv * Q_PER_KV + g) * HEAD_DIM:(kv * Q_PER_KV + g + 1) * HEAD_DIM, :]
                              for g in range(Q_PER_KV)], axis=1)
        rhs = jnp.concatenate([qk, zeros] if half == 0 else [zeros, qk], axis=0)
        return _dot(kwin[:, slab * LANES:(slab + 1) * LANES], rhs)

    s_next = scores(0)
    for kv in range(N_KV_HEADS):
        s = s_next
        if kv + 1 < N_KV_HEADS:
            s_next = scores(kv + 1)
        if between is not None:
            between()
        s = jnp.where(mask4, s, -jnp.inf)
        sink = jnp.concatenate([jnp.full((1, BLK), sinks_ref[kv * Q_PER_KV + g], F32) for g in range(Q_PER_KV)],
                               axis=1)
        m = jnp.maximum(jnp.max(s, axis=0, keepdims=True), sink)
        e = jnp.exp(s - m)
        denom = jnp.sum(e, axis=0, keepdims=True) + jnp.exp(sink - m)
        p = (e * (1.0 / denom)).astype(BF16)
        o = _dot(vtwin[kv * HEAD_DIM:(kv + 1) * HEAD_DIM, :], p)
        emit(kv, jnp.concatenate([o[:, g * BLK:(g + 1) * BLK] for g in range(Q_PER_KV)], axis=0))
        yield


def _split3(x):
    hi = x.astype(BF16)
    r1 = x - hi.astype(F32)
    mid = r1.astype(BF16)
    lo = (r1 - mid.astype(F32)).astype(BF16)
    return hi, mid, lo


def _ssd_steps(xc, dt, a_row, dskip_ref, ht_ref, y_ref, r0, between=None):
    left = _left_half((BLK, LANES))
    row_i = lax.broadcasted_iota(jnp.int32, (BLK, BLK), 0)
    col_i = lax.broadcasted_iota(jnp.int32, (BLK, BLK), 1)
    tri = col_i <= row_i
    tri_b = tri.astype(BF16)
    da = dt * a_row
    hi, mid, lo = _split3(da)
    parts = _dot(tri_b, jnp.concatenate([hi, mid, lo], axis=1))
    cs = (parts[:, 0:LANES] + parts[:, LANES:2 * LANES]) + parts[:, 2 * LANES:]
    cs_t = cs.T
    if between is not None:
        between()

    def colb(a, h):
        return jnp.broadcast_to(a[:, h:h + 1], (BLK, LANES))

    def group_products(g):
        bg = xc((D_SSM + g * SSM_STATE) // LANES)
        cg = xc((D_SSM + D_BC + g * SSM_STATE) // LANES).astype(BF16)
        cb = _dot_nt(cg, bg.astype(BF16))
        ht_g = ht_ref[:, g * 2 * LANES:(g + 1) * 2 * LANES]
        yoff = _dot(cg, ht_g.astype(BF16))
        return bg, cb, ht_g, yoff

    nxt = group_products(0)
    for g in range(SSM_GROUPS):
        bg, cb, ht_g, yoff = nxt
        gs = slice(g * 2 * LANES, (g + 1) * 2 * LANES)
        if g + 1 < SSM_GROUPS:
            nxt = group_products(g + 1)
        if between is not None:
            between()
        xdec, cdec, late = [], [], []
        for hp in range(2):
            h0 = g * 4 + 2 * hp
            h1 = h0 + 1
            cols = slice((g * 2 + hp) * LANES, (g * 2 + hp + 1) * LANES)
            xs2 = xc(g * 2 + hp)
            csb = [colb(cs, h0), colb(cs, h1)]
            cs2 = jnp.where(left, csb[0], csb[1])
            ecs2 = jnp.exp(cs2)
            dec2 = jnp.exp(cs2[BLK - 1:BLK, :] - cs2)
            dt2 = jnp.where(left, colb(dt, h0), colb(dt, h1))
            xd2 = xs2 * dt2
            ms = []
            for idx, h in enumerate((h0, h1)):
                seg = csb[idx] - jnp.broadcast_to(cs_t[h:h + 1, :], (BLK, BLK))
                ms.append((cb * jnp.exp(jnp.where(tri, seg, -jnp.inf))).astype(BF16))
            xd2b = xd2.astype(BF16)
            zero = jnp.zeros_like(xd2b)
            xd_diag = jnp.concatenate([jnp.where(left, xd2b, zero), jnp.where(left, zero, xd2b)], axis=0)
            yd = _dot(jnp.concatenate(ms, axis=1), xd_diag)
            dsk2 = jnp.where(left, dskip_ref[h0], dskip_ref[h1])
            late.append((cols, yd, yoff[:, hp * LANES:(hp + 1) * LANES] * ecs2, xs2 * dsk2))
            xdec.append((xd2 * dec2).astype(BF16))
            cdec.append(ecs2[BLK - 1:BLK, :])
        xdec = jnp.concatenate(xdec, axis=1)
        cdec = jnp.concatenate(cdec, axis=1)
        ht_ref[:, gs] = cdec * ht_g + _dot(bg.T.astype(BF16), xdec)
        for cols, yd, off, skip in late:
            y_ref[pl.ds(r0, BLK), cols] = (yd + off) + skip
        yield


def _gated_norm(y, z, w):
    g = y * _silu(z)
    parts = []
    width = D_SSM // SSM_GROUPS
    for i in range(SSM_GROUPS):
        gg = g[:, i * width:(i + 1) * width]
        parts.append(gg * lax.rsqrt(jnp.mean(gg * gg, axis=-1, keepdims=True) + EPS))
    return jnp.concatenate(parts, axis=1) * w


def _conv(xbc_ref, rows, conv_w_ref, conv_b_ref):
    acc = conv_b_ref[...]
    for i in range(CONV_WIDTH):
        off = HIST - (CONV_WIDTH - 1) + i
        acc = acc + xbc_ref[off:off + rows, :] * conv_w_ref[i:i + 1, :]
    return _silu(acc)


CONV_PHASES = 4


def _conv_slab(xbc_ref, xc_ref, c, rows, conv_w_ref, conv_b_ref):
    n = rows // CONV_PHASES
    cols = slice(c * LANES, (c + 1) * LANES)
    taps = [conv_w_ref[i:i + 1, cols] for i in range(CONV_WIDTH)]
    bias = conv_b_ref[:, cols]
    lo = -(CONV_WIDTH - 1)
    shifted = [xbc_ref[c, pl.ds(HIST + s, n, stride=CONV_PHASES), :] for s in range(lo, CONV_PHASES)]
    for r in range(CONV_PHASES):
        acc = bias
        for i in range(CONV_WIDTH):
            acc = acc + shifted[r + i] * taps[i]
        xc_ref[c, pl.ds(r, n, stride=CONV_PHASES), :] = _silu(acc)


def _nat_proj(u, w_nat_ref, lo, hi):
    return _dot(u, w_nat_ref[:, lo:hi])


def _out_proj(x, attn_t, yn, w_out_ref):
    return x + (_dot_tn(attn_t, w_out_ref[0:D_ATTN, :]) + _dot(yn, w_out_ref[D_ATTN:, :]))


def _mix_prompt_kernel(sinks_ref, dskip_ref, xp_ref, xo_ref, g1_ref, w_qvt_ref, w_nat_ref, conv_w_ref, conv_b_ref,
                       dtb_ref, alog_ref, normw_ref, w_out_ref,
                       h1_ref, kout_ref, vout_ref, convout_ref, ssmout_ref,
                       qt_s, kh_s, vth_s, z_s, xbc_s, xc_s, dt_s, y_s, attnt_s, yn_s, ht_s, *, ts, nj):
    t = pl.program_id(0)
    sp = t % 2
    sc = 1 - sp
    jp = t % nj
    jc = (t + nj - 1) % nj

    @pl.when(t == 0)
    def _():
        for ref in (qt_s, kh_s, vth_s, z_s, xbc_s, xc_s, dt_s, attnt_s, yn_s):
            ref[...] = jnp.zeros(ref.shape, ref.dtype)

    @pl.when(jc == 0)
    def _():
        ht_s[...] = jnp.zeros(ht_s.shape, F32)

    def project():
        fresh = jp == 0
        kh_s[sp, 0:WINDOW, :] = jnp.where(fresh, 0.0, kh_s[sc, ts:ts + WINDOW, :])
        vth_s[sp, :, 0:WINDOW] = jnp.where(fresh, 0.0, vth_s[sc, :, ts:ts + WINDOW])
        xbc_s[sp, :, 0:HIST, :] = jnp.where(fresh, 0.0, xbc_s[sc, :, ts:ts + HIST, :])
        u = _rms(xp_ref[0], g1_ref[...]).astype(BF16)
        attn_t = attnt_s[sp]
        yn = yn_s[sp]
        n_piece = 4
        slabs = CONV_DIM // LANES // n_piece

        def xbc(c):
            w = CONV_DIM // n_piece
            piece = _nat_proj(u, w_nat_ref, N_X + c * w, N_X + (c + 1) * w)
            for k in range(slabs):
                xbc_s[sp, c * slabs + k, HIST:HIST + ts, :] = piece[:, k * LANES:(k + 1) * LANES]

        def conv(c):
            for k in range(slabs):
                _conv_slab(xbc_s.at[sp], xc_s.at[sp], c * slabs + k, ts, conv_w_ref, conv_b_ref)

        def q(c):
            w = D_ATTN // n_piece
            qt_s[sp, c * w:(c + 1) * w, :] = (_dot_nt(w_qvt_ref[c * w:(c + 1) * w, :], u) * SCALE).astype(BF16)

        def z(c):
            w = D_SSM // n_piece
            z_s[sp, :, c * w:(c + 1) * w] = _nat_proj(u, w_nat_ref, N_Z + c * w, N_Z + (c + 1) * w)

        def out(c):
            w = D_MODEL // n_piece
            cols = slice(c * w, (c + 1) * w)
            h1_ref[0, :, cols] = xo_ref[0, :, cols] + (_dot_tn(attn_t, w_out_ref[0:D_ATTN, cols])
                                                       + _dot(yn, w_out_ref[D_ATTN:, cols]))

        def v():
            vth_s[sp, :, WINDOW:WINDOW + ts] = _dot_nt(w_qvt_ref[D_ATTN:, :], u)

        def k():
            kh_s[sp, WINDOW:WINDOW + ts, :] = _nat_proj(u, w_nat_ref, N_K, N_DT)

        def dt():
            dt_s[sp] = _softplus(_nat_proj(u, w_nat_ref, N_DT, N_Z) + dtb_ref[...])

        placement = [
            [(xbc, 0)], [(xbc, 1), (conv, 0)], [(xbc, 2), (conv, 1)], [(xbc, 3), (conv, 2)],
            [(q, 0)], [(q, 1)], [(q, 2)], [(q, 3)], [(z, 0)],
            [(out, 0), (conv, 3)], [(out, 1)], [(out, 2)], [(out, 3)],
            [(z, 1)], [(z, 2)], [(z, 3)], [(v,)], [(k,)],
            [(dt,)],
        ]
        for group in placement:
            yield
            for fn, *args in group:
                fn(*args)

    pieces = project()
    advance = lambda: next(pieces, None)
    advance()

    a_row = -jnp.exp(alog_ref[...])
    key = lax.broadcasted_iota(jnp.int32, (2 * BLK, BLK), 0)
    qc = lax.broadcasted_iota(jnp.int32, (2 * BLK, BLK), 1) // CHUNK
    band = (key >= qc * CHUNK) & (key < (qc + 3) * CHUNK)
    for p in range(ts // BLK):
        r0 = p * BLK
        first = jnp.where(jc == 0, WINDOW, 0) if p == 0 else 0
        mask_t = band & (key >= first)

        def emit(kv, o, r0=r0):
            attnt_s[sc, kv * Q_PER_KV * HEAD_DIM:(kv + 1) * Q_PER_KV * HEAD_DIM, r0:r0 + BLK] = o.astype(BF16)

        attn = _attn_steps(qt_s[sc, :, r0:r0 + BLK], kh_s[sc, r0:r0 + 2 * BLK, :].astype(BF16),
                           vth_s[sc, :, r0:r0 + 2 * BLK].astype(BF16), mask_t, sinks_ref, emit, advance)
        ssd = _ssd_steps(lambda i, r0=r0: xc_s[sc, i, r0:r0 + BLK, :], dt_s[sc, r0:r0 + BLK, :], a_row, dskip_ref,
                         ht_s, y_s, r0, advance)
        for _ in attn:
            pass
        for _ in ssd:
            pass
    for p in range(ts // BLK):
        rows = slice(p * BLK, (p + 1) * BLK)
        advance()
        yn_s[sc, rows, :] = _gated_norm(y_s[rows, :], z_s[sc, rows, :], normw_ref[...]).astype(BF16)
    for _ in pieces:
        pass

    @pl.when(jc == nj - 1)
    def _():
        kout_ref[0] = kh_s[sc, ts:ts + WINDOW, :]
        vout_ref[0] = vth_s[sc, :, ts:ts + WINDOW].T
        for c in range(CONV_DIM // LANES):
            convout_ref[0, :, c * LANES:(c + 1) * LANES] = xbc_s[sc, c, HIST + ts - (CONV_WIDTH - 1):HIST + ts, :]
        ssmout_ref[0] = ht_s[...].T


def _mix_prompt(x, p, ts):
    b, s, _ = x.shape
    nj = s // ts
    nt = b * nj
    const2 = lambda t: (0, 0)
    prev2 = lambda t: (jnp.maximum(t - 2, 0), 0, 0)
    per_stream = lambda t: (jnp.clip(t - 1, 0, nt - 1) // nj, 0, 0)
    smem = pl.BlockSpec(memory_space=pltpu.SMEM)
    in_specs = [
        smem, smem,
        pl.BlockSpec((1, ts, D_MODEL), lambda t: (jnp.minimum(t, nt - 1), 0, 0)),
        pl.BlockSpec((1, ts, D_MODEL), prev2),
        pl.BlockSpec((1, D_MODEL), const2),
        pl.BlockSpec((D_ATTN + D_KV, D_MODEL), const2),
        pl.BlockSpec((D_MODEL, D_NAT), const2),
        pl.BlockSpec((CONV_WIDTH, CONV_DIM), const2),
        pl.BlockSpec((1, CONV_DIM), const2),
        pl.BlockSpec((1, DT_PAD), const2),
        pl.BlockSpec((1, DT_PAD), const2),
        pl.BlockSpec((1, D_SSM), const2),
        pl.BlockSpec((D_MIX, D_MODEL), const2),
    ]
    out_shape = (
        jax.ShapeDtypeStruct((nt, ts, D_MODEL), F32),
        jax.ShapeDtypeStruct((b, WINDOW, D_KV), F32),
        jax.ShapeDtypeStruct((b, WINDOW, D_KV), F32),
        jax.ShapeDtypeStruct((b, CONV_WIDTH - 1, CONV_DIM), F32),
        jax.ShapeDtypeStruct((b, D_SSM, SSM_STATE), F32),
    )
    out_specs = (
        pl.BlockSpec((1, ts, D_MODEL), prev2),
        pl.BlockSpec((1, WINDOW, D_KV), per_stream),
        pl.BlockSpec((1, WINDOW, D_KV), per_stream),
        pl.BlockSpec((1, CONV_WIDTH - 1, CONV_DIM), per_stream),
        pl.BlockSpec((1, D_SSM, SSM_STATE), per_stream),
    )
    scratch = [
        pltpu.VMEM((2, D_ATTN, ts), BF16),
        pltpu.VMEM((2, WINDOW + ts, D_KV), F32),
        pltpu.VMEM((2, D_KV, WINDOW + ts), F32),
        pltpu.VMEM((2, ts, D_SSM), F32),
        pltpu.VMEM((2, CONV_DIM // LANES, HIST + ts, LANES), F32),
        pltpu.VMEM((2, CONV_DIM // LANES, ts, LANES), F32),
        pltpu.VMEM((2, ts, DT_PAD), F32),
        pltpu.VMEM((ts, D_SSM), F32),
        pltpu.VMEM((2, D_ATTN, ts), BF16),
        pltpu.VMEM((2, ts, D_SSM), BF16),
        pltpu.VMEM((SSM_STATE, D_SSM), F32),
    ]
    x3 = x.reshape(nt, ts, D_MODEL)
    outs = pl.pallas_call(
        functools.partial(_mix_prompt_kernel, ts=ts, nj=nj),
        grid=(nt + 2,),
        in_specs=in_specs,
        out_specs=out_specs,
        out_shape=out_shape,
        scratch_shapes=scratch,
        compiler_params=pltpu.CompilerParams(dimension_semantics=("arbitrary",), vmem_limit_bytes=VMEM_LIMIT),
        name="mix_prompt",
    )(p["sinks"], p["d_skip"], x3, x3, p["ln1_g"], p["w_qvt"], p["w_nat"], p["conv_w"], p["conv_b"], p["dt_bias"],
      p["a_log"], p["ssm_norm_w"], p["w_out"])
    return (outs[0].reshape(b, s, D_MODEL),) + tuple(outs[1:])


def _mix_sample_kernel(sinks_ref, dskip_ref, x_ref, ck_ref, cv_ref, sconv_ref, sssm_ref, g1_ref, w_qvt_ref,
                       w_nat_ref, conv_w_ref, conv_b_ref, dtb_ref, alog_ref, normw_ref, w_out_ref,
                       h1_ref, kout_ref, vout_ref, convout_ref, ssmout_ref,
                       qt_s, vt_s, z_s, xbcall_s, dt_s, xbc_s, y_s, attnt_s, yn_s, ht_s, *, nb, t):
    i = pl.program_id(0)
    n = nb * t
    per_tile = BLK // t

    @pl.when(i == 0)
    def _():
        u = _rms(x_ref[...], g1_ref[...]).astype(BF16)
        qt = (_dot_nt(w_qvt_ref[0:D_ATTN, :], u) * SCALE).astype(BF16)
        vt = _dot_nt(w_qvt_ref[D_ATTN:, :], u)
        for c in range(n // BLK):
            qt_s[c] = qt[:, c * BLK:(c + 1) * BLK]
            vt_s[c] = vt[:, c * BLK:(c + 1) * BLK]
        vout_ref[...] = vt.T
        kout_ref[...] = _nat_proj(u, w_nat_ref, N_K, N_DT)
        z_s[...] = _nat_proj(u, w_nat_ref, N_Z, N_X)
        xbcall_s[...] = _nat_proj(u, w_nat_ref, N_X, D_NAT)
        dt_s[...] = _softplus(_nat_proj(u, w_nat_ref, N_DT, N_Z) + dtb_ref[...])
        attnt_s[...] = jnp.zeros(attnt_s.shape, BF16)
        xbc_s[...] = jnp.zeros(xbc_s.shape, F32)

    @pl.when((i >= 1) & (i <= nb))
    def _():
        b = i - 1
        rows = pl.ds(pl.multiple_of(b * t, t), t)
        c = b // per_tile
        lo = (b % per_tile) * t
        kwin = jnp.concatenate([kout_ref[pl.ds(pl.multiple_of(c * BLK, BLK), BLK), :], ck_ref[0]], axis=0)
        vtwin = jnp.concatenate([vt_s[c], cv_ref[0].T], axis=1)
        key = lax.broadcasted_iota(jnp.int32, (2 * BLK, BLK), 0)
        mask_t = ((key >= lo) & (key < lo + t)) | (key >= BLK)
        xbc_s[HIST - (CONV_WIDTH - 1):HIST, :] = sconv_ref[0]
        xbc_s[HIST:HIST + t, :] = xbcall_s[rows, :]
        convout_ref[0] = xbc_s[HIST + t - (CONV_WIDTH - 1):HIST + t, :]
        live = lax.broadcasted_iota(jnp.int32, (BLK, 1), 0) < t
        xc = jnp.where(live, _conv(xbc_s, BLK, conv_w_ref, conv_b_ref), 0.0)
        dt = jnp.concatenate([dt_s[rows, :], jnp.zeros((BLK - t, DT_PAD), F32)], axis=0)
        ht_s[...] = sssm_ref[0].T
        ssd = _ssd_steps(lambda i: xc[:, i * LANES:(i + 1) * LANES], dt, -jnp.exp(alog_ref[...]), dskip_ref, ht_s,
                         y_s, 0)

        outs = [None] * N_KV_HEADS

        def emit(kv, o):
            outs[kv] = o

        for _ in _attn_steps(qt_s[c], kwin.astype(BF16), vtwin.astype(BF16), mask_t, sinks_ref, emit,
                             lambda: next(ssd, None)):
            pass
        for _ in ssd:
            pass
        o_t = jnp.concatenate(outs, axis=0)
        lane = lax.broadcasted_iota(jnp.int32, (D_ATTN, BLK), 1)
        attnt_s[c] = jnp.where((lane >= lo) & (lane < lo + t), o_t.astype(BF16), attnt_s[c])
        ssmout_ref[0] = ht_s[...].T
        yn_s[rows, :] = _gated_norm(y_s[0:t, :], z_s[rows, :], normw_ref[...]).astype(BF16)

    @pl.when(i == nb + 1)
    def _():
        for c in range(n // BLK):
            r = slice(c * BLK, (c + 1) * BLK)
            h1_ref[r, :] = _out_proj(x_ref[r, :], attnt_s[c], yn_s[r, :], w_out_ref)


def _mix_sample(x, cache_k, cache_v, state_conv, state_ssm, p):
    nb, t, _ = x.shape
    n = nb * t
    assert BLK % t == 0 and n % BLK == 0
    const2 = lambda i: (0, 0)
    per = lambda i: (jnp.clip(i - 1, 0, nb - 1), 0, 0)
    smem = pl.BlockSpec(memory_space=pltpu.SMEM)
    in_specs = [
        smem, smem,
        pl.BlockSpec((n, D_MODEL), const2),
        pl.BlockSpec((1, WINDOW, D_KV), per),
        pl.BlockSpec((1, WINDOW, D_KV), per),
        pl.BlockSpec((1, CONV_WIDTH - 1, CONV_DIM), per),
        pl.BlockSpec((1, D_SSM, SSM_STATE), per),
        pl.BlockSpec((1, D_MODEL), const2),
        pl.BlockSpec((D_ATTN + D_KV, D_MODEL), const2),
        pl.BlockSpec((D_MODEL, D_NAT), const2),
        pl.BlockSpec((CONV_WIDTH, CONV_DIM), const2),
        pl.BlockSpec((1, CONV_DIM), const2),
        pl.BlockSpec((1, DT_PAD), const2),
        pl.BlockSpec((1, DT_PAD), const2),
        pl.BlockSpec((1, D_SSM), const2),
        pl.BlockSpec((D_MIX, D_MODEL), const2),
    ]
    out_shape = (
        jax.ShapeDtypeStruct((n, D_MODEL), F32),
        jax.ShapeDtypeStruct((n, D_KV), F32),
        jax.ShapeDtypeStruct((n, D_KV), F32),
        jax.ShapeDtypeStruct((nb, CONV_WIDTH - 1, CONV_DIM), F32),
        jax.ShapeDtypeStruct((nb, D_SSM, SSM_STATE), F32),
    )
    out_specs = (
        pl.BlockSpec((n, D_MODEL), const2),
        pl.BlockSpec((n, D_KV), const2),
        pl.BlockSpec((n, D_KV), const2),
        pl.BlockSpec((1, CONV_WIDTH - 1, CONV_DIM), per),
        pl.BlockSpec((1, D_SSM, SSM_STATE), per),
    )
    scratch = [
        pltpu.VMEM((n // BLK, D_ATTN, BLK), BF16),
        pltpu.VMEM((n // BLK, D_KV, BLK), F32),
        pltpu.VMEM((n, D_SSM), F32),
        pltpu.VMEM((n, CONV_DIM), F32),
        pltpu.VMEM((n, DT_PAD), F32),
        pltpu.VMEM((HIST + BLK, CONV_DIM), F32),
        pltpu.VMEM((BLK, D_SSM), F32),
        pltpu.VMEM((n // BLK, D_ATTN, BLK), BF16),
        pltpu.VMEM((n, D_SSM), BF16),
        pltpu.VMEM((SSM_STATE, D_SSM), F32),
    ]
    return pl.pallas_call(
        functools.partial(_mix_sample_kernel, nb=nb, t=t),
        grid=(nb + 2,),
        in_specs=in_specs,
        out_specs=out_specs,
        out_shape=out_shape,
        scratch_shapes=scratch,
        compiler_params=pltpu.CompilerParams(dimension_semantics=("arbitrary",), vmem_limit_bytes=VMEM_LIMIT),
        name="mix_sample",
    )(p["sinks"], p["d_skip"], x.reshape(n, D_MODEL), cache_k, cache_v, state_conv, state_ssm, p["ln1_g"],
      p["w_qvt"], p["w_nat"], p["conv_w"], p["conv_b"], p["dt_bias"], p["a_log"], p["ssm_norm_w"], p["w_out"])


FFN_ROWS = 256


def _ffn_kernel(h_ref, hx_ref, g2_ref, wg_ref, wu_ref, wd_ref, gf_ref, o_ref, ox_ref):
    _ffn_rows(h_ref, g2_ref, wg_ref, wu_ref, wd_ref, gf_ref, o_ref)

    @pl.when(pl.program_id(0) == pl.num_programs(0) - 1)
    def _():
        _ffn_rows(hx_ref, g2_ref, wg_ref, wu_ref, wd_ref, gf_ref, ox_ref)


def _ffn_rows(h_ref, g2_ref, wg_ref, wu_ref, wd_ref, gf_ref, o_ref):
    tm = h_ref.shape[0]
    chunks = [slice(r, min(r + FFN_ROWS, tm)) for r in range(0, tm, FFN_ROWS)]

    def up(rows):
        u = _rms(h_ref[rows, :], g2_ref[...]).astype(BF16)
        return _dot(u, wg_ref[...]), _dot(u, wu_ref[...])

    def down(rows, gate, upv):
        mid = (_silu(gate) * upv).astype(BF16)
        o_ref[rows, :] = _rms(h_ref[rows, :] + _dot(mid, wd_ref[...]), gf_ref[...])

    pending = up(chunks[0])
    for i, rows in enumerate(chunks):
        nxt = up(chunks[i + 1]) if i + 1 < len(chunks) else None
        down(rows, *pending)
        pending = nxt


def _ffn(h, hx, p, tm):
    n, nx = h.shape[0], hx.shape[0]
    d_ff = p["w_gate"].shape[1]
    const2 = lambda i: (0, 0)
    return pl.pallas_call(
        _ffn_kernel,
        grid=(n // tm,),
        in_specs=[
            pl.BlockSpec((tm, D_MODEL), lambda i: (i, 0)),
            pl.BlockSpec((nx, D_MODEL), const2),
            pl.BlockSpec((1, D_MODEL), const2),
            pl.BlockSpec((D_MODEL, d_ff), const2),
            pl.BlockSpec((D_MODEL, d_ff), const2),
            pl.BlockSpec((d_ff, D_MODEL), const2),
            pl.BlockSpec((1, D_MODEL), const2),
        ],
        out_specs=(pl.BlockSpec((tm, D_MODEL), lambda i: (i, 0)), pl.BlockSpec((nx, D_MODEL), const2)),
        out_shape=(jax.ShapeDtypeStruct((n, D_MODEL), F32), jax.ShapeDtypeStruct((nx, D_MODEL), F32)),
        compiler_params=pltpu.CompilerParams(dimension_semantics=("arbitrary",), vmem_limit_bytes=VMEM_LIMIT),
        name="ffn",
    )(h, hx, p["ln2_g"], p["w_gate"], p["w_up"], p["w_down"], p["final_g"])


def _prep_params(ln1_g, w_in, conv_w, conv_b, dt_bias, a_log, d_skip, ssm_norm_w, sinks, w_out, ln2_g, w_gate,
                 w_up, w_down, final_g):
    pad_h = lambda v: jnp.pad(v.reshape(1, SSM_HEADS), ((0, 0), (0, DT_PAD - SSM_HEADS)))
    o_k, o_v, o_z = D_ATTN, D_ATTN + D_KV, D_ATTN + 2 * D_KV
    w_q, w_k, w_v = w_in[:, :o_k], w_in[:, o_k:o_v], w_in[:, o_v:o_z]
    w_zx, w_dt = w_in[:, o_z:o_z + D_SSM + CONV_DIM], w_in[:, o_z + D_SSM + CONV_DIM:]
    w_dt = jnp.pad(w_dt, ((0, 0), (0, DT_PAD - w_dt.shape[1])))
    return {
        "ln1_g": ln1_g.reshape(1, D_MODEL),
        "w_qvt": jnp.concatenate([w_q, w_v], axis=1).T.astype(BF16),
        "w_nat": jnp.concatenate([w_k, w_dt, w_zx], axis=1).astype(BF16),
        "conv_w": conv_w,
        "conv_b": conv_b.reshape(1, CONV_DIM),
        "dt_bias": pad_h(dt_bias),
        "a_log": pad_h(a_log),
        "d_skip": d_skip,
        "ssm_norm_w": ssm_norm_w.reshape(1, D_SSM),
        "sinks": sinks,
        "w_out": w_out.astype(BF16),
        "ln2_g": ln2_g.reshape(1, D_MODEL),
        "w_gate": w_gate.astype(BF16),
        "w_up": w_up.astype(BF16),
        "w_down": w_down.astype(BF16),
        "final_g": final_g.reshape(1, D_MODEL),
    }


def _layer(x_prompt, x_sample, cache_k, cache_v, state_conv, state_ssm, p, ts, tm):
    b, s, _ = x_prompt.shape
    nb, t, _ = x_sample.shape
    h1p, kp, vp, cp, sp = _mix_prompt(x_prompt, p, ts)
    h1s, kn, vn, cn, sn = _mix_sample(x_sample, cache_k.reshape(nb, WINDOW, D_KV), cache_v.reshape(nb, WINDOW, D_KV),
                                      state_conv, state_ssm.reshape(nb, D_SSM, SSM_STATE), p)
    yp, ys = _ffn(h1p.reshape(b * s, D_MODEL), h1s, p, tm)
    yp, ys = yp.reshape(b, s, D_MODEL), ys.reshape(nb, t, D_MODEL)
    return (yp, ys,
            kp.reshape(b, WINDOW, N_KV_HEADS, HEAD_DIM), vp.reshape(b, WINDOW, N_KV_HEADS, HEAD_DIM), cp,
            sp.reshape(b, SSM_HEADS, SSM_HEAD_DIM, SSM_STATE),
            kn.reshape(nb, t, N_KV_HEADS, HEAD_DIM), vn.reshape(nb, t, N_KV_HEADS, HEAD_DIM), cn,
            sn.reshape(nb, SSM_HEADS, SSM_HEAD_DIM, SSM_STATE))


def kernel(x_prompt, x_sample, cache_k, cache_v, state_conv, state_ssm, ln1_g, w_in, conv_w, conv_b, dt_bias, a_log,
           d_skip, ssm_norm_w, sinks, w_out, ln2_g, w_gate, w_up, w_down, final_g):
    assert w_in.shape[0] == 1, "one layer"
    p = _prep_params(ln1_g[0], w_in[0], conv_w[0], conv_b[0], dt_bias[0], a_log[0], d_skip[0], ssm_norm_w[0],
                     sinks[0], w_out[0], ln2_g[0], w_gate[0], w_up[0], w_down[0], final_g)
    outs = _layer(x_prompt, x_sample, cache_k[0], cache_v[0], state_conv[0], state_ssm[0], p, ts=256, tm=1024)
    return tuple(o[None] if i >= 2 else o for i, o in enumerate(outs))
```

```python
import functools

import jax
import jax.numpy as jnp
from jax import lax
from jax.experimental import pallas as pl
from jax.experimental.pallas import tpu as pltpu

F32 = jnp.float32
BF16 = jnp.bfloat16

D_MODEL = 1024
CHUNK = 64
WINDOW = 128
HEAD_DIM = 64
N_HEADS = 16
N_KV_HEADS = 4
Q_PER_KV = N_HEADS // N_KV_HEADS
D_ATTN = N_HEADS * HEAD_DIM
D_KV = N_KV_HEADS * HEAD_DIM
SCALE = HEAD_DIM ** -0.5
SSM_HEADS = 16
SSM_HEAD_DIM = 64
SSM_GROUPS = 4
SSM_STATE = 128
D_SSM = SSM_HEADS * SSM_HEAD_DIM
D_BC = SSM_GROUPS * SSM_STATE
CONV_WIDTH = 4
CONV_DIM = D_SSM + 2 * D_BC
D_MIX = D_ATTN + D_SSM
EPS = 1e-6

LANES = 128
BLK = 2 * CHUNK
DT_PAD = LANES
N_K, N_DT, N_Z, N_X = 0, D_KV, D_KV + DT_PAD, D_KV + DT_PAD + D_SSM
D_NAT = N_X + CONV_DIM
HIST = 8
VMEM_LIMIT = 56 * 1024 * 1024


def _rms(x, g):
    ms = jnp.mean(x * x, axis=-1, keepdims=True)
    return (x * lax.rsqrt(ms + EPS)) * g


def _silu(x):
    h = 0.5 * x
    return h + h * jnp.tanh(h)


def _softplus(x):
    return jnp.maximum(x, 0.0) + jnp.log1p(jnp.exp(-jnp.abs(x)))


def _dot(a, b):
    return jnp.dot(a, b, preferred_element_type=F32)


def _dot_nt(a, b):
    return lax.dot_general(a, b, (((1,), (1,)), ((), ())), preferred_element_type=F32)


def _dot_tn(a, b):
    return lax.dot_general(a, b, (((0,), (0,)), ((), ())), preferred_element_type=F32)


def _left_half(shape):
    return lax.broadcasted_iota(jnp.int32, shape, len(shape) - 1) < HEAD_DIM


def _attn_steps(qt_blk, kwin, vtwin, mask_t, sinks_ref, emit, between=None):
    nq = Q_PER_KV * BLK
    mask4 = jnp.concatenate([mask_t] * Q_PER_KV, axis=1)
    zeros = jnp.zeros((HEAD_DIM, nq), BF16)

    def scores(kv):
        slab, half = kv // 2, kv % 2
        qk = jnp.concatenate([qt_blk[(kv * Q_PER_KV + g) * HEAD_DIM:(kv * Q_PER_KV + g + 1) * HEAD_DIM, :]
                              for g in range(Q_PER_KV)], axis=1)
        rhs = jnp.concatenate([qk, zeros] if half == 0 else [zeros, qk], axis=0)
        return _dot(kwin[:, slab * LANES:(slab + 1) * LANES], rhs)

    s_next = scores(0)
    for kv in range(N_KV_HEADS):
        s = s_next
        if kv + 1 < N_KV_HEADS:
            s_next = scores(kv + 1)
        if between is not None:
            between()
        s = jnp.concatenate([jnp.where(mask4[0:BLK], s[0:BLK], -jnp.inf), s[BLK:BLK + CHUNK],
                             jnp.where(mask4[BLK + CHUNK:], s[BLK + CHUNK:], -jnp.inf)], axis=0)
        sink = jnp.concatenate([jnp.full((1, BLK), sinks_ref[kv * Q_PER_KV + g], F32) for g in range(Q_PER_KV)],
                               axis=1)
        m = jnp.maximum(jnp.max(s, axis=0, keepdims=True), sink)
        e = jnp.exp(s - m)
        denom = jnp.sum(e, axis=0, keepdims=True) + jnp.exp(sink - m)
        p = (e * (1.0 / denom)).astype(BF16)
        o = _dot(vtwin[kv * HEAD_DIM:(kv + 1) * HEAD_DIM, :], p)
        emit(kv, jnp.concatenate([o[:, g * BLK:(g + 1) * BLK] for g in range(Q_PER_KV)], axis=0))
        yield


def _split3(x):
    hi = x.astype(BF16)
    r1 = x - hi.astype(F32)
    mid = r1.astype(BF16)
    lo = (r1 - mid.astype(F32)).astype(BF16)
    return hi, mid, lo


def _ssd_steps(xc, dt, a_row, dskip_ref, ht_ref, y_ref, r0, between=None):
    left = _left_half((BLK, LANES))
    row_i = lax.broadcasted_iota(jnp.int32, (BLK, BLK), 0)
    col_i = lax.broadcasted_iota(jnp.int32, (BLK, BLK), 1)
    tri = col_i <= row_i
    tri_b = tri.astype(BF16)
    da = dt * a_row
    hi, mid, lo = _split3(da)
    parts = _dot(tri_b, jnp.concatenate([hi, mid, lo], axis=1))
    cs = (parts[:, 0:LANES] + parts[:, LANES:2 * LANES]) + parts[:, 2 * LANES:]
    cs_t = cs.T
    if between is not None:
        between()

    def colb(a, h):
        return jnp.broadcast_to(a[:, h:h + 1], (BLK, LANES))

    def group_products(g):
        bg = xc((D_SSM + g * SSM_STATE) // LANES)
        cg = xc((D_SSM + D_BC + g * SSM_STATE) // LANES).astype(BF16)
        cb = _dot_nt(cg, bg.astype(BF16))
        ht_g = ht_ref[:, g * 2 * LANES:(g + 1) * 2 * LANES]
        yoff = _dot(cg, ht_g.astype(BF16))
        return bg, cb, ht_g, yoff

    nxt = group_products(0)
    for g in range(SSM_GROUPS):
        bg, cb, ht_g, yoff = nxt
        gs = slice(g * 2 * LANES, (g + 1) * 2 * LANES)
        if g + 1 < SSM_GROUPS:
            nxt = group_products(g + 1)
        if between is not None:
            between()
        xdec, cdec, late = [], [], []
        for hp in range(2):
            h0 = g * 4 + 2 * hp
            h1 = h0 + 1
            cols = slice((g * 2 + hp) * LANES, (g * 2 + hp + 1) * LANES)
            xs2 = xc(g * 2 + hp)
            csb = [colb(cs, h0), colb(cs, h1)]
            cs2 = jnp.where(left, csb[0], csb[1])
            ecs2 = jnp.exp(cs2)
            dec2 = jnp.exp(cs2[BLK - 1:BLK, :] - cs2)
            dt2 = jnp.where(left, colb(dt, h0), colb(dt, h1))
            xd2 = xs2 * dt2
            ms = []
            for idx, h in enumerate((h0, h1)):
                seg = csb[idx] - jnp.broadcast_to(cs_t[h:h + 1, :], (BLK, BLK))
                ms.append((cb * jnp.exp(jnp.where(tri, seg, -jnp.inf))).astype(BF16))
            xd2b = xd2.astype(BF16)
            zero = jnp.zeros_like(xd2b)
            xd_diag = jnp.concatenate([jnp.where(left, xd2b, zero), jnp.where(left, zero, xd2b)], axis=0)
            yd = _dot(jnp.concatenate(ms, axis=1), xd_diag)
            dsk2 = jnp.where(left, dskip_ref[h0], dskip_ref[h1])
            late.append((cols, yd, yoff[:, hp * LANES:(hp + 1) * LANES] * ecs2, xs2 * dsk2))
            xdec.append((xd2 * dec2).astype(BF16))
            cdec.append(ecs2[BLK - 1:BLK, :])
        xdec = jnp.concatenate(xdec, axis=1)
        cdec = jnp.concatenate(cdec, axis=1)
        ht_ref[:, gs] = cdec * ht_g + _dot(bg.T.astype(BF16), xdec)
        for cols, yd, off, skip in late:
            y_ref[pl.ds(r0, BLK), cols] = (yd + off) + skip
        yield


def _gated_norm(y, z, w):
    g = y * _silu(z)
    parts = []
    width = D_SSM // SSM_GROUPS
    for i in range(SSM_GROUPS):
        gg = g[:, i * width:(i + 1) * width]
        parts.append(gg * lax.rsqrt(jnp.mean(gg * gg, axis=-1, keepdims=True) + EPS))
    return jnp.concatenate(parts, axis=1) * w


def _conv(xbc_ref, rows, conv_w_ref, conv_b_ref):
    acc = conv_b_ref[...]
    for i in range(CONV_WIDTH):
        off = HIST - (CONV_WIDTH - 1) + i
        acc = acc + xbc_ref[off:off + rows, :] * conv_w_ref[i:i + 1, :]
    return _silu(acc)


CONV_PHASES = 4


def _conv_slab(xbc_ref, xc_ref, c, rows, conv_w_ref, conv_b_ref):
    n = rows // CONV_PHASES
    cols = slice(c * LANES, (c + 1) * LANES)
    taps = [conv_w_ref[i:i + 1, cols] for i in range(CONV_WIDTH)]
    bias = conv_b_ref[:, cols]
    lo = -(CONV_WIDTH - 1)
    shifted = [xbc_ref[c, pl.ds(HIST + s, n, stride=CONV_PHASES), :] for s in range(lo, CONV_PHASES)]
    for r in range(CONV_PHASES):
        acc = bias
        for i in range(CONV_WIDTH):
            acc = acc + shifted[r + i] * taps[i]
        xc_ref[c, pl.ds(r, n, stride=CONV_PHASES), :] = _silu(acc)


def _nat_proj(u, w_nat_ref, lo, hi):
    return _dot(u, w_nat_ref[:, lo:hi])


def _out_proj(x, attn_t, yn, w_out_ref):
    return x + (_dot_tn(attn_t, w_out_ref[0:D_ATTN, :]) + _dot(yn, w_out_ref[D_ATTN:, :]))


def _mix_prompt_kernel(sinks_ref, dskip_ref, xp_ref, xo_ref, g1_ref, w_qvt_ref, w_nat_ref, conv_w_ref, conv_b_ref,
                       dtb_ref, alog_ref, normw_ref, w_out_ref,
                       h1_ref, kout_ref, vout_ref, convout_ref, ssmout_ref,
                       qt_s, kh_s, vth_s, z_s, xbc_s, xc_s, dt_s, y_s, attnt_s, yn_s, ht_s, *, ts, nj):
    t = pl.program_id(0)
    sp = t % 2
    sc = 1 - sp
    jp = t % nj
    jc = (t + nj - 1) % nj

    @pl.when(t == 0)
    def _():
        for ref in (qt_s, kh_s, vth_s, z_s, xbc_s, xc_s, dt_s, attnt_s, yn_s):
            ref[...] = jnp.zeros(ref.shape, ref.dtype)

    @pl.when(jc == 0)
    def _():
        ht_s[...] = jnp.zeros(ht_s.shape, F32)

    def project():
        fresh = jp == 0
        kh_s[sp, 0:WINDOW, :] = jnp.where(fresh, 0.0, kh_s[sc, ts:ts + WINDOW, :])
        vth_s[sp, :, 0:WINDOW] = jnp.where(fresh, 0.0, vth_s[sc, :, ts:ts + WINDOW])
        xbc_s[sp, :, 0:HIST, :] = jnp.where(fresh, 0.0, xbc_s[sc, :, ts:ts + HIST, :])
        u = _rms(xp_ref[0], g1_ref[...]).astype(BF16)
        attn_t = attnt_s[sp]
        yn = yn_s[sp]
        n_piece = 4
        slabs = CONV_DIM // LANES // n_piece

        def xbc(c):
            w = CONV_DIM // n_piece
            piece = _nat_proj(u, w_nat_ref, N_X + c * w, N_X + (c + 1) * w)
            for k in range(slabs):
                xbc_s[sp, c * slabs + k, HIST:HIST + ts, :] = piece[:, k * LANES:(k + 1) * LANES]

        def conv(c):
            for k in range(slabs):
                _conv_slab(xbc_s.at[sp], xc_s.at[sp], c * slabs + k, ts, conv_w_ref, conv_b_ref)

        def q(c):
            w = D_ATTN // n_piece
            qt_s[sp, c * w:(c + 1) * w, :] = (_dot_nt(w_qvt_ref[c * w:(c + 1) * w, :], u) * SCALE).astype(BF16)

        def z(c):
            w = D_SSM // n_piece
            z_s[sp, :, c * w:(c + 1) * w] = _nat_proj(u, w_nat_ref, N_Z + c * w, N_Z + (c + 1) * w)

        def out(c):
            w = D_MODEL // n_piece
            cols = slice(c * w, (c + 1) * w)
            h1_ref[0, :, cols] = xo_ref[0, :, cols] + (_dot_tn(attn_t, w_out_ref[0:D_ATTN, cols])
                                                       + _dot(yn, w_out_ref[D_ATTN:, cols]))

        def v():
            vth_s[sp, :, WINDOW:WINDOW + ts] = _dot_nt(w_qvt_ref[D_ATTN:, :], u)

        def k():
            kh_s[sp, WINDOW:WINDOW + ts, :] = _nat_proj(u, w_nat_ref, N_K, N_DT)

        def dt():
            dt_s[sp] = _softplus(_nat_proj(u, w_nat_ref, N_DT, N_Z) + dtb_ref[...])

        placement = [
            [(xbc, 0)], [(xbc, 1), (conv, 0)], [(xbc, 2), (conv, 1)], [(xbc, 3), (conv, 2)],
            [(q, 0)], [(q, 1)], [(q, 2)], [(q, 3)], [(z, 0)],
            [(out, 0), (conv, 3)], [(out, 1)], [(out, 2)], [(out, 3)],
            [(z, 1)], [(z, 2)], [(z, 3)], [(v,)], [(k,)],
            [(dt,)],
        ]
        for group in placement:
            yield
            for fn, *args in group:
                fn(*args)

    pieces = project()
    advance = lambda: next(pieces, None)
    advance()

    a_row = -jnp.exp(alog_ref[...])
    key = lax.broadcasted_iota(jnp.int32, (2 * BLK, BLK), 0)
    qc = lax.broadcasted_iota(jnp.int32, (2 * BLK, BLK), 1) // CHUNK
    band = (key >= qc * CHUNK) & (key < (qc + 3) * CHUNK)
    for p in range(ts // BLK):
        r0 = p * BLK
        first = jnp.where(jc == 0, WINDOW, 0) if p == 0 else 0
        mask_t = band & (key >= first)

        def emit(kv, o, r0=r0):
            attnt_s[sc, kv * Q_PER_KV * HEAD_DIM:(kv + 1) * Q_PER_KV * HEAD_DIM, r0:r0 + BLK] = o.astype(BF16)

        attn = _attn_steps(qt_s[sc, :, r0:r0 + BLK], kh_s[sc, r0:r0 + 2 * BLK, :].astype(BF16),
                           vth_s[sc, :, r0:r0 + 2 * BLK].astype(BF16), mask_t, sinks_ref, emit, advance)
        ssd = _ssd_steps(lambda i, r0=r0: xc_s[sc, i, r0:r0 + BLK, :], dt_s[sc, r0:r0 + BLK, :], a_row, dskip_ref,
                         ht_s, y_s, r0, advance)
        for _ in attn:
            pass
        for _ in ssd:
            pass
    for p in range(ts // BLK):
        rows = slice(p * BLK, (p + 1) * BLK)
        advance()
        yn_s[sc, rows, :] = _gated_norm(y_s[rows, :], z_s[sc, rows, :], normw_ref[...]).astype(BF16)
    for _ in pieces:
        pass

    @pl.when(jc == nj - 1)
    def _():
        kout_ref[0] = kh_s[sc, ts:ts + WINDOW, :]
        vout_ref[0] = vth_s[sc, :, ts:ts + WINDOW].T
        for c in range(CONV_DIM // LANES):
            convout_ref[0, :, c * LANES:(c + 1) * LANES] = xbc_s[sc, c, HIST + ts - (CONV_WIDTH - 1):HIST + ts, :]
        ssmout_ref[0] = ht_s[...].T


def _mix_prompt(x, p, ts):
    b, s, _ = x.shape
    nj = s // ts
    nt = b * nj
    const2 = lambda t: (0, 0)
    prev2 = lambda t: (jnp.maximum(t - 2, 0), 0, 0)
    per_stream = lambda t: (jnp.clip(t - 1, 0, nt - 1) // nj, 0, 0)
    smem = pl.BlockSpec(memory_space=pltpu.SMEM)
    in_specs = [
        smem, smem,
        pl.BlockSpec((1, ts, D_MODEL), lambda t: (jnp.minimum(t, nt - 1), 0, 0)),
        pl.BlockSpec((1, ts, D_MODEL), prev2),
        pl.BlockSpec((1, D_MODEL), const2),
        pl.BlockSpec((D_ATTN + D_KV, D_MODEL), const2),
        pl.BlockSpec((D_MODEL, D_NAT), const2),
        pl.BlockSpec((CONV_WIDTH, CONV_DIM), const2),
        pl.BlockSpec((1, CONV_DIM), const2),
        pl.BlockSpec((1, DT_PAD), const2),
        pl.BlockSpec((1, DT_PAD), const2),
        pl.BlockSpec((1, D_SSM), const2),
        pl.BlockSpec((D_MIX, D_MODEL), const2),
    ]
    out_shape = (
        jax.ShapeDtypeStruct((nt, ts, D_MODEL), F32),
        jax.ShapeDtypeStruct((b, WINDOW, D_KV), F32),
        jax.ShapeDtypeStruct((b, WINDOW, D_KV), F32),
        jax.ShapeDtypeStruct((b, CONV_WIDTH - 1, CONV_DIM), F32),
        jax.ShapeDtypeStruct((b, D_SSM, SSM_STATE), F32),
    )
    out_specs = (
        pl.BlockSpec((1, ts, D_MODEL), prev2),
        pl.BlockSpec((1, WINDOW, D_KV), per_stream),
        pl.BlockSpec((1, WINDOW, D_KV), per_stream),
        pl.BlockSpec((1, CONV_WIDTH - 1, CONV_DIM), per_stream),
        pl.BlockSpec((1, D_SSM, SSM_STATE), per_stream),
    )
    scratch = [
        pltpu.VMEM((2, D_ATTN, ts), BF16),
        pltpu.VMEM((2, WINDOW + ts, D_KV), F32),
        pltpu.VMEM((2, D_KV, WINDOW + ts), F32),
        pltpu.VMEM((2, ts, D_SSM), F32),
        pltpu.VMEM((2, CONV_DIM // LANES, HIST + ts, LANES), F32),
        pltpu.VMEM((2, CONV_DIM // LANES, ts, LANES), F32),
        pltpu.VMEM((2, ts, DT_PAD), F32),
        pltpu.VMEM((ts, D_SSM), F32),
        pltpu.VMEM((2, D_ATTN, ts), BF16),
        pltpu.VMEM((2, ts, D_SSM), BF16),
        pltpu.VMEM((SSM_STATE, D_SSM), F32),
    ]
    x3 = x.reshape(nt, ts, D_MODEL)
    outs = pl.pallas_call(
        functools.partial(_mix_prompt_kernel, ts=ts, nj=nj),
        grid=(nt + 2,),
        in_specs=in_specs,
        out_specs=out_specs,
        out_shape=out_shape,
        scratch_shapes=scratch,
        compiler_params=pltpu.CompilerParams(dimension_semantics=("arbitrary",), vmem_limit_bytes=VMEM_LIMIT),
        name="mix_prompt",
    )(p["sinks"], p["d_skip"], x3, x3, p["ln1_g"], p["w_qvt"], p["w_nat"], p["conv_w"], p["conv_b"], p["dt_bias"],
      p["a_log"], p["ssm_norm_w"], p["w_out"])
    return (outs[0].reshape(b, s, D_MODEL),) + tuple(outs[1:])


def _mix_sample_kernel(sinks_ref, dskip_ref, x_ref, ck_ref, cv_ref, sconv_ref, sssm_ref, g1_ref, w_qvt_ref,
                       w_nat_ref, conv_w_ref, conv_b_ref, dtb_ref, alog_ref, normw_ref, w_out_ref,
                       h1_ref, kout_ref, vout_ref, convout_ref, ssmout_ref,
                       qt_s, vt_s, z_s, xbcall_s, dt_s, xbc_s, y_s, attnt_s, yn_s, ht_s, *, nb, t):
    i = pl.program_id(0)
    n = nb * t
    per_tile = BLK // t

    @pl.when(i == 0)
    def _():
        u = _rms(x_ref[...], g1_ref[...]).astype(BF16)
        qt = (_dot_nt(w_qvt_ref[0:D_ATTN, :], u) * SCALE).astype(BF16)
        vt = _dot_nt(w_qvt_ref[D_ATTN:, :], u)
        for c in range(n // BLK):
            qt_s[c] = qt[:, c * BLK:(c + 1) * BLK]
            vt_s[c] = vt[:, c * BLK:(c + 1) * BLK]
        vout_ref[...] = vt.T
        kout_ref[...] = _nat_proj(u, w_nat_ref, N_K, N_DT)
        z_s[...] = _nat_proj(u, w_nat_ref, N_Z, N_X)
        xbcall_s[...] = _nat_proj(u, w_nat_ref, N_X, D_NAT)
        dt_s[...] = _softplus(_nat_proj(u, w_nat_ref, N_DT, N_Z) + dtb_ref[...])
        attnt_s[...] = jnp.zeros(attnt_s.shape, BF16)
        xbc_s[...] = jnp.zeros(xbc_s.shape, F32)

    @pl.when((i >= 1) & (i <= nb))
    def _():
        b = i - 1
        rows = pl.ds(pl.multiple_of(b * t, t), t)
        c = b // per_tile
        lo = (b % per_tile) * t
        kwin = jnp.concatenate([kout_ref[pl.ds(pl.multiple_of(c * BLK, BLK), BLK), :], ck_ref[0]], axis=0)
        vtwin = jnp.concatenate([vt_s[c], cv_ref[0].T], axis=1)
        key = lax.broadcasted_iota(jnp.int32, (2 * BLK, BLK), 0)
        mask_t = ((key >= lo) & (key < lo + t)) | (key >= BLK)
        xbc_s[HIST - (CONV_WIDTH - 1):HIST, :] = sconv_ref[0]
        xbc_s[HIST:HIST + t, :] = xbcall_s[rows, :]
        convout_ref[0] = xbc_s[HIST + t - (CONV_WIDTH - 1):HIST + t, :]
        live = lax.broadcasted_iota(jnp.int32, (BLK, 1), 0) < t
        xc = jnp.where(live, _conv(xbc_s, BLK, conv_w_ref, conv_b_ref), 0.0)
        dt = jnp.concatenate([dt_s[rows, :], jnp.zeros((BLK - t, DT_PAD), F32)], axis=0)
        ht_s[...] = sssm_ref[0].T
        ssd = _ssd_steps(lambda i: xc[:, i * LANES:(i + 1) * LANES], dt, -jnp.exp(alog_ref[...]), dskip_ref, ht_s,
                         y_s, 0)

        outs = [None] * N_KV_HEADS

        def emit(kv, o):
            outs[kv] = o

        for _ in _attn_steps(qt_s[c], kwin.astype(BF16), vtwin.astype(BF16), mask_t, sinks_ref, emit,
                             lambda: next(ssd, None)):
            pass
        for _ in ssd:
            pass
        o_t = jnp.concatenate(outs, axis=0)
        lane = lax.broadcasted_iota(jnp.int32, (D_ATTN, BLK), 1)
        attnt_s[c] = jnp.where((lane >= lo) & (lane < lo + t), o_t.astype(BF16), attnt_s[c])
        ssmout_ref[0] = ht_s[...].T
        yn_s[rows, :] = _gated_norm(y_s[0:t, :], z_s[rows, :], normw_ref[...]).astype(BF16)

    @pl.when(i == nb + 1)
    def _():
        for c in range(n // BLK):
            r = slice(c * BLK, (c + 1) * BLK)
            h1_ref[r, :] = _out_proj(x_ref[r, :], attnt_s[c], yn_s[r, :], w_out_ref)


def _mix_sample(x, cache_k, cache_v, state_conv, state_ssm, p):
    nb, t, _ = x.shape
    n = nb * t
    assert BLK % t == 0 and n % BLK == 0
    const2 = lambda i: (0, 0)
    per = lambda i: (jnp.clip(i - 1, 0, nb - 1), 0, 0)
    smem = pl.BlockSpec(memory_space=pltpu.SMEM)
    in_specs = [
        smem, smem,
        pl.BlockSpec((n, D_MODEL), const2),
        pl.BlockSpec((1, WINDOW, D_KV), per),
        pl.BlockSpec((1, WINDOW, D_KV), per),
        pl.BlockSpec((1, CONV_WIDTH - 1, CONV_DIM), per),
        pl.BlockSpec((1, D_SSM, SSM_STATE), per),
        pl.BlockSpec((1, D_MODEL), const2),
        pl.BlockSpec((D_ATTN + D_KV, D_MODEL), const2),
        pl.BlockSpec((D_MODEL, D_NAT), const2),
        pl.BlockSpec((CONV_WIDTH, CONV_DIM), const2),
        pl.BlockSpec((1, CONV_DIM), const2),
        pl.BlockSpec((1, DT_PAD), const2),
        pl.BlockSpec((1, DT_PAD), const2),
        pl.BlockSpec((1, D_SSM), const2),
        pl.BlockSpec((D_MIX, D_MODEL), const2),
    ]
    out_shape = (
        jax.ShapeDtypeStruct((n, D_MODEL), F32),
        jax.ShapeDtypeStruct((n, D_KV), F32),
        jax.ShapeDtypeStruct((n, D_KV), F32),
        jax.ShapeDtypeStruct((nb, CONV_WIDTH - 1, CONV_DIM), F32),
        jax.ShapeDtypeStruct((nb, D_SSM, SSM_STATE), F32),
    )
    out_specs = (
        pl.BlockSpec((n, D_MODEL), const2),
        pl.BlockSpec((n, D_KV), const2),
        pl.BlockSpec((n, D_KV), const2),
        pl.BlockSpec((1, CONV_WIDTH - 1, CONV_DIM), per),
        pl.BlockSpec((1, D_SSM, SSM_STATE), per),
    )
    scratch = [
        pltpu.VMEM((n // BLK, D_ATTN, BLK), BF16),
        pltpu.VMEM((n // BLK, D_KV, BLK), F32),
        pltpu.VMEM((n, D_SSM), F32),
        pltpu.VMEM((n, CONV_DIM), F32),
        pltpu.VMEM((n, DT_PAD), F32),
        pltpu.VMEM((HIST + BLK, CONV_DIM), F32),
        pltpu.VMEM((BLK, D_SSM), F32),
        pltpu.VMEM((n // BLK, D_ATTN, BLK), BF16),
        pltpu.VMEM((n, D_SSM), BF16),
        pltpu.VMEM((SSM_STATE, D_SSM), F32),
    ]
    return pl.pallas_call(
        functools.partial(_mix_sample_kernel, nb=nb, t=t),
        grid=(nb + 2,),
        in_specs=in_specs,
        out_specs=out_specs,
        out_shape=out_shape,
        scratch_shapes=scratch,
        compiler_params=pltpu.CompilerParams(dimension_semantics=("arbitrary",), vmem_limit_bytes=VMEM_LIMIT),
        name="mix_sample",
    )(p["sinks"], p["d_skip"], x.reshape(n, D_MODEL), cache_k, cache_v, state_conv, state_ssm, p["ln1_g"],
      p["w_qvt"], p["w_nat"], p["conv_w"], p["conv_b"], p["dt_bias"], p["a_log"], p["ssm_norm_w"], p["w_out"])


FFN_ROWS = 256


def _ffn_kernel(h_ref, hx_ref, g2_ref, wg_ref, wu_ref, wd_ref, gf_ref, o_ref, ox_ref):
    _ffn_rows(h_ref, g2_ref, wg_ref, wu_ref, wd_ref, gf_ref, o_ref)

    @pl.when(pl.program_id(0) == pl.num_programs(0) - 1)
    def _():
        _ffn_rows(hx_ref, g2_ref, wg_ref, wu_ref, wd_ref, gf_ref, ox_ref)


def _ffn_rows(h_ref, g2_ref, wg_ref, wu_ref, wd_ref, gf_ref, o_ref):
    tm = h_ref.shape[0]
    chunks = [slice(r, min(r + FFN_ROWS, tm)) for r in range(0, tm, FFN_ROWS)]

    def up(rows):
        u = _rms(h_ref[rows, :], g2_ref[...]).astype(BF16)
        return _dot(u, wg_ref[...]), _dot(u, wu_ref[...])

    def down(rows, gate, upv):
        mid = (_silu(gate) * upv).astype(BF16)
        o_ref[rows, :] = _rms(h_ref[rows, :] + _dot(mid, wd_ref[...]), gf_ref[...])

    pending = up(chunks[0])
    for i, rows in enumerate(chunks):
        nxt = up(chunks[i + 1]) if i + 1 < len(chunks) else None
        down(rows, *pending)
        pending = nxt


def _ffn(h, hx, p, tm):
    n, nx = h.shape[0], hx.shape[0]
    d_ff = p["w_gate"].shape[1]
    const2 = lambda i: (0, 0)
    return pl.pallas_call(
        _ffn_kernel,
        grid=(n // tm,),
        in_specs=[
            pl.BlockSpec((tm, D_MODEL), lambda i: (i, 0)),
            pl.BlockSpec((nx, D_MODEL), const2),
            pl.BlockSpec((1, D_MODEL), const2),
            pl.BlockSpec((D_MODEL, d_ff), const2),
            pl.BlockSpec((D_MODEL, d_ff), const2),
            pl.BlockSpec((d_ff, D_MODEL), const2),
            pl.BlockSpec((1, D_MODEL), const2),
        ],
        out_specs=(pl.BlockSpec((tm, D_MODEL), lambda i: (i, 0)), pl.BlockSpec((nx, D_MODEL), const2)),
        out_shape=(jax.ShapeDtypeStruct((n, D_MODEL), F32), jax.ShapeDtypeStruct((nx, D_MODEL), F32)),
        compiler_params=pltpu.CompilerParams(dimension_semantics=("arbitrary",), vmem_limit_bytes=VMEM_LIMIT),
        name="ffn",
    )(h, hx, p["ln2_g"], p["w_gate"], p["w_up"], p["w_down"], p["final_g"])


def _prep_params(ln1_g, w_in, conv_w, conv_b, dt_bias, a_log, d_skip, ssm_norm_w, sinks, w_out, ln2_g, w_gate,
                 w_up, w_down, final_g):
    pad_h = lambda v: jnp.pad(v.reshape(1, SSM_HEADS), ((0, 0), (0, DT_PAD - SSM_HEADS)))
    o_k, o_v, o_z = D_ATTN, D_ATTN + D_KV, D_ATTN + 2 * D_KV
    w_q, w_k, w_v = w_in[:, :o_k], w_in[:, o_k:o_v], w_in[:, o_v:o_z]
    w_zx, w_dt = w_in[:, o_z:o_z + D_SSM + CONV_DIM], w_in[:, o_z + D_SSM + CONV_DIM:]
    w_dt = jnp.pad(w_dt, ((0, 0), (0, DT_PAD - w_dt.shape[1])))
    return {
        "ln1_g": ln1_g.reshape(1, D_MODEL),
        "w_qvt": jnp.concatenate([w_q, w_v], axis=1).T.astype(BF16),
        "w_nat": jnp.concatenate([w_k, w_dt, w_zx], axis=1).astype(BF16),
        "conv_w": conv_w,
        "conv_b": conv_b.reshape(1, CONV_DIM),
        "dt_bias": pad_h(dt_bias),
        "a_log": pad_h(a_log),
        "d_skip": d_skip,
        "ssm_norm_w": ssm_norm_w.reshape(1, D_SSM),
        "sinks": sinks,
        "w_out": w_out.astype(BF16),
        "ln2_g": ln2_g.reshape(1, D_MODEL),
        "w_gate": w_gate.astype(BF16),
        "w_up": w_up.astype(BF16),
        "w_down": w_down.astype(BF16),
        "final_g": final_g.reshape(1, D_MODEL),
    }


def _layer(x_prompt, x_sample, cache_k, cache_v, state_conv, state_ssm, p, ts, tm):
    b, s, _ = x_prompt.shape
    nb, t, _ = x_sample.shape
    h1p, kp, vp, cp, sp = _mix_prompt(x_prompt, p, ts)
    h1s, kn, vn, cn, sn = _mix_sample(x_sample, cache_k.reshape(nb, WINDOW, D_KV), cache_v.reshape(nb, WINDOW, D_KV),
                                      state_conv, state_ssm.reshape(nb, D_SSM, SSM_STATE), p)
    yp, ys = _ffn(h1p.reshape(b * s, D_MODEL), h1s, p, tm)
    yp, ys = yp.reshape(b, s, D_MODEL), ys.reshape(nb, t, D_MODEL)
    return (yp, ys,
            kp.reshape(b, WINDOW, N_KV_HEADS, HEAD_DIM), vp.reshape(b, WINDOW, N_KV_HEADS, HEAD_DIM), cp,
            sp.reshape(b, SSM_HEADS, SSM_HEAD_DIM, SSM_STATE),
            kn.reshape(nb, t, N_KV_HEADS, HEAD_DIM), vn.reshape(nb, t, N_KV_HEADS, HEAD_DIM), cn,
            sn.reshape(nb, SSM_HEADS, SSM_HEAD_DIM, SSM_STATE))


def kernel(x_prompt, x_sample, cache_k, cache_v, state_conv, state_ssm, ln1_g, w_in, conv_w, conv_b, dt_bias, a_log,
           d_skip, ssm_norm_w, sinks, w_out, ln2_g, w_gate, w_up, w_down, final_g):
    assert w_in.shape[0] == 1, "one layer"
    p = _prep_params(ln1_g[0], w_in[0], conv_w[0], conv_b[0], dt_bias[0], a_log[0], d_skip[0], ssm_norm_w[0],
                     sinks[0], w_out[0], ln2_g[0], w_gate[0], w_up[0], w_down[0], final_g)
    outs = _layer(x_prompt, x_sample, cache_k[0], cache_v[0], state_conv[0], state_ssm[0], p, ts=256, tm=1024)
    return tuple(o[None] if i >= 2 else o for i, o in enumerate(outs))
```
